```python
import math
import jax, jax.numpy as jnp
from jax import lax
import numpy as np

D_MODEL = 1024
BATCH = 2
SEQ = 8192
DEPTH = 4
DEC_BATCH = 128
DEC_SEQ = 4
PAST_LEN = 8192
PAGE_SIZE = 128

N_MIXERS = 2
N_META = 16
EPS = 1e-6
NEG_INF = -1e30
N_HEADS = 16
HEAD_DIM = D_MODEL // N_HEADS
N_KV_HEADS = 4
GROUP = N_HEADS // N_KV_HEADS
ATT_WIDTH = N_HEADS * HEAD_DIM
KV_WIDTH = N_KV_HEADS * HEAD_DIM
WINDOW = 128
ATT_BLOCK = 128
N_BUCKETS = 32
MAX_DISTANCE = 128
M_PROJ_FACTOR = 2
M_INNER = M_PROJ_FACTOR * D_MODEL
M_HEADS = 4
M_HEAD_DIM = M_INNER // M_HEADS
CONV_WIDTH = 4
MLSTM_CHUNK = 64

kernel_name = 'hybrid_swa_mlstm_meta_decode_step'


def n_layers_of(mixer):
    return len(range(mixer, DEPTH, N_MIXERS))


def rmsnorm(x, w):
    xf = x.astype(jnp.float32)
    y = xf * lax.rsqrt(jnp.mean(xf * xf, axis=-1, keepdims=True) + EPS)
    return (y * w.astype(jnp.float32)).astype(x.dtype)


def rel_bucket(dist):
    n = jnp.maximum(dist, 0)
    max_exact = N_BUCKETS // 2
    nf = jnp.maximum(n, 1).astype(jnp.float32)
    large = max_exact + (jnp.log(nf / max_exact) / math.log(MAX_DISTANCE / max_exact)
                         * (N_BUCKETS - max_exact)).astype(jnp.int32)
    large = jnp.minimum(large, N_BUCKETS - 1)
    return jnp.where(n < max_exact, n, large)


def swa_project(h, w_in, q_norm, k_norm):
    B, T, _ = h.shape
    proj = h @ w_in
    q = proj[..., :ATT_WIDTH].reshape(B, T, N_KV_HEADS, GROUP, HEAD_DIM)
    k = proj[..., ATT_WIDTH:ATT_WIDTH + KV_WIDTH].reshape(B, T, N_KV_HEADS, HEAD_DIM)
    v = proj[..., ATT_WIDTH + KV_WIDTH:ATT_WIDTH + 2 * KV_WIDTH].reshape(B, T, N_KV_HEADS, HEAD_DIM)
    gate = proj[..., ATT_WIDTH + 2 * KV_WIDTH:]
    return rmsnorm(q, q_norm), rmsnorm(k, k_norm), v, gate


def swa_attend(q, k, v, q_pos, k_pos, k_valid, rel_bias, sinks):
    n_blk, t_q = q_pos.shape
    t_k = k_pos.shape[1]
    dist = q_pos[:, :, None] - k_pos[:, None, :]
    mask = k_valid[:, None, :] & (dist >= 0) & ((dist <= WINDOW) | (k_pos[:, None, :] < N_META))
    bias = jnp.transpose(rel_bias.astype(jnp.float32)[rel_bucket(dist)], (0, 3, 1, 2))
    bias = bias.reshape(n_blk, N_KV_HEADS, GROUP, t_q, t_k)
    logits = jnp.einsum('bnqkgd,bnskd->bnkgqs', q, k).astype(jnp.float32) * (HEAD_DIM ** -0.5) + bias[None]
    logits = jnp.where(mask[None, :, None, None], logits, NEG_INF)
    sink = jnp.broadcast_to(sinks.astype(jnp.float32).reshape(1, 1, N_KV_HEADS, GROUP, 1, 1),
                            logits.shape[:-1] + (1,))
    probs = jax.nn.softmax(jnp.concatenate([logits, sink], axis=-1), axis=-1)[..., :-1]
    return jnp.einsum('bnkgqs,bnskd->bnqkgd', probs.astype(v.dtype), v)


def swa_prompt(h, rel_bias, w_in, q_norm, k_norm, sinks, w_out):
    B, T, _ = h.shape
    n_real = T - N_META
    n_blk = n_real // ATT_BLOCK
    q, k, v, gate = swa_project(h, w_in, q_norm, k_norm)
    meta_pos = jnp.arange(N_META, dtype=jnp.int32)
    o_meta = swa_attend(q[:, None, :N_META], k[:, None, :N_META], v[:, None, :N_META],
                        meta_pos[None], meta_pos[None], jnp.ones((1, N_META), bool), rel_bias, sinks)
    qb = q[:, N_META:].reshape(B, n_blk, ATT_BLOCK, N_KV_HEADS, GROUP, HEAD_DIM)

    def band_rows(a):
        cur = a[:, N_META:].reshape(B, n_blk, ATT_BLOCK, N_KV_HEADS, HEAD_DIM)
        prev = jnp.concatenate([jnp.zeros_like(cur[:, :1]), cur[:, :-1]], axis=1)
        meta = jnp.broadcast_to(a[:, None, :N_META], (B, n_blk, N_META, N_KV_HEADS, HEAD_DIM))
        return jnp.concatenate([meta, prev, cur], axis=2)

    blk = jnp.arange(n_blk, dtype=jnp.int32)[:, None]
    q_pos = N_META + blk * ATT_BLOCK + jnp.arange(ATT_BLOCK, dtype=jnp.int32)[None]
    band_pos = N_META + (blk - 1) * ATT_BLOCK + jnp.arange(2 * ATT_BLOCK, dtype=jnp.int32)[None]
    k_pos = jnp.concatenate([jnp.broadcast_to(meta_pos[None], (n_blk, N_META)), band_pos], axis=1)
    k_valid = jnp.concatenate([jnp.ones((n_blk, N_META), bool), band_pos >= N_META], axis=1)
    o_real = swa_attend(qb, band_rows(k), band_rows(v), q_pos, k_pos, k_valid, rel_bias, sinks)
    o = jnp.concatenate([o_meta.reshape(B, N_META, ATT_WIDTH), o_real.reshape(B, n_real, ATT_WIDTH)], axis=1)
    y = (o * jax.nn.silu(gate)) @ w_out
    n_buf = min(WINDOW, T)
    return y, k[:, T - n_buf:], v[:, T - n_buf:], k[:, :N_META], v[:, :N_META]


def swa_sample(h, cache_k, cache_v, meta_k, meta_v, rel_bias, w_in, q_norm, k_norm, sinks, w_out):
    B, T, _ = h.shape
    n_buf = cache_k.shape[1]
    q, k, v, gate = swa_project(h, w_in, q_norm, k_norm)
    win_k = jnp.concatenate([cache_k.astype(k.dtype), k], axis=1)
    win_v = jnp.concatenate([cache_v.astype(v.dtype), v], axis=1)
    keys = jnp.concatenate([meta_k.astype(k.dtype), win_k], axis=1)[:, None]
    vals = jnp.concatenate([meta_v.astype(v.dtype), win_v], axis=1)[:, None]
    buf_pos = PAST_LEN - n_buf + jnp.arange(n_buf, dtype=jnp.int32)
    new_pos = PAST_LEN + jnp.arange(T, dtype=jnp.int32)
    k_pos = jnp.concatenate([jnp.arange(N_META, dtype=jnp.int32), buf_pos, new_pos])[None]
    k_valid = jnp.concatenate([jnp.ones((N_META,), bool), buf_pos >= N_META, jnp.ones((T,), bool)])[None]
    o = swa_attend(q[:, None], keys, vals, new_pos[None], k_pos, k_valid, rel_bias, sinks)
    y = (o.reshape(B, T, ATT_WIDTH) * jax.nn.silu(gate)) @ w_out
    return y, win_k[:, T:], win_v[:, T:]


def mlstm_chunkwise(q, k, v, i_pre, log_f, state, chunk):
    B, T, H, D = q.shape
    n_chunks = T // chunk

    def to_chunks(a):
        return jnp.moveaxis(a.reshape((B, n_chunks, chunk) + a.shape[2:]), 1, 0)

    causal = jnp.tril(jnp.ones((chunk, chunk), bool))

    def step(carry, xs):
        C, n, m = carry
        qc, kc, vc, ic, fc = xs
        b = jnp.cumsum(fc, axis=1)
        g = b[:, -1]
        log_d = b[:, :, None, :] - b[:, None, :, :] + ic[:, None, :, :]
        log_d = jnp.where(causal[None, :, :, None], log_d, -jnp.inf)
        inter = b + m[:, None, :]
        m_t = jnp.maximum(inter, jnp.max(log_d, axis=2))
        w_intra = jnp.exp(log_d - m_t[:, :, None, :])
        w_inter = jnp.exp(inter - m_t)
        s = jnp.einsum('bthd,bjhd->btjh', qc, kc) * w_intra
        num = jnp.einsum('btjh,bjhd->bthd', s, vc) + w_inter[..., None] * jnp.einsum('bthd,bhde->bthe', qc, C)
        den = jnp.sum(s, axis=2) + w_inter * jnp.einsum('bthd,bhd->bth', qc, n)
        h = num / jnp.maximum(jnp.abs(den), jnp.exp(-m_t))[..., None]
        lw = g[:, None, :] - b + ic
        m_new = jnp.maximum(g + m, jnp.max(lw, axis=1))
        wk = jnp.exp(lw - m_new[:, None, :])
        decay = jnp.exp(g + m - m_new)
        C_new = decay[..., None, None] * C + jnp.einsum('bjhd,bjhe->bhde', kc * wk[..., None], vc)
        n_new = decay[..., None] * n + jnp.einsum('bjh,bjhd->bhd', wk, kc)
        return (C_new, n_new, m_new), h

    state, hs = lax.scan(step, state, tuple(to_chunks(a) for a in (q, k, v, i_pre, log_f)))
    return state, jnp.moveaxis(hs, 0, 1).reshape(B, T, H, D)


def mlstm_branch(h, conv_state, cell_state, plan, w_in, conv_w, conv_b, wq, wk, wv,
                 w_gates, b_gates, head_norm, skip, w_out):
    B, T, _ = h.shape
    proj = h @ w_in
    xm, z, o_pre = proj[..., :M_INNER], proj[..., M_INNER:2 * M_INNER], proj[..., 2 * M_INNER:]
    xpad = jnp.concatenate([conv_state.astype(xm.dtype), xm], axis=1)
    xc = jax.nn.silu(sum(xpad[:, w:w + T] * conv_w[w] for w in range(CONV_WIDTH)) + conv_b)
    xc_h = xc.reshape(B, T, M_HEADS, M_HEAD_DIM)
    q = jnp.einsum('bthd,hde->bthe', xc_h, wq)
    k = jnp.einsum('bthd,hde->bthe', xc_h, wk) * (M_HEAD_DIM ** -0.5)
    v = jnp.einsum('bthd,hde->bthe', xm.reshape(B, T, M_HEADS, M_HEAD_DIM), wv)
    gates = (q.reshape(B, T, M_INNER) @ w_gates[:M_INNER]
             + k.reshape(B, T, M_INNER) @ w_gates[M_INNER:2 * M_INNER]
             + v.reshape(B, T, M_INNER) @ w_gates[2 * M_INNER:] + b_gates).astype(jnp.float32)
    i_pre = gates[..., :M_HEADS]
    log_f = jax.nn.log_sigmoid(gates[..., M_HEADS:])
    qf, kf, vf = (a.astype(jnp.float32) for a in (q, k, v))
    state = tuple(s.astype(jnp.float32) for s in cell_state)
    outs, start = [], 0
    for seg_len, chunk in plan:
        seg = slice(start, start + seg_len)
        state, h_seg = mlstm_chunkwise(qf[:, seg], kf[:, seg], vf[:, seg], i_pre[:, seg], log_f[:, seg], state, chunk)
        outs.append(h_seg)
        start += seg_len
    h_cell = jnp.concatenate(outs, axis=1)
    mu = jnp.mean(h_cell, axis=-1, keepdims=True)
    var = jnp.mean(jnp.square(h_cell - mu), axis=-1, keepdims=True)
    h_norm = ((h_cell - mu) * lax.rsqrt(var + EPS)).reshape(B, T, M_INNER) * head_norm.astype(jnp.float32)
    h_out = (jax.nn.sigmoid(o_pre.astype(jnp.float32)) * h_norm).astype(h.dtype)
    y = ((h_out + skip * xc) * jax.nn.silu(z)) @ w_out
    return y, (state[0], state[1], state[2], xpad[:, T:])


def setup_inputs(seed: int = 0) -> dict:
    key = jax.random.key(seed)
    keys = iter(jax.random.split(key, 64))

    def rnd(shape, scale):
        return scale * jax.random.normal(next(keys), shape, jnp.float32)

    n_swa, n_ml = n_layers_of(0), n_layers_of(1)
    n_buf = min(WINDOW, PAST_LEN)
    in_width = 2 * ATT_WIDTH + 2 * KV_WIDTH
    f_bias = jnp.linspace(3.0, 6.0, M_HEADS, dtype=jnp.float32)[None] + rnd((n_ml, M_HEADS), 0.1)
    return {
        'x_prompt': rnd((BATCH, SEQ, D_MODEL), 1.0),
        'x_sample': rnd((DEC_BATCH, DEC_SEQ, D_MODEL), 1.0),
        'cache_swa_k': rnd((n_swa, DEC_BATCH, n_buf, N_KV_HEADS, HEAD_DIM), 1.0),
        'cache_swa_v': rnd((n_swa, DEC_BATCH, n_buf, N_KV_HEADS, HEAD_DIM), 1.0),
        'cache_swa_meta_k': rnd((n_swa, DEC_BATCH, N_META, N_KV_HEADS, HEAD_DIM), 1.0),
        'cache_swa_meta_v': rnd((n_swa, DEC_BATCH, N_META, N_KV_HEADS, HEAD_DIM), 1.0),
        'state_mlstm_c': rnd((n_ml, DEC_BATCH, M_HEADS, M_HEAD_DIM, M_HEAD_DIM), 0.02),
        'state_mlstm_n': rnd((n_ml, DEC_BATCH, M_HEADS, M_HEAD_DIM), 0.02),
        'state_mlstm_m': rnd((n_ml, DEC_BATCH, M_HEADS), 1.0),
        'state_mlstm_conv': rnd((n_ml, DEC_BATCH, CONV_WIDTH - 1, M_INNER), 1.0),
        'meta_tokens': rnd((N_META, D_MODEL), 1.0),
        'rel_bias': rnd((N_BUCKETS, N_HEADS), 0.5),
        'norm_w': 1.0 + rnd((DEPTH, D_MODEL), 0.02),
        'swa_w_in': rnd((n_swa, D_MODEL, in_width), D_MODEL ** -0.5),
        'swa_q_norm': 1.0 + rnd((n_swa, HEAD_DIM), 0.02),
        'swa_k_norm': 1.0 + rnd((n_swa, HEAD_DIM), 0.02),
        'swa_sinks': rnd((n_swa, N_HEADS), 0.5),
        'swa_w_out': rnd((n_swa, ATT_WIDTH, D_MODEL), ATT_WIDTH ** -0.5),
        'mlstm_w_in': rnd((n_ml, D_MODEL, 3 * M_INNER), D_MODEL ** -0.5),
        'mlstm_conv_w': rnd((n_ml, CONV_WIDTH, M_INNER), CONV_WIDTH ** -0.5),
        'mlstm_conv_b': rnd((n_ml, M_INNER), 0.02),
        'mlstm_wq': rnd((n_ml, M_HEADS, M_HEAD_DIM, M_HEAD_DIM), M_HEAD_DIM ** -0.5),
        'mlstm_wk': rnd((n_ml, M_HEADS, M_HEAD_DIM, M_HEAD_DIM), M_HEAD_DIM ** -0.5),
        'mlstm_wv': rnd((n_ml, M_HEADS, M_HEAD_DIM, M_HEAD_DIM), M_HEAD_DIM ** -0.5),
        'mlstm_w_gates': rnd((n_ml, 3 * M_INNER, 2 * M_HEADS), (3 * M_INNER) ** -0.5),
        'mlstm_b_gates': jnp.concatenate([rnd((n_ml, M_HEADS), 0.1), f_bias], axis=-1),
        'mlstm_head_norm': 1.0 + rnd((n_ml, M_INNER), 0.02),
        'mlstm_skip': 1.0 + rnd((n_ml, M_INNER), 0.02),
        'mlstm_w_out': rnd((n_ml, M_INNER, D_MODEL), M_INNER ** -0.5),
    }


def stack_rows(rows, idx):
    return jnp.stack([r[idx] for r in rows])


def reference(x_prompt, x_sample, cache_swa_k, cache_swa_v, cache_swa_meta_k, cache_swa_meta_v,
              state_mlstm_c, state_mlstm_n, state_mlstm_m, state_mlstm_conv,
              meta_tokens, rel_bias, norm_w, swa_w_in, swa_q_norm, swa_k_norm, swa_sinks, swa_w_out,
              mlstm_w_in, mlstm_conv_w, mlstm_conv_b, mlstm_wq, mlstm_wk, mlstm_wv,
              mlstm_w_gates, mlstm_b_gates, mlstm_head_norm, mlstm_skip, mlstm_w_out):
    B, n_real, _ = x_prompt.shape
    meta = jnp.broadcast_to(meta_tokens.astype(x_prompt.dtype)[None], (B, N_META, D_MODEL))
    xp = jnp.concatenate([meta, x_prompt], axis=1)
    xs = x_sample
    prompt_plan = ((N_META, N_META), (n_real, MLSTM_CHUNK))
    sample_plan = ((x_sample.shape[1], x_sample.shape[1]),)
    swa_p, swa_s, ml_p, ml_s = [], [], [], []
    for layer in range(DEPTH):
        j = layer // N_MIXERS
        hp = rmsnorm(xp, norm_w[layer])
        hs = rmsnorm(xs, norm_w[layer])
        if layer % N_MIXERS == 0:
            aw = (swa_w_in[j], swa_q_norm[j], swa_k_norm[j], swa_sinks[j], swa_w_out[j])
            yp, *p_rows = swa_prompt(hp, rel_bias, *aw)
            ys, *s_rows = swa_sample(hs, cache_swa_k[j], cache_swa_v[j], cache_swa_meta_k[j],
                                     cache_swa_meta_v[j], rel_bias, *aw)
            swa_p.append(p_rows)
            swa_s.append(s_rows)
        else:
            mw = (mlstm_w_in[j], mlstm_conv_w[j], mlstm_conv_b[j], mlstm_wq[j], mlstm_wk[j], mlstm_wv[j],
                  mlstm_w_gates[j], mlstm_b_gates[j], mlstm_head_norm[j], mlstm_skip[j], mlstm_w_out[j])
            zero_cell = (jnp.zeros((B, M_HEADS, M_HEAD_DIM, M_HEAD_DIM), jnp.float32),
                         jnp.zeros((B, M_HEADS, M_HEAD_DIM), jnp.float32),
                         jnp.zeros((B, M_HEADS), jnp.float32))
            zero_conv = jnp.zeros((B, CONV_WIDTH - 1, M_INNER), xp.dtype)
            yp, p_rows = mlstm_branch(hp, zero_conv, zero_cell, prompt_plan, *mw)
            ys, s_rows = mlstm_branch(hs, state_mlstm_conv[j],
                                      (state_mlstm_c[j], state_mlstm_n[j], state_mlstm_m[j]), sample_plan, *mw)
            ml_p.append(p_rows)
            ml_s.append(s_rows)
        xp = xp + yp
        xs = xs + ys
    sd = state_mlstm_c.dtype
    return (xp[:, N_META:], xs,
            stack_rows(swa_p, 0), stack_rows(swa_p, 1), stack_rows(swa_p, 2), stack_rows(swa_p, 3),
            stack_rows(ml_p, 0).astype(sd), stack_rows(ml_p, 1).astype(sd), stack_rows(ml_p, 2).astype(sd),
            stack_rows(ml_p, 3),
            stack_rows(swa_s, 0), stack_rows(swa_s, 1),
            stack_rows(ml_s, 0).astype(sd), stack_rows(ml_s, 1).astype(sd), stack_rows(ml_s, 2).astype(sd),
            stack_rows(ml_s, 3))
```

```python
import functools
import math

import jax
import jax.numpy as jnp
from jax import lax
from jax.experimental import pallas as pl
from jax.experimental.pallas import tpu as pltpu

F32 = jnp.float32
BF16 = jnp.bfloat16

D_MODEL = 1024
N_META = 16
EPS = 1e-6
NEG_INF = -1e30
N_HEADS = 16
HEAD_DIM = 64
N_KV = 4
GROUP = 4
ATT_W = N_HEADS * HEAD_DIM
KV_W = N_KV * HEAD_DIM
WINDOW = 128
ATT_BLOCK = 128
N_BUCKETS = 32
MAX_DISTANCE = 128
M_INNER = 2048
M_HEADS = 4
M_HD = 512
CONV_W = 4
PAST_LEN = 8192

VMEM_LIMIT_BYTES = 56 * 1024 * 1024
PROMPT_CHUNK = 256
SMALL_CHUNK = 16
ROW_TILE = 256
SMALL_ROW_TILE = 128
SAMPLE_ATT_GROUP = 8


def _cparams(*sem):
    return pltpu.CompilerParams(dimension_semantics=sem, vmem_limit_bytes=VMEM_LIMIT_BYTES)


def _rms(x, w):
    ms = jnp.mean(x * x, axis=-1, keepdims=True)
    return x * lax.rsqrt(ms + EPS) * w


def _sigmoid(x):
    return 1.0 / (1.0 + jnp.exp(-x))


def _silu(x):
    return x * _sigmoid(x)


def _dot(a, b):
    return jnp.dot(a, b, preferred_element_type=F32)


def _dot_nt(a, b):
    return lax.dot_general(a, b, (((1,), (1,)), ((), ())), preferred_element_type=F32)


def _dot_tn(a, b):
    return lax.dot_general(a, b, (((0,), (0,)), ((), ())), preferred_element_type=F32)


def _norm_proj_kernel(x_ref, nw_ref, w_ref, o_ref):
    h = _rms(x_ref[...], nw_ref[...]).astype(BF16)
    o_ref[...] = _dot(h, w_ref[...]).astype(o_ref.dtype)


def _norm_proj(x, nw, w, tm, tn):
    rows, d = x.shape
    n = w.shape[1]
    return pl.pallas_call(
        _norm_proj_kernel,
        grid=(n // tn, rows // tm),
        in_specs=[pl.BlockSpec((tm, d), lambda j, i: (i, 0)),
                  pl.BlockSpec((1, d), lambda j, i: (0, 0)),
                  pl.BlockSpec((d, tn), lambda j, i: (0, j))],
        out_specs=pl.BlockSpec((tm, tn), lambda j, i: (i, j)),
        out_shape=jax.ShapeDtypeStruct((rows, n), BF16),
        compiler_params=_cparams("parallel", "parallel"),
        name="norm_proj",
    )(x, nw, w)


def _swa_proj_kernel(x_ref, nw_ref, w_ref, qn_ref, kn_ref, q_ref, k_ref, v_ref, g_ref):
    h = _rms(x_ref[...], nw_ref[...]).astype(BF16)
    proj = _dot(h, w_ref[...])
    qn = qn_ref[...]
    kn = kn_ref[...]
    for hd in range(N_HEADS):
        sl = proj[:, hd * HEAD_DIM:(hd + 1) * HEAD_DIM]
        q_ref[:, hd * HEAD_DIM:(hd + 1) * HEAD_DIM] = _rms(sl, qn).astype(q_ref.dtype)
    for hd in range(N_KV):
        sl = proj[:, ATT_W + hd * HEAD_DIM:ATT_W + (hd + 1) * HEAD_DIM]
        k_ref[:, hd * HEAD_DIM:(hd + 1) * HEAD_DIM] = _rms(sl, kn)
    v_ref[...] = proj[:, ATT_W + KV_W:ATT_W + 2 * KV_W]
    g_ref[...] = _silu(proj[:, ATT_W + 2 * KV_W:])


def _swa_proj(x, nw, w, qn, kn, tm):
    rows, d = x.shape
    n = w.shape[1]
    row = lambda i: (i, 0)
    fixed = lambda i: (0, 0)
    return pl.pallas_call(
        _swa_proj_kernel,
        grid=(rows // tm,),
        in_specs=[pl.BlockSpec((tm, d), row), pl.BlockSpec((1, d), fixed), pl.BlockSpec((d, n), fixed),
                  pl.BlockSpec((1, HEAD_DIM), fixed), pl.BlockSpec((1, HEAD_DIM), fixed)],
        out_specs=[pl.BlockSpec((tm, ATT_W), row), pl.BlockSpec((tm, KV_W), row),
                   pl.BlockSpec((tm, KV_W), row), pl.BlockSpec((tm, ATT_W), row)],
        out_shape=[jax.ShapeDtypeStruct((rows, ATT_W), BF16), jax.ShapeDtypeStruct((rows, KV_W), F32),
                   jax.ShapeDtypeStruct((rows, KV_W), F32), jax.ShapeDtypeStruct((rows, ATT_W), F32)],
        compiler_params=_cparams("parallel"),
        name="swa_proj",
    )(x, nw, w, qn, kn)


def _resid_matmul_kernel(u_ref, w_ref, x_ref, o_ref):
    o_ref[...] = x_ref[...] + _dot(u_ref[...], w_ref[...])


def _gated_resid_matmul_kernel(a_ref, g_ref, w_ref, x_ref, o_ref):
    u = (a_ref[...] * g_ref[...]).astype(BF16)
    o_ref[...] = x_ref[...] + _dot(u, w_ref[...])


def _resid_matmul(u, w, x, tm, gate=None):
    rows, k = u.shape
    d = w.shape[1]
    row = lambda i: (i, 0)
    ins = [u] if gate is None else [u, gate]
    body = _resid_matmul_kernel if gate is None else _gated_resid_matmul_kernel
    return pl.pallas_call(
        body,
        grid=(rows // tm,),
        in_specs=[pl.BlockSpec((tm, k), row)] * len(ins)
                 + [pl.BlockSpec((k, d), lambda i: (0, 0)), pl.BlockSpec((tm, d), row)],
        out_specs=pl.BlockSpec((tm, d), row),
        out_shape=jax.ShapeDtypeStruct((rows, d), F32),
        compiler_params=_cparams("parallel"),
        name="resid_matmul",
    )(*ins, w, x)


def _bias_table_kernel(idx_ref, tab_ref, o_ref):
    idx = idx_ref[...]
    for hd in range(N_HEADS):
        acc = jnp.zeros(idx.shape, F32)
        for b in range(N_BUCKETS):
            acc = jnp.where(idx == b, tab_ref[b, hd], acc)
        o_ref[hd] = acc


def _bias_table(idx, rel_bias):
    r, c = idx.shape
    return pl.pallas_call(
        _bias_table_kernel,
        in_specs=[pl.BlockSpec((r, c), lambda: (0, 0)), pl.BlockSpec(memory_space=pltpu.SMEM)],
        out_specs=pl.BlockSpec((N_HEADS, r, c), lambda: (0, 0, 0)),
        out_shape=jax.ShapeDtypeStruct((N_HEADS, r, c), F32),
        name="bias_table",
    )(idx, rel_bias)


def _rel_bucket(dist):
    n = jnp.maximum(dist, 0)
    max_exact = N_BUCKETS // 2
    nf = jnp.maximum(n, 1).astype(F32)
    large = max_exact + (jnp.log(nf / max_exact) / math.log(MAX_DISTANCE / max_exact)
                         * (N_BUCKETS - max_exact)).astype(jnp.int32)
    large = jnp.minimum(large, N_BUCKETS - 1)
    return jnp.where(n < max_exact, n, large)


def _mask_and_bucket(q_pos, k_pos, k_valid):
    dist = q_pos[:, None] - k_pos[None, :]
    mask = k_valid[None, :] & (dist >= 0) & ((dist <= WINDOW) | (k_pos[None, :] < N_META))
    return mask, _rel_bucket(dist)


def _swa_prompt_kernel(q_ref, kc_ref, kp_ref, vc_ref, vp_ref, mk_ref, mv_ref, g_ref, x_ref,
                       bias_ref, mask_ref, sink_ref, w_ref, o_ref, att_ref):
    kband = jnp.concatenate([mk_ref[...], kp_ref[...], kc_ref[...]], axis=0).astype(BF16)
    vband = jnp.concatenate([mv_ref[...], vp_ref[...], vc_ref[...]], axis=0).astype(BF16)
    q = q_ref[...]
    keep = mask_ref[...] > 0.5
    sink = sink_ref[...]
    scale = HEAD_DIM ** -0.5
    for kv in range(N_KV):
        q4 = jnp.concatenate(
            [q[:, (kv * GROUP + g) * HEAD_DIM:(kv * GROUP + g + 1) * HEAD_DIM] for g in range(GROUP)], axis=0)
        kk = kband[:, kv * HEAD_DIM:(kv + 1) * HEAD_DIM]
        vv = vband[:, kv * HEAD_DIM:(kv + 1) * HEAD_DIM]
        logits = _dot_nt(q4, kk) * scale + bias_ref[kv]
        logits = jnp.where(keep, logits, NEG_INF)
        sk = sink[kv]
        m = jnp.maximum(jnp.max(logits, axis=-1, keepdims=True), sk)
        p = jnp.exp(logits - m)
        den = jnp.sum(p, axis=-1, keepdims=True) + jnp.exp(sk - m)
        o4 = _dot(p.astype(BF16), vv) / den
        for g in range(GROUP):
            hd = kv * GROUP + g
            att_ref[:, hd * HEAD_DIM:(hd + 1) * HEAD_DIM] = o4[g * ATT_BLOCK:(g + 1) * ATT_BLOCK]
    u = (att_ref[...] * g_ref[...]).astype(BF16)
    o_ref[...] = x_ref[...] + _dot(u, w_ref[...])


def _swa_prompt(qn, kn, vn, sg, x, mk, mv, bias, mask, sink, w_out, batch, n_blk):
    cur = lambda b, i: (b * n_blk + i, 0)
    prev = lambda b, i: (b * n_blk + jnp.maximum(i - 1, 0), 0)
    meta = lambda b, i: (b, 0, 0)
    first = lambda b, i: (jnp.minimum(i, 1), 0, 0, 0)
    tk = N_META + 2 * ATT_BLOCK
    rows = qn.shape[0]
    return pl.pallas_call(
        _swa_prompt_kernel,
        grid=(batch, n_blk),
        in_specs=[pl.BlockSpec((ATT_BLOCK, ATT_W), cur),
                  pl.BlockSpec((ATT_BLOCK, KV_W), cur), pl.BlockSpec((ATT_BLOCK, KV_W), prev),
                  pl.BlockSpec((ATT_BLOCK, KV_W), cur), pl.BlockSpec((ATT_BLOCK, KV_W), prev),
                  pl.BlockSpec((None, N_META, KV_W), meta), pl.BlockSpec((None, N_META, KV_W), meta),
                  pl.BlockSpec((ATT_BLOCK, ATT_W), cur), pl.BlockSpec((ATT_BLOCK, D_MODEL), cur),
                  pl.BlockSpec((None, N_KV, GROUP * ATT_BLOCK, tk), first),
                  pl.BlockSpec((None, GROUP * ATT_BLOCK, tk), lambda b, i: (jnp.minimum(i, 1), 0, 0)),
                  pl.BlockSpec((N_KV, GROUP * ATT_BLOCK, 1), lambda b, i: (0, 0, 0)),
                  pl.BlockSpec((ATT_W, D_MODEL), lambda b, i: (0, 0))],
        out_specs=pl.BlockSpec((ATT_BLOCK, D_MODEL), cur),
        out_shape=jax.ShapeDtypeStruct((rows, D_MODEL), F32),
        scratch_shapes=[pltpu.VMEM((ATT_BLOCK, ATT_W), F32)],
        compiler_params=_cparams("parallel", "parallel"),
        name="swa_prompt",
    )(qn, kn, kn, vn, vn, mk, mv, sg, x, bias, mask, sink, w_out)


def _swa_small_kernel(n_seg, group, *refs):
    q_ref = refs[0]
    k_refs = refs[1:1 + n_seg]
    v_refs = refs[1 + n_seg:1 + 2 * n_seg]
    bias_ref, mask_ref, sink_ref, o_ref = refs[1 + 2 * n_seg:]
    keep = mask_ref[...] > 0.5
    scale = HEAD_DIM ** -0.5
    for b in range(group):
        kall = jnp.concatenate([r[b] for r in k_refs], axis=0).astype(BF16) if n_seg > 1 else k_refs[0][b].astype(BF16)
        vall = jnp.concatenate([r[b] for r in v_refs], axis=0).astype(BF16) if n_seg > 1 else v_refs[0][b].astype(BF16)
        for kv in range(N_KV):
            kk = kall[:, kv * HEAD_DIM:(kv + 1) * HEAD_DIM]
            vv = vall[:, kv * HEAD_DIM:(kv + 1) * HEAD_DIM]
            logits = _dot_nt(q_ref[b, kv], kk) * scale + bias_ref[kv]
            logits = jnp.where(keep, logits, NEG_INF)
            sk = sink_ref[kv]
            m = jnp.maximum(jnp.max(logits, axis=-1, keepdims=True), sk)
            p = jnp.exp(logits - m)
            den = jnp.sum(p, axis=-1, keepdims=True) + jnp.exp(sk - m)
            o_ref[b, kv] = _dot(p.astype(BF16), vv) / den


def _swa_small(q, k_segs, v_segs, bias, mask, sink, group):
    nb, _, rq, _ = q.shape
    n_seg = len(k_segs)
    tk = sum(s.shape[1] for s in k_segs)
    seg_specs = [pl.BlockSpec((group, s.shape[1], KV_W), lambda i: (i, 0, 0)) for s in k_segs]
    return pl.pallas_call(
        functools.partial(_swa_small_kernel, n_seg, group),
        grid=(nb // group,),
        in_specs=[pl.BlockSpec((group, N_KV, rq, HEAD_DIM), lambda i: (i, 0, 0, 0))] + seg_specs + seg_specs
                 + [pl.BlockSpec((N_KV, rq, tk), lambda i: (0, 0, 0)),
                    pl.BlockSpec((rq, tk), lambda i: (0, 0)),
                    pl.BlockSpec((N_KV, rq, 1), lambda i: (0, 0, 0))],
        out_specs=pl.BlockSpec((group, N_KV, rq, HEAD_DIM), lambda i: (i, 0, 0, 0)),
        out_shape=jax.ShapeDtypeStruct((nb, N_KV, rq, HEAD_DIM), F32),
        compiler_params=_cparams("parallel"),
        name="swa_small",
    )(q, *k_segs, *v_segs, bias, mask, sink)


def _to_group_rows(a, nb, t):
    a = a.reshape(nb, t, N_KV, GROUP, HEAD_DIM)
    return jnp.transpose(a, (0, 2, 3, 1, 4)).reshape(nb, N_KV, GROUP * t, HEAD_DIM)


def _from_group_rows(a, nb, t):
    a = a.reshape(nb, N_KV, GROUP, t, HEAD_DIM)
    return jnp.transpose(a, (0, 3, 1, 2, 4)).reshape(nb * t, ATT_W)


def _bias_group_rows(b, t):
    return b.reshape(N_KV, GROUP * t, b.shape[-1])


def _qkv_body(xs, cw_ref, cb_ref, wq_ref, wk_ref, wv_ref, wg_ref, bg_ref, q_ref, k_ref, v_ref, xc_ref, gt_ref):
    cw = cw_ref[...]
    acc = xs[0] * cw[0:1]
    for w in range(1, CONV_W):
        acc = acc + xs[w] * cw[w:w + 1]
    xc = _silu(acc + cb_ref[...])
    xcb = xc.astype(BF16)
    xmb = xs[CONV_W - 1].astype(BF16)
    xc_ref[...] = xcb
    gates = bg_ref[...]
    for h in range(M_HEADS):
        sl = slice(h * M_HD, (h + 1) * M_HD)
        qh = _dot(xcb[:, sl], wq_ref[h]).astype(BF16)
        kh = (_dot(xcb[:, sl], wk_ref[h]) * (M_HD ** -0.5)).astype(BF16)
        vh = _dot(xmb[:, sl], wv_ref[h]).astype(BF16)
        q_ref[:, sl] = qh
        k_ref[:, sl] = kh
        v_ref[:, sl] = vh
        gates = (gates + _dot(qh, wg_ref[h * M_HD:(h + 1) * M_HD])
                 + _dot(kh, wg_ref[M_INNER + h * M_HD:M_INNER + (h + 1) * M_HD])
                 + _dot(vh, wg_ref[2 * M_INNER + h * M_HD:2 * M_INNER + (h + 1) * M_HD]))
    lane = lax.broadcasted_iota(jnp.int32, gates.shape, 1)
    log_f = jnp.minimum(gates, 0.0) - jnp.log1p(jnp.exp(-jnp.abs(gates)))
    gt_ref[...] = jnp.where(lane < M_HEADS, gates, log_f)


def _qkv_halo_kernel(xm_ref, prev_ref, cs_ref, *rest):
    i = pl.program_id(1)
    cur = xm_ref[...].astype(F32)
    halo = jnp.where(i == 0, cs_ref[...], prev_ref[...].astype(F32)[8:])
    full = jnp.concatenate([halo, cur], axis=0)
    tm = cur.shape[0]
    off = 8 - (CONV_W - 1)
    xs = [full[off + w:off + w + tm] for w in range(CONV_W - 1)] + [cur]
    _qkv_body(xs, *rest)


def _qkv_shift_kernel(x0_ref, x1_ref, x2_ref, x3_ref, *rest):
    _qkv_body([x0_ref[...], x1_ref[...], x2_ref[...], x3_ref[...]], *rest)


def _qkv_weight_specs(fixed2, fixed3):
    return [pl.BlockSpec((CONV_W, M_INNER), fixed2), pl.BlockSpec((1, M_INNER), fixed2),
            pl.BlockSpec((M_HEADS, M_HD, M_HD), fixed3), pl.BlockSpec((M_HEADS, M_HD, M_HD), fixed3),
            pl.BlockSpec((M_HEADS, M_HD, M_HD), fixed3),
            pl.BlockSpec((3 * M_INNER, 2 * M_HEADS), fixed2), pl.BlockSpec((1, 2 * M_HEADS), fixed2)]


def _qkv_real(proj3, cs, wts, tm):
    b, t, _ = proj3.shape
    cur = lambda bi, i: (bi, i, 0)
    outs = [jax.ShapeDtypeStruct((b, t, M_INNER), BF16)] * 4 + [jax.ShapeDtypeStruct((b, t, 2 * M_HEADS), F32)]
    return pl.pallas_call(
        _qkv_halo_kernel,
        grid=(b, t // tm),
        in_specs=[pl.BlockSpec((None, tm, M_INNER), cur),
                  pl.BlockSpec((None, 16, M_INNER), lambda bi, i: (bi, jnp.maximum(i * (tm // 16) - 1, 0), 0)),
                  pl.BlockSpec((None, 8, M_INNER), lambda bi, i: (bi, 0, 0))]
                 + _qkv_weight_specs(lambda bi, i: (0, 0), lambda bi, i: (0, 0, 0)),
        out_specs=[pl.BlockSpec((None, tm, M_INNER), cur)] * 4 + [pl.BlockSpec((None, tm, 2 * M_HEADS), cur)],
        out_shape=outs,
        compiler_params=_cparams("parallel", "parallel"),
        name="mlstm_qkv_real",
    )(proj3, proj3, cs, *wts)


def _qkv_small(xs, wts, tm):
    rows = xs[0].shape[0]
    row = lambda i: (i, 0)
    outs = [jax.ShapeDtypeStruct((rows, M_INNER), BF16)] * 4 + [jax.ShapeDtypeStruct((rows, 2 * M_HEADS), F32)]
    return pl.pallas_call(
        _qkv_shift_kernel,
        grid=(rows // tm,),
        in_specs=[pl.BlockSpec((tm, M_INNER), row)] * CONV_W
                 + _qkv_weight_specs(lambda i: (0, 0), lambda i: (0, 0, 0)),
        out_specs=[pl.BlockSpec((tm, M_INNER), row)] * 4 + [pl.BlockSpec((tm, 2 * M_HEADS), row)],
        out_shape=outs,
        compiler_params=_cparams("parallel"),
        name="mlstm_qkv_small",
    )(*xs, *wts)


def _chunk_kernel(valid, q_ref, k_ref, v_ref, xc_ref, z_ref, op_ref, gc_ref, gr_ref, c0_ref, n0_ref, m0_ref,
                  hn_ref, sk_ref, u_ref, c_ref, n_ref, m_ref):
    c_idx = pl.program_id(2)
    L = q_ref.shape[0]

    @pl.when(c_idx == 0)
    def _():
        c_ref[...] = c0_ref[...]
        n_ref[...] = n0_ref[...]
        m_ref[...] = m0_ref[...]

    gc = gc_ref[...]
    gr = gr_ref[...]
    col_c = lax.broadcasted_iota(jnp.int32, gc.shape, 1)
    row_r = lax.broadcasted_iota(jnp.int32, gr.shape, 0)
    lf_c = jnp.where(col_c == 1, gc, 0.0)
    lf_r = jnp.where(row_r == 1, gr, 0.0)
    i_c = gc[:, 0:1]
    i_r = gr[0:1, :]
    if valid < L:
        lf_c = jnp.where(lax.broadcasted_iota(jnp.int32, gc.shape, 0) < valid, lf_c, 0.0)
        lf_r = jnp.where(lax.broadcasted_iota(jnp.int32, gr.shape, 1) < valid, lf_r, 0.0)
        i_c = jnp.where(lax.broadcasted_iota(jnp.int32, (L, 1), 0) < valid, i_c, -jnp.inf)
        i_r = jnp.where(lax.broadcasted_iota(jnp.int32, (1, L), 1) < valid, i_r, -jnp.inf)
    t_i = lax.broadcasted_iota(jnp.int32, (L, L), 0)
    j_i = lax.broadcasted_iota(jnp.int32, (L, L), 1)
    causal = j_i <= t_i
    tril = jnp.where(causal, 1.0, 0.0).astype(F32)
    triu = jnp.where(t_i <= j_i, 1.0, 0.0).astype(F32)
    b_c = jnp.dot(tril, lf_c, precision=lax.Precision.HIGHEST, preferred_element_type=F32)[:, 1:2]
    b_r = jnp.dot(lf_r, triu, precision=lax.Precision.HIGHEST, preferred_element_type=F32)[1:2, :]
    g = b_c[L - 1:L, :]
    m_prev = m_ref[:, 0:1]
    c_prev = c_ref[...]
    n_prev = n_ref[...]

    log_d = jnp.where(causal, b_c + (i_r - b_r), -jnp.inf)
    inter = b_c + m_prev
    m_t = jnp.maximum(inter, jnp.max(log_d, axis=-1, keepdims=True))
    w_intra = jnp.exp(log_d - m_t)
    w_inter = jnp.exp(inter - m_t)
    q = q_ref[...]
    k = k_ref[...]
    v = v_ref[...]
    s = _dot_nt(q, k) * w_intra
    qf = q.astype(F32)
    num = _dot(s.astype(BF16), v) + w_inter * _dot(q, c_prev.astype(BF16))
    den = jnp.sum(s, axis=-1, keepdims=True) + w_inter * jnp.sum(qf * n_prev, axis=-1, keepdims=True)
    h = num * (1.0 / jnp.maximum(jnp.abs(den), jnp.exp(-m_t)))

    lw = (g - b_c) + i_c
    m_new = jnp.maximum(g + m_prev, jnp.max(lw, axis=0, keepdims=True))
    wk = jnp.exp(lw - m_new)
    decay = jnp.exp(g + m_prev - m_new)
    kw = k.astype(F32) * wk
    c_ref[...] = decay * c_prev + _dot_tn(kw.astype(BF16), v)
    n_ref[...] = decay * n_prev + jnp.sum(kw, axis=0, keepdims=True)
    m_ref[...] = jnp.broadcast_to(m_new, m_ref.shape)

    mu = jnp.mean(h, axis=-1, keepdims=True)
    hc = h - mu
    var = jnp.mean(hc * hc, axis=-1, keepdims=True)
    h_out = _sigmoid(op_ref[...].astype(F32)) * (hc * lax.rsqrt(var + EPS) * hn_ref[...])
    u = (h_out + sk_ref[...] * xc_ref[...].astype(F32)) * _silu(z_ref[...].astype(F32))
    u_ref[...] = u.astype(u_ref.dtype)


def _mlstm_chunk(q, k, v, xc, proj3, gc, gr, c0, n0, m0, head_norm, skip, chunk, valid):
    b, t, _ = q.shape
    nc = t // chunk
    tok = lambda bi, h, c: (bi, c, h)
    st = lambda bi, h, c: (bi, h, 0, 0)
    hd_blk = pl.BlockSpec((None, chunk, M_HD), tok)
    c_blk = pl.BlockSpec((None, None, M_HD, M_HD), st)
    n_blk = pl.BlockSpec((None, None, 1, M_HD), st)
    m_blk = pl.BlockSpec((None, None, 1, 128), st)
    return pl.pallas_call(
        functools.partial(_chunk_kernel, valid),
        grid=(b, M_HEADS, nc),
        in_specs=[hd_blk, hd_blk, hd_blk, hd_blk,
                  pl.BlockSpec((None, chunk, M_HD), lambda bi, h, c: (bi, c, M_HEADS + h)),
                  pl.BlockSpec((None, chunk, M_HD), lambda bi, h, c: (bi, c, 2 * M_HEADS + h)),
                  pl.BlockSpec((None, None, chunk, 8), lambda bi, h, c: (bi, h, c, 0)),
                  pl.BlockSpec((None, None, 8, chunk), lambda bi, h, c: (bi, h, 0, c)),
                  c_blk, n_blk, m_blk,
                  pl.BlockSpec((1, M_HD), lambda bi, h, c: (0, h)),
                  pl.BlockSpec((1, M_HD), lambda bi, h, c: (0, h))],
        out_specs=[hd_blk, c_blk, n_blk, m_blk],
        out_shape=[jax.ShapeDtypeStruct((b, t, M_INNER), BF16),
                   jax.ShapeDtypeStruct(c0.shape, F32), jax.ShapeDtypeStruct(n0.shape, F32),
                   jax.ShapeDtypeStruct(m0.shape, F32)],
        compiler_params=_cparams("parallel", "parallel", "arbitrary"),
        name="mlstm_chunk",
    )(q, k, v, xc, proj3, proj3, gc, gr, c0, n0, m0, head_norm, skip)


def _gate_layouts(gates, b, t):
    g = gates.reshape(b, t, 2, M_HEADS)
    gc = jnp.transpose(g, (0, 3, 1, 2))
    gr = jnp.transpose(g, (0, 3, 2, 1))
    return (jnp.pad(gc, ((0, 0), (0, 0), (0, 0), (0, 6))), jnp.pad(gr, ((0, 0), (0, 0), (0, 6), (0, 0))))


def _pad_t(a, t_to):
    return jnp.pad(a, ((0, 0), (0, t_to - a.shape[1])) + ((0, 0),) * (a.ndim - 2))


def kernel(x_prompt, x_sample, cache_swa_k, cache_swa_v, cache_swa_meta_k, cache_swa_meta_v, state_mlstm_c, state_mlstm_n, state_mlstm_m, state_mlstm_conv, meta_tokens, rel_bias, norm_w, swa_w_in, swa_q_norm, swa_k_norm, swa_sinks, swa_w_out, mlstm_w_in, mlstm_conv_w, mlstm_conv_b, mlstm_wq, mlstm_wk, mlstm_wv, mlstm_w_gates, mlstm_b_gates, mlstm_head_norm, mlstm_skip, mlstm_w_out):
    B, T, D = x_prompt.shape
    DB, DT, _ = x_sample.shape
    depth = norm_w.shape[0]
    n_buf = cache_swa_k.shape[2]
    n_blk = T // ATT_BLOCK
    n_real, n_samp, n_meta = B * T, DB * DT, B * N_META
    n_small = -(-(n_samp + n_meta) // SMALL_ROW_TILE) * SMALL_ROW_TILE
    pad_small = n_small - n_samp - n_meta
    s_meta = slice(n_samp, n_samp + n_meta)

    def small_rows(samp, meta):
        return jnp.concatenate([samp, meta, jnp.zeros((pad_small,) + samp.shape[1:], samp.dtype)], axis=0)

    xr = x_prompt.reshape(n_real, D)
    xs = small_rows(x_sample.reshape(n_samp, D),
                    jnp.broadcast_to(meta_tokens.astype(x_prompt.dtype)[None], (B, N_META, D)).reshape(n_meta, D))

    ar = lambda n: jnp.arange(n, dtype=jnp.int32)
    meta_pos = ar(N_META)
    geo = []
    for blk in (0, 1):
        q_pos = N_META + blk * ATT_BLOCK + ar(ATT_BLOCK)
        band_pos = N_META + (blk - 1) * ATT_BLOCK + ar(2 * ATT_BLOCK)
        k_pos = jnp.concatenate([meta_pos, band_pos])
        k_valid = jnp.concatenate([jnp.ones((N_META,), bool), band_pos >= N_META])
        geo.append(_mask_and_bucket(q_pos, k_pos, k_valid))
    tk_p = N_META + 2 * ATT_BLOCK
    mask_m, buck_m = _mask_and_bucket(meta_pos, meta_pos, jnp.ones((N_META,), bool))
    buf_pos = PAST_LEN - n_buf + ar(n_buf)
    new_pos = PAST_LEN + ar(DT)
    k_pos_s = jnp.concatenate([meta_pos, buf_pos, new_pos])
    k_valid_s = jnp.concatenate([jnp.ones((N_META,), bool), buf_pos >= N_META, jnp.ones((DT,), bool)])
    mask_s, buck_s = _mask_and_bucket(new_pos, k_pos_s, k_valid_s)
    tk_s = N_META + n_buf + DT

    def place(a, rows):
        return jnp.pad(a, ((0, rows - a.shape[0]), (0, tk_p - a.shape[1])))

    idx_all = jnp.concatenate([geo[0][1], geo[1][1], place(buck_m, N_META), place(buck_s, 8)], axis=0)
    bias_all = _bias_table(idx_all, rel_bias.astype(F32))
    bias_p = jnp.transpose(bias_all[:, :2 * ATT_BLOCK].reshape(N_HEADS, 2, ATT_BLOCK, tk_p), (1, 0, 2, 3))
    bias_p = bias_p.reshape(2, N_KV, GROUP * ATT_BLOCK, tk_p)
    mask_p = jnp.stack([jnp.tile(geo[0][0], (GROUP, 1)), jnp.tile(geo[1][0], (GROUP, 1))]).astype(F32)
    o0 = 2 * ATT_BLOCK
    bias_m = _bias_group_rows(bias_all[:, o0:o0 + N_META, :N_META], N_META)
    mask_mg = jnp.tile(mask_m, (GROUP, 1)).astype(F32)
    o1 = o0 + N_META
    bias_s = _bias_group_rows(bias_all[:, o1:o1 + DT, :tk_s], DT)
    mask_sg = jnp.tile(mask_s, (GROUP, 1)).astype(F32)

    def sink_rows(sinks, t):
        return jnp.repeat(sinks.astype(F32).reshape(N_KV, GROUP, 1), t, axis=2).reshape(N_KV, GROUP * t, 1)

    swa_p, swa_s, ml_p, ml_s = [], [], [], []
    for layer in range(depth):
        j = layer // 2
        nw = norm_w[layer].astype(F32).reshape(1, D)
        if layer % 2 == 0:
            w_in = swa_w_in[j].astype(BF16)
            w_out = swa_w_out[j].astype(BF16)
            qn = swa_q_norm[j].astype(F32).reshape(1, HEAD_DIM)
            kn = swa_k_norm[j].astype(F32).reshape(1, HEAD_DIM)
            q_r, k_r, v_r, g_r = _swa_proj(xr, nw, w_in, qn, kn, ROW_TILE)
            q_s, k_s, v_s, g_s = _swa_proj(xs, nw, w_in, qn, kn, SMALL_ROW_TILE)
            mk = k_s[s_meta].reshape(B, N_META, KV_W)
            mv = v_s[s_meta].reshape(B, N_META, KV_W)
            xr = _swa_prompt(q_r, k_r, v_r, g_r, xr, mk, mv, bias_p, mask_p,
                             sink_rows(swa_sinks[j], ATT_BLOCK), w_out, B, n_blk)
            k_new = k_s[:n_samp].reshape(DB, DT, KV_W)
            v_new = v_s[:n_samp].reshape(DB, DT, KV_W)
            ck = cache_swa_k[j].astype(F32).reshape(DB, n_buf, KV_W)
            cv = cache_swa_v[j].astype(F32).reshape(DB, n_buf, KV_W)
            cmk = cache_swa_meta_k[j].astype(F32).reshape(DB, N_META, KV_W)
            cmv = cache_swa_meta_v[j].astype(F32).reshape(DB, N_META, KV_W)
            o_samp = _swa_small(_to_group_rows(q_s[:n_samp], DB, DT), [cmk, ck, k_new], [cmv, cv, v_new],
                                bias_s, mask_sg, sink_rows(swa_sinks[j], DT), SAMPLE_ATT_GROUP)
            o_meta = _swa_small(_to_group_rows(q_s[s_meta], B, N_META), [mk], [mv],
                                bias_m, mask_mg, sink_rows(swa_sinks[j], N_META), B)
            o_small = small_rows(_from_group_rows(o_samp, DB, DT), _from_group_rows(o_meta, B, N_META))
            xs = _resid_matmul(o_small, w_out, xs, SMALL_ROW_TILE, gate=g_s)
            kr3 = k_r.reshape(B, T, N_KV, HEAD_DIM)
            vr3 = v_r.reshape(B, T, N_KV, HEAD_DIM)
            n_keep = min(WINDOW, T + N_META)
            swa_p.append((kr3[:, T - n_keep:], vr3[:, T - n_keep:],
                          mk.reshape(B, N_META, N_KV, HEAD_DIM), mv.reshape(B, N_META, N_KV, HEAD_DIM)))
            swa_s.append((jnp.concatenate([ck, k_new], axis=1)[:, DT:].reshape(DB, n_buf, N_KV, HEAD_DIM),
                          jnp.concatenate([cv, v_new], axis=1)[:, DT:].reshape(DB, n_buf, N_KV, HEAD_DIM)))
        else:
            w_in = mlstm_w_in[j].astype(BF16)
            w_out = mlstm_w_out[j].astype(BF16)
            wts = (mlstm_conv_w[j].astype(F32), mlstm_conv_b[j].astype(F32).reshape(1, M_INNER),
                   mlstm_wq[j].astype(BF16), mlstm_wk[j].astype(BF16), mlstm_wv[j].astype(BF16),
                   mlstm_w_gates[j].astype(BF16), mlstm_b_gates[j].astype(F32).reshape(1, 2 * M_HEADS))
            hn = mlstm_head_norm[j].astype(F32).reshape(1, M_INNER)
            sk = mlstm_skip[j].astype(F32).reshape(1, M_INNER)
            p_r = _norm_proj(xr, nw, w_in, 512, M_INNER).reshape(B, T, 3 * M_INNER)
            p_s = _norm_proj(xs, nw, w_in, SMALL_ROW_TILE, M_INNER)
            xm_samp = p_s[:n_samp, :M_INNER].astype(F32).reshape(DB, DT, M_INNER)
            xm_meta = p_s[s_meta, :M_INNER].astype(F32).reshape(B, N_META, M_INNER)
            xpad_s = jnp.concatenate([state_mlstm_conv[j].astype(F32), xm_samp], axis=1)
            xpad_m = jnp.concatenate([jnp.zeros((B, CONV_W - 1, M_INNER), F32), xm_meta], axis=1)
            shifted = [small_rows(xpad_s[:, w:w + DT].reshape(n_samp, M_INNER),
                                  xpad_m[:, w:w + N_META].reshape(n_meta, M_INNER)) for w in range(CONV_W)]
            q_s, k_s, v_s, xc_s, gt_s = _qkv_small(shifted, wts, SMALL_ROW_TILE)
            cs_real = jnp.pad(xpad_m[:, N_META:], ((0, 0), (8 - (CONV_W - 1), 0), (0, 0)))
            q_r, k_r, v_r, xc_r, gt_r = _qkv_real(p_r, cs_real, wts, ROW_TILE)

            to3 = lambda a, sl, b, t: a[sl].reshape(b, t, a.shape[-1])
            gc_m, gr_m = _gate_layouts(gt_s[s_meta], B, N_META)
            zc = jnp.zeros((B, M_HEADS, M_HD, M_HD), F32)
            zn = jnp.zeros((B, M_HEADS, 1, M_HD), F32)
            zm = jnp.zeros((B, M_HEADS, 1, 128), F32)
            u_m, c_m, n_m, m_m = _mlstm_chunk(
                to3(q_s, s_meta, B, N_META), to3(k_s, s_meta, B, N_META), to3(v_s, s_meta, B, N_META),
                to3(xc_s, s_meta, B, N_META), to3(p_s, s_meta, B, N_META), gc_m, gr_m, zc, zn, zm, hn, sk,
                SMALL_CHUNK, N_META)
            gc_r, gr_r = _gate_layouts(gt_r.reshape(n_real, 2 * M_HEADS), B, T)
            u_r, c_r, n_r, m_r = _mlstm_chunk(q_r, k_r, v_r, xc_r, p_r, gc_r, gr_r, c_m, n_m, m_m, hn, sk,
                                              PROMPT_CHUNK, PROMPT_CHUNK)
            s_samp = slice(0, n_samp)
            gc_s, gr_s = _gate_layouts(gt_s[s_samp], DB, DT)
            gc_s = jnp.pad(gc_s, ((0, 0), (0, 0), (0, SMALL_CHUNK - DT), (0, 0)))
            gr_s = jnp.pad(gr_s, ((0, 0), (0, 0), (0, 0), (0, SMALL_CHUNK - DT)))
            pad3 = lambda a: _pad_t(to3(a, s_samp, DB, DT), SMALL_CHUNK)
            u_s, c_s, n_s, m_s = _mlstm_chunk(
                pad3(q_s), pad3(k_s), pad3(v_s), pad3(xc_s), pad3(p_s), gc_s, gr_s,
                state_mlstm_c[j].astype(F32), state_mlstm_n[j].astype(F32).reshape(DB, M_HEADS, 1, M_HD),
                jnp.broadcast_to(state_mlstm_m[j].astype(F32)[..., None, None], (DB, M_HEADS, 1, 128)), hn, sk,
                SMALL_CHUNK, DT)
            xr = _resid_matmul(u_r.reshape(n_real, M_INNER), w_out, xr, ROW_TILE)
            u_small = small_rows(u_s[:, :DT].reshape(n_samp, M_INNER), u_m.reshape(n_meta, M_INNER))
            xs = _resid_matmul(u_small, w_out, xs, SMALL_ROW_TILE)
            xm_r = p_r[:, T - (CONV_W - 1):, :M_INNER].astype(F32)
            ml_p.append((c_r, n_r.reshape(B, M_HEADS, M_HD), m_r[:, :, 0, 0], xm_r))
            ml_s.append((c_s, n_s.reshape(DB, M_HEADS, M_HD), m_s[:, :, 0, 0], xpad_s[:, DT:]))

    sd = state_mlstm_c.dtype
    stack = lambda rows, idx: jnp.stack([r[idx] for r in rows])
    return (xr.reshape(B, T, D), xs[:n_samp].reshape(DB, DT, D),
            stack(swa_p, 0), stack(swa_p, 1), stack(swa_p, 2), stack(swa_p, 3),
            stack(ml_p, 0).astype(sd), stack(ml_p, 1).astype(sd), stack(ml_p, 2).astype(sd), stack(ml_p, 3),
            stack(swa_s, 0), stack(swa_s, 1),
            stack(ml_s, 0).astype(sd), stack(ml_s, 1).astype(sd), stack(ml_s, 2).astype(sd), stack(ml_s, 3))
```

```python
import functools
import math

import jax
import jax.numpy as jnp
from jax import lax
from jax.experimental import pallas as pl
from jax.experimental.pallas import tpu as pltpu

F32 = jnp.float32
BF16 = jnp.bfloat16

D_MODEL = 1024
N_META = 16
EPS = 1e-6
NEG_INF = -1e30
N_HEADS = 16
HEAD_DIM = 64
N_KV = 4
GROUP = 4
ATT_W = N_HEADS * HEAD_DIM
KV_W = N_KV * HEAD_DIM
WINDOW = 128
ATT_BLOCK = 128
N_BUCKETS = 32
MAX_DISTANCE = 128
M_INNER = 2048
M_HEADS = 4
M_HD = 512
CONV_W = 4
PAST_LEN = 8192

VMEM_LIMIT_BYTES = 56 * 1024 * 1024
PROMPT_CHUNK = 256
SMALL_CHUNK = 16
ROW_TILE = 256
SMALL_ROW_TILE = 128
SAMPLE_ATT_GROUP = 8


def _cparams(*sem):
    return pltpu.CompilerParams(dimension_semantics=sem, vmem_limit_bytes=VMEM_LIMIT_BYTES)


def _rms(x, w):
    ms = jnp.mean(x * x, axis=-1, keepdims=True)
    return x * lax.rsqrt(ms + EPS) * w


def _sigmoid(x):
    return 1.0 / (1.0 + jnp.exp(-x))


def _silu(x):
    return x * _sigmoid(x)


def _dot(a, b):
    return jnp.dot(a, b, preferred_element_type=F32)


def _dot_nt(a, b):
    return lax.dot_general(a, b, (((1,), (1,)), ((), ())), preferred_element_type=F32)


def _dot_tn(a, b):
    return lax.dot_general(a, b, (((0,), (0,)), ((), ())), preferred_element_type=F32)


def _norm_proj_kernel(x_ref, nw_ref, w_ref, o_ref):
    h = _rms(x_ref[...], nw_ref[...]).astype(BF16)
    o_ref[...] = _dot(h, w_ref[...]).astype(o_ref.dtype)


def _norm_proj(x, nw, w, tm, tn):
    rows, d = x.shape
    n = w.shape[1]
    return pl.pallas_call(
        _norm_proj_kernel,
        grid=(n // tn, rows // tm),
        in_specs=[pl.BlockSpec((tm, d), lambda j, i: (i, 0)),
                  pl.BlockSpec((1, d), lambda j, i: (0, 0)),
                  pl.BlockSpec((d, tn), lambda j, i: (0, j))],
        out_specs=pl.BlockSpec((tm, tn), lambda j, i: (i, j)),
        out_shape=jax.ShapeDtypeStruct((rows, n), BF16),
        compiler_params=_cparams("parallel", "parallel"),
        name="norm_proj",
    )(x, nw, w)


def _swa_proj_kernel(x_ref, nw_ref, w_ref, qn_ref, kn_ref, q_ref, k_ref, v_ref, g_ref):
    h = _rms(x_ref[...], nw_ref[...]).astype(BF16)
    proj = _dot(h, w_ref[...])
    qn = qn_ref[...]
    kn = kn_ref[...]
    for hd in range(N_HEADS):
        sl = proj[:, hd * HEAD_DIM:(hd + 1) * HEAD_DIM]
        q_ref[:, hd * HEAD_DIM:(hd + 1) * HEAD_DIM] = _rms(sl, qn).astype(q_ref.dtype)
    for hd in range(N_KV):
        sl = proj[:, ATT_W + hd * HEAD_DIM:ATT_W + (hd + 1) * HEAD_DIM]
        k_ref[:, hd * HEAD_DIM:(hd + 1) * HEAD_DIM] = _rms(sl, kn)
    v_ref[...] = proj[:, ATT_W + KV_W:ATT_W + 2 * KV_W]
    g_ref[...] = _silu(proj[:, ATT_W + 2 * KV_W:])


def _swa_proj(x, nw, w, qn, kn, tm):
    rows, d = x.shape
    n = w.shape[1]
    row = lambda i: (i, 0)
    fixed = lambda i: (0, 0)
    return pl.pallas_call(
        _swa_proj_kernel,
        grid=(rows // tm,),
        in_specs=[pl.BlockSpec((tm, d), row), pl.BlockSpec((1, d), fixed), pl.BlockSpec((d, n), fixed),
                  pl.BlockSpec((1, HEAD_DIM), fixed), pl.BlockSpec((1, HEAD_DIM), fixed)],
        out_specs=[pl.BlockSpec((tm, ATT_W), row), pl.BlockSpec((tm, KV_W), row),
                   pl.BlockSpec((tm, KV_W), row), pl.BlockSpec((tm, ATT_W), row)],
        out_shape=[jax.ShapeDtypeStruct((rows, ATT_W), BF16), jax.ShapeDtypeStruct((rows, KV_W), F32),
                   jax.ShapeDtypeStruct((rows, KV_W), F32), jax.ShapeDtypeStruct((rows, ATT_W), F32)],
        compiler_params=_cparams("parallel"),
        name="swa_proj",
    )(x, nw, w, qn, kn)


def _resid_matmul_kernel(u_ref, w_ref, x_ref, o_ref):
    o_ref[...] = x_ref[...] + _dot(u_ref[...], w_ref[...])


def _gated_resid_matmul_kernel(a_ref, g_ref, w_ref, x_ref, o_ref):
    u = (a_ref[...] * g_ref[...]).astype(BF16)
    o_ref[...] = x_ref[...] + _dot(u, w_ref[...])


def _resid_matmul(u, w, x, tm, gate=None):
    rows, k = u.shape
    d = w.shape[1]
    row = lambda i: (i, 0)
    ins = [u] if gate is None else [u, gate]
    body = _resid_matmul_kernel if gate is None else _gated_resid_matmul_kernel
    return pl.pallas_call(
        body,
        grid=(rows // tm,),
        in_specs=[pl.BlockSpec((tm, k), row)] * len(ins)
                 + [pl.BlockSpec((k, d), lambda i: (0, 0)), pl.BlockSpec((tm, d), row)],
        out_specs=pl.BlockSpec((tm, d), row),
        out_shape=jax.ShapeDtypeStruct((rows, d), F32),
        compiler_params=_cparams("parallel"),
        name="resid_matmul",
    )(*ins, w, x)


def _bias_table_kernel(idx_ref, tab_ref, o_ref):
    idx = idx_ref[...]
    for hd in range(N_HEADS):
        acc = jnp.zeros(idx.shape, F32)
        for b in range(N_BUCKETS):
            acc = jnp.where(idx == b, tab_ref[b, hd], acc)
        o_ref[hd] = acc


def _bias_table(idx, rel_bias):
    r, c = idx.shape
    return pl.pallas_call(
        _bias_table_kernel,
        in_specs=[pl.BlockSpec((r, c), lambda: (0, 0)), pl.BlockSpec(memory_space=pltpu.SMEM)],
        out_specs=pl.BlockSpec((N_HEADS, r, c), lambda: (0, 0, 0)),
        out_shape=jax.ShapeDtypeStruct((N_HEADS, r, c), F32),
        name="bias_table",
    )(idx, rel_bias)


def _rel_bucket(dist):
    n = jnp.maximum(dist, 0)
    max_exact = N_BUCKETS // 2
    nf = jnp.maximum(n, 1).astype(F32)
    large = max_exact + (jnp.log(nf / max_exact) / math.log(MAX_DISTANCE / max_exact)
                         * (N_BUCKETS - max_exact)).astype(jnp.int32)
    large = jnp.minimum(large, N_BUCKETS - 1)
    return jnp.where(n < max_exact, n, large)


def _mask_and_bucket(q_pos, k_pos, k_valid):
    dist = q_pos[:, None] - k_pos[None, :]
    mask = k_valid[None, :] & (dist >= 0) & ((dist <= WINDOW) | (k_pos[None, :] < N_META))
    return mask, _rel_bucket(dist)


def _swa_prompt_kernel(q_ref, kc_ref, kp_ref, vc_ref, vp_ref, mk_ref, mv_ref, g_ref, x_ref,
                       bias_ref, mask_ref, sink_ref, w_ref, o_ref, att_ref):
    kband = jnp.concatenate([mk_ref[...], kp_ref[...], kc_ref[...]], axis=0).astype(BF16)
    vband = jnp.concatenate([mv_ref[...], vp_ref[...], vc_ref[...]], axis=0).astype(BF16)
    q = q_ref[...]
    keep = mask_ref[...] > 0.5
    sink = sink_ref[...]
    scale = HEAD_DIM ** -0.5
    for kv in range(N_KV):
        q4 = jnp.concatenate(
            [q[:, (kv * GROUP + g) * HEAD_DIM:(kv * GROUP + g + 1) * HEAD_DIM] for g in range(GROUP)], axis=0)
        kk = kband[:, kv * HEAD_DIM:(kv + 1) * HEAD_DIM]
        vv = vband[:, kv * HEAD_DIM:(kv + 1) * HEAD_DIM]
        logits = _dot_nt(q4, kk) * scale + bias_ref[kv]
        logits = jnp.where(keep, logits, NEG_INF)
        sk = sink[kv]
        m = jnp.maximum(jnp.max(logits, axis=-1, keepdims=True), sk)
        p = jnp.exp(logits - m)
        den = jnp.sum(p, axis=-1, keepdims=True) + jnp.exp(sk - m)
        o4 = _dot(p.astype(BF16), vv) / den
        for g in range(GROUP):
            hd = kv * GROUP + g
            att_ref[:, hd * HEAD_DIM:(hd + 1) * HEAD_DIM] = o4[g * ATT_BLOCK:(g + 1) * ATT_BLOCK]
    u = (att_ref[...] * g_ref[...]).astype(BF16)
    o_ref[...] = x_ref[...] + _dot(u, w_ref[...])


def _swa_prompt(qn, kn, vn, sg, x, mk, mv, bias, mask, sink, w_out, batch, n_blk):
    cur = lambda b, i: (b * n_blk + i, 0)
    prev = lambda b, i: (b * n_blk + jnp.maximum(i - 1, 0), 0)
    meta = lambda b, i: (b, 0, 0)
    first = lambda b, i: (jnp.minimum(i, 1), 0, 0, 0)
    tk = N_META + 2 * ATT_BLOCK
    rows = qn.shape[0]
    return pl.pallas_call(
        _swa_prompt_kernel,
        grid=(batch, n_blk),
        in_specs=[pl.BlockSpec((ATT_BLOCK, ATT_W), cur),
                  pl.BlockSpec((ATT_BLOCK, KV_W), cur), pl.BlockSpec((ATT_BLOCK, KV_W), prev),
                  pl.BlockSpec((ATT_BLOCK, KV_W), cur), pl.BlockSpec((ATT_BLOCK, KV_W), prev),
                  pl.BlockSpec((None, N_META, KV_W), meta), pl.BlockSpec((None, N_META, KV_W), meta),
                  pl.BlockSpec((ATT_BLOCK, ATT_W), cur), pl.BlockSpec((ATT_BLOCK, D_MODEL), cur),
                  pl.BlockSpec((None, N_KV, GROUP * ATT_BLOCK, tk), first),
                  pl.BlockSpec((None, GROUP * ATT_BLOCK, tk), lambda b, i: (jnp.minimum(i, 1), 0, 0)),
                  pl.BlockSpec((N_KV, GROUP * ATT_BLOCK, 1), lambda b, i: (0, 0, 0)),
                  pl.BlockSpec((ATT_W, D_MODEL), lambda b, i: (0, 0))],
        out_specs=pl.BlockSpec((ATT_BLOCK, D_MODEL), cur),
        out_shape=jax.ShapeDtypeStruct((rows, D_MODEL), F32),
        scratch_shapes=[pltpu.VMEM((ATT_BLOCK, ATT_W), F32)],
        compiler_params=_cparams("parallel", "parallel"),
        name="swa_prompt",
    )(qn, kn, kn, vn, vn, mk, mv, sg, x, bias, mask, sink, w_out)


def _swa_small_kernel(n_seg, *refs):
    q_ref = refs[0]
    k_refs = refs[1:1 + n_seg]
    v_refs = refs[1 + n_seg:1 + 2 * n_seg]
    bias_ref, mask_ref, sink_ref, o_ref = refs[1 + 2 * n_seg:]
    keep = mask_ref[...] > 0.5
    scale = HEAD_DIM ** -0.5
    kall = jnp.concatenate([r[...] for r in k_refs], axis=0).astype(BF16) if n_seg > 1 else k_refs[0][...].astype(BF16)
    vall = jnp.concatenate([r[...] for r in v_refs], axis=0).astype(BF16) if n_seg > 1 else v_refs[0][...].astype(BF16)
    for kv in range(N_KV):
        kk = kall[:, kv * HEAD_DIM:(kv + 1) * HEAD_DIM]
        vv = vall[:, kv * HEAD_DIM:(kv + 1) * HEAD_DIM]
        logits = _dot_nt(q_ref[kv], kk) * scale + bias_ref[kv]
        logits = jnp.where(keep, logits, NEG_INF)
        sk = sink_ref[kv]
        m = jnp.maximum(jnp.max(logits, axis=-1, keepdims=True), sk)
        p = jnp.exp(logits - m)
        den = jnp.sum(p, axis=-1, keepdims=True) + jnp.exp(sk - m)
        o_ref[kv] = _dot(p.astype(BF16), vv) / den


def _swa_small(q, k_segs, v_segs, seg_len, bias, mask, sink, group, rq):
    rows = q.shape[1]
    n_seg = len(k_segs)
    tk = group * sum(seg_len)
    seg_specs = [pl.BlockSpec((group * n, KV_W), lambda i: (i, 0)) for n in seg_len]
    return pl.pallas_call(
        functools.partial(_swa_small_kernel, n_seg),
        grid=(rows // (group * rq),),
        in_specs=[pl.BlockSpec((N_KV, group * rq, HEAD_DIM), lambda i: (0, i, 0))] + seg_specs + seg_specs
                 + [pl.BlockSpec((N_KV, group * rq, tk), lambda i: (0, 0, 0)),
                    pl.BlockSpec((group * rq, tk), lambda i: (0, 0)),
                    pl.BlockSpec((N_KV, group * rq, 1), lambda i: (0, 0, 0))],
        out_specs=pl.BlockSpec((N_KV, group * rq, HEAD_DIM), lambda i: (0, i, 0)),
        out_shape=jax.ShapeDtypeStruct((N_KV, rows, HEAD_DIM), F32),
        compiler_params=_cparams("parallel"),
        name="swa_small",
    )(q, *k_segs, *v_segs, bias, mask, sink)


def _to_group_rows(a, nb, t):
    a = a.reshape(nb, t, N_KV, GROUP, HEAD_DIM)
    return jnp.transpose(a, (2, 0, 3, 1, 4)).reshape(N_KV, nb * GROUP * t, HEAD_DIM)


def _from_group_rows(a, nb, t):
    a = a.reshape(N_KV, nb, GROUP, t, HEAD_DIM)
    return jnp.transpose(a, (1, 3, 0, 2, 4)).reshape(nb * t, ATT_W)


def _bias_group_rows(b, t):
    return b.reshape(N_KV, GROUP * t, b.shape[-1])


def _block_diag_keys(a, group, seg_len):
    eye = jnp.eye(group, dtype=a.dtype)
    out, o = [], 0
    for n in seg_len:
        blk = a[..., o:o + n]
        o += n
        big = eye[:, None, :, None] * blk[..., None, :, None, :]
        out.append(big.reshape(a.shape[:-2] + (group * a.shape[-2], group * n)))
    return jnp.concatenate(out, axis=-1)


def _qkv_body(xs, cw_ref, cb_ref, wq_ref, wk_ref, wv_ref, wg_ref, bg_ref, q_ref, k_ref, v_ref, xc_ref, gt_ref):
    cw = cw_ref[...]
    acc = xs[0] * cw[0:1]
    for w in range(1, CONV_W):
        acc = acc + xs[w] * cw[w:w + 1]
    xc = _silu(acc + cb_ref[...])
    xcb = xc.astype(BF16)
    xmb = xs[CONV_W - 1].astype(BF16)
    xc_ref[...] = xcb
    gates = bg_ref[...]
    for h in range(M_HEADS):
        sl = slice(h * M_HD, (h + 1) * M_HD)
        qh = _dot(xcb[:, sl], wq_ref[h]).astype(BF16)
        kh = (_dot(xcb[:, sl], wk_ref[h]) * (M_HD ** -0.5)).astype(BF16)
        vh = _dot(xmb[:, sl], wv_ref[h]).astype(BF16)
        q_ref[:, sl] = qh
        k_ref[:, sl] = kh
        v_ref[:, sl] = vh
        gates = (gates + _dot(qh, wg_ref[h * M_HD:(h + 1) * M_HD])
                 + _dot(kh, wg_ref[M_INNER + h * M_HD:M_INNER + (h + 1) * M_HD])
                 + _dot(vh, wg_ref[2 * M_INNER + h * M_HD:2 * M_INNER + (h + 1) * M_HD]))
    lane = lax.broadcasted_iota(jnp.int32, gates.shape, 1)
    log_f = jnp.minimum(gates, 0.0) - jnp.log1p(jnp.exp(-jnp.abs(gates)))
    gt_ref[...] = jnp.where(lane < M_HEADS, gates, log_f)


def _qkv_halo_kernel(xm_ref, prev_ref, cs_ref, *rest):
    i = pl.program_id(1)
    cur = xm_ref[...].astype(F32)
    halo = jnp.where(i == 0, cs_ref[...], prev_ref[...].astype(F32)[8:])
    full = jnp.concatenate([halo, cur], axis=0)
    tm = cur.shape[0]
    off = 8 - (CONV_W - 1)
    xs = [full[off + w:off + w + tm] for w in range(CONV_W - 1)] + [cur]
    _qkv_body(xs, *rest)


def _qkv_shift_kernel(x0_ref, x1_ref, x2_ref, x3_ref, *rest):
    _qkv_body([x0_ref[...], x1_ref[...], x2_ref[...], x3_ref[...]], *rest)


def _qkv_weight_specs(fixed2, fixed3):
    return [pl.BlockSpec((CONV_W, M_INNER), fixed2), pl.BlockSpec((1, M_INNER), fixed2),
            pl.BlockSpec((M_HEADS, M_HD, M_HD), fixed3), pl.BlockSpec((M_HEADS, M_HD, M_HD), fixed3),
            pl.BlockSpec((M_HEADS, M_HD, M_HD), fixed3),
            pl.BlockSpec((3 * M_INNER, 2 * M_HEADS), fixed2), pl.BlockSpec((1, 2 * M_HEADS), fixed2)]


def _qkv_real(proj3, cs, wts, tm):
    b, t, _ = proj3.shape
    cur = lambda bi, i: (bi, i, 0)
    outs = [jax.ShapeDtypeStruct((b, t, M_INNER), BF16)] * 4 + [jax.ShapeDtypeStruct((b, t, 2 * M_HEADS), F32)]
    return pl.pallas_call(
        _qkv_halo_kernel,
        grid=(b, t // tm),
        in_specs=[pl.BlockSpec((None, tm, M_INNER), cur),
                  pl.BlockSpec((None, 16, M_INNER), lambda bi, i: (bi, jnp.maximum(i * (tm // 16) - 1, 0), 0)),
                  pl.BlockSpec((None, 8, M_INNER), lambda bi, i: (bi, 0, 0))]
                 + _qkv_weight_specs(lambda bi, i: (0, 0), lambda bi, i: (0, 0, 0)),
        out_specs=[pl.BlockSpec((None, tm, M_INNER), cur)] * 4 + [pl.BlockSpec((None, tm, 2 * M_HEADS), cur)],
        out_shape=outs,
        compiler_params=_cparams("parallel", "parallel"),
        name="mlstm_qkv_real",
    )(proj3, proj3, cs, *wts)


def _qkv_small(xs, wts, tm):
    rows = xs[0].shape[0]
    row = lambda i: (i, 0)
    outs = [jax.ShapeDtypeStruct((rows, M_INNER), BF16)] * 4 + [jax.ShapeDtypeStruct((rows, 2 * M_HEADS), F32)]
    return pl.pallas_call(
        _qkv_shift_kernel,
        grid=(rows // tm,),
        in_specs=[pl.BlockSpec((tm, M_INNER), row)] * CONV_W
                 + _qkv_weight_specs(lambda i: (0, 0), lambda i: (0, 0, 0)),
        out_specs=[pl.BlockSpec((tm, M_INNER), row)] * 4 + [pl.BlockSpec((tm, 2 * M_HEADS), row)],
        out_shape=outs,
        compiler_params=_cparams("parallel"),
        name="mlstm_qkv_small",
    )(*xs, *wts)


def _chunk_kernel(seg, aliased, q_ref, k_ref, v_ref, xc_ref, z_ref, op_ref, gc_ref, gr_ref, c0_ref, n0_ref, m0_ref,
                  hn_ref, sk_ref, *rest):
    u_ref, c_ref, n_ref, m_ref = rest[1:] if aliased else rest
    h_idx = pl.program_id(1)
    c_idx = pl.program_id(2)
    L = q_ref.shape[0]
    nseg = L // seg
    hi = lax.Precision.HIGHEST

    @pl.when(c_idx == 0)
    def _():
        c_ref[...] = c0_ref[...]
        n_ref[...] = n0_ref[...]
        m_ref[...] = m0_ref[...]

    gc = gc_ref[...]
    gr = gr_ref[...]
    col_c = lax.broadcasted_iota(jnp.int32, gc.shape, 1)
    row_r = lax.broadcasted_iota(jnp.int32, gr.shape, 0)
    i_c = jnp.sum(jnp.where(col_c == h_idx, gc, 0.0), axis=1, keepdims=True)
    i_r = jnp.sum(jnp.where(row_r == h_idx, gr, 0.0), axis=0, keepdims=True)
    lf_c = jnp.where(col_c == M_HEADS + h_idx, gc, 0.0)
    lf_r = jnp.where(row_r == M_HEADS + h_idx, gr, 0.0)
    t_i = lax.broadcasted_iota(jnp.int32, (L, L), 0)
    j_i = lax.broadcasted_iota(jnp.int32, (L, L), 1)
    causal = j_i <= t_i
    upper = t_i <= j_i
    if nseg > 1:
        sid_c = lax.broadcasted_iota(jnp.int32, (L, 1), 0) // seg
        same = (t_i // seg) == (j_i // seg)
        causal = causal & same
        upper = upper & same
    tril = jnp.where(causal, 1.0, 0.0).astype(F32)
    triu = jnp.where(upper, 1.0, 0.0).astype(F32)
    b_c = jnp.sum(jnp.dot(tril, lf_c, precision=hi, preferred_element_type=F32), axis=1, keepdims=True)
    b_r = jnp.sum(jnp.dot(lf_r, triu, precision=hi, preferred_element_type=F32), axis=0, keepdims=True)
    if nseg > 1:
        g = jnp.sum(jnp.dot(jnp.where(same, 1.0, 0.0).astype(F32), lf_c, precision=hi, preferred_element_type=F32),
                    axis=1, keepdims=True)
        pick = lambda vals: sum(jnp.where(sid_c == s, vals[s], 0.0) for s in range(nseg))
        m_prev = pick([m_ref[s][:, 0:1] for s in range(nseg)])
    else:
        g = b_c[L - 1:L, :]
        m_prev = m_ref[0][:, 0:1]

    log_d = jnp.where(causal, b_c + (i_r - b_r), -jnp.inf)
    inter = b_c + m_prev
    m_t = jnp.maximum(inter, jnp.max(log_d, axis=-1, keepdims=True))
    w_intra = jnp.exp(log_d - m_t)
    w_inter = jnp.exp(inter - m_t)
    q = q_ref[...]
    k = k_ref[...]
    v = v_ref[...]
    s_mat = _dot_nt(q, k) * w_intra
    qf = q.astype(F32)
    if nseg > 1:
        q_c = pick([_dot(q, c_ref[s].astype(BF16)) for s in range(nseg)])
        q_n = pick([jnp.sum(qf * n_ref[s], axis=-1, keepdims=True) for s in range(nseg)])
    else:
        q_c = _dot(q, c_ref[0].astype(BF16))
        q_n = jnp.sum(qf * n_ref[0], axis=-1, keepdims=True)
    num = _dot(s_mat.astype(BF16), v) + w_inter * q_c
    den = jnp.sum(s_mat, axis=-1, keepdims=True) + w_inter * q_n
    h = num * (1.0 / jnp.maximum(jnp.abs(den), jnp.exp(-m_t)))

    lw = (g - b_c) + i_c
    kf = k.astype(F32)
    if nseg > 1:
        in_seg = [sid_c == s for s in range(nseg)]
        g_s = [jnp.max(jnp.where(in_seg[s], g, -jnp.inf), axis=0, keepdims=True) for s in range(nseg)]
        m_old = [m_ref[s][:, 0:1] for s in range(nseg)]
        m_new = [jnp.maximum(g_s[s] + m_old[s], jnp.max(jnp.where(in_seg[s], lw, -jnp.inf), axis=0, keepdims=True))
                 for s in range(nseg)]
        kw = kf * jnp.exp(lw - pick(m_new))
        for s in range(nseg):
            decay = jnp.exp(g_s[s] + m_old[s] - m_new[s])
            kw_s = jnp.where(in_seg[s], kw, 0.0)
            c_ref[s] = decay * c_ref[s] + _dot_tn(kw_s.astype(BF16), v)
            n_ref[s] = decay * n_ref[s] + jnp.sum(kw_s, axis=0, keepdims=True)
            m_ref[s] = jnp.broadcast_to(m_new[s], m_ref.shape[1:])
    else:
        m_new = jnp.maximum(g + m_prev, jnp.max(lw, axis=0, keepdims=True))
        decay = jnp.exp(g + m_prev - m_new)
        kw = kf * jnp.exp(lw - m_new)
        c_ref[0] = decay * c_ref[0] + _dot_tn(kw.astype(BF16), v)
        n_ref[0] = decay * n_ref[0] + jnp.sum(kw, axis=0, keepdims=True)
        m_ref[0] = jnp.broadcast_to(m_new, m_ref.shape[1:])

    mu = jnp.mean(h, axis=-1, keepdims=True)
    hc = h - mu
    var = jnp.mean(hc * hc, axis=-1, keepdims=True)
    h_out = _sigmoid(op_ref[...].astype(F32)) * (hc * lax.rsqrt(var + EPS) * hn_ref[...])
    u = (h_out + sk_ref[...] * xc_ref[...].astype(F32)) * _silu(z_ref[...].astype(F32))
    u_ref[...] = u.astype(u_ref.dtype)


def _mlstm_chunk(q, k, v, xc, proj3, gates, c0, n0, m0, lyr_in, head_norm, skip, chunk, seg,
                 c_acc=None, lyr_out=0, n_lyr_out=1):
    nb, t, _ = q.shape
    nc = t // chunk
    nseg = chunk // seg
    assert nseg == 1 or nc == 1
    n_seq = nb * nseg
    gates_t = jnp.swapaxes(gates, 1, 2)
    tok = lambda bi, h, c: (bi, c, h)
    hd_blk = pl.BlockSpec((None, chunk, M_HD), tok)
    st_in = lambda bi, h, c: (lyr_in, bi, h, 0, 0)
    st_out = lambda bi, h, c: (0, bi, h, 0, 0)
    c_out = lambda bi, h, c: (lyr_out, bi, h, 0, 0)
    head_row = pl.BlockSpec((1, M_HD), lambda bi, h, c: (0, h))
    in_specs = [hd_blk, hd_blk, hd_blk, hd_blk,
                pl.BlockSpec((None, chunk, M_HD), lambda bi, h, c: (bi, c, M_HEADS + h)),
                pl.BlockSpec((None, chunk, M_HD), lambda bi, h, c: (bi, c, 2 * M_HEADS + h)),
                pl.BlockSpec((None, chunk, 2 * M_HEADS), lambda bi, h, c: (bi, c, 0)),
                pl.BlockSpec((None, 2 * M_HEADS, chunk), lambda bi, h, c: (bi, 0, c)),
                pl.BlockSpec((None, nseg, None, M_HD, M_HD), st_in),
                pl.BlockSpec((None, nseg, None, 1, M_HD), st_in),
                pl.BlockSpec((None, nseg, None, 1, 128), st_in),
                head_row, head_row]
    args = [q, k, v, xc, proj3, proj3, gates, gates_t, c0, n0, m0, head_norm, skip]
    aliases = {}
    if c_acc is not None:
        in_specs.append(pl.BlockSpec(memory_space=pl.ANY))
        aliases = {len(args): 1}
        args.append(c_acc)
    return pl.pallas_call(
        functools.partial(_chunk_kernel, seg, c_acc is not None),
        grid=(nb, M_HEADS, nc),
        in_specs=in_specs,
        out_specs=[hd_blk,
                   pl.BlockSpec((None, nseg, None, M_HD, M_HD), c_out),
                   pl.BlockSpec((None, nseg, None, 1, M_HD), st_out),
                   pl.BlockSpec((None, nseg, None, 1, 128), st_out)],
        out_shape=[jax.ShapeDtypeStruct((nb, t, M_INNER), BF16),
                   jax.ShapeDtypeStruct((n_lyr_out, n_seq, M_HEADS, M_HD, M_HD), F32),
                   jax.ShapeDtypeStruct((1, n_seq, M_HEADS, 1, M_HD), F32),
                   jax.ShapeDtypeStruct((1, n_seq, M_HEADS, 1, 128), F32)],
        input_output_aliases=aliases,
        compiler_params=_cparams("parallel", "parallel", "arbitrary"),
        name="mlstm_chunk",
    )(*args)


def kernel(x_prompt, x_sample, cache_swa_k, cache_swa_v, cache_swa_meta_k, cache_swa_meta_v, state_mlstm_c, state_mlstm_n, state_mlstm_m, state_mlstm_conv, meta_tokens, rel_bias, norm_w, swa_w_in, swa_q_norm, swa_k_norm, swa_sinks, swa_w_out, mlstm_w_in, mlstm_conv_w, mlstm_conv_b, mlstm_wq, mlstm_wk, mlstm_wv, mlstm_w_gates, mlstm_b_gates, mlstm_head_norm, mlstm_skip, mlstm_w_out):
    B, T, D = x_prompt.shape
    DB, DT, _ = x_sample.shape
    depth = norm_w.shape[0]
    n_buf = cache_swa_k.shape[2]
    n_blk = T // ATT_BLOCK
    n_real, n_samp, n_meta = B * T, DB * DT, B * N_META
    n_small = -(-(n_samp + n_meta) // SMALL_ROW_TILE) * SMALL_ROW_TILE
    pad_small = n_small - n_samp - n_meta
    s_meta = slice(n_samp, n_samp + n_meta)

    def small_rows(samp, meta):
        return jnp.concatenate([samp, meta, jnp.zeros((pad_small,) + samp.shape[1:], samp.dtype)], axis=0)

    xr = x_prompt.reshape(n_real, D)
    xs = small_rows(x_sample.reshape(n_samp, D),
                    jnp.broadcast_to(meta_tokens.astype(x_prompt.dtype)[None], (B, N_META, D)).reshape(n_meta, D))

    ar = lambda n: jnp.arange(n, dtype=jnp.int32)
    meta_pos = ar(N_META)
    geo = []
    for blk in (0, 1):
        q_pos = N_META + blk * ATT_BLOCK + ar(ATT_BLOCK)
        band_pos = N_META + (blk - 1) * ATT_BLOCK + ar(2 * ATT_BLOCK)
        k_pos = jnp.concatenate([meta_pos, band_pos])
        k_valid = jnp.concatenate([jnp.ones((N_META,), bool), band_pos >= N_META])
        geo.append(_mask_and_bucket(q_pos, k_pos, k_valid))
    tk_p = N_META + 2 * ATT_BLOCK
    mask_m, buck_m = _mask_and_bucket(meta_pos, meta_pos, jnp.ones((N_META,), bool))
    buf_pos = PAST_LEN - n_buf + ar(n_buf)
    new_pos = PAST_LEN + ar(DT)
    k_pos_s = jnp.concatenate([meta_pos, buf_pos, new_pos])
    k_valid_s = jnp.concatenate([jnp.ones((N_META,), bool), buf_pos >= N_META, jnp.ones((DT,), bool)])
    mask_s, buck_s = _mask_and_bucket(new_pos, k_pos_s, k_valid_s)
    tk_s = N_META + n_buf + DT

    def place(a, rows):
        return jnp.pad(a, ((0, rows - a.shape[0]), (0, tk_p - a.shape[1])))

    idx_all = jnp.concatenate([geo[0][1], geo[1][1], place(buck_m, N_META), place(buck_s, 8)], axis=0)
    bias_all = _bias_table(idx_all, rel_bias.astype(F32))
    bias_p = jnp.transpose(bias_all[:, :2 * ATT_BLOCK].reshape(N_HEADS, 2, ATT_BLOCK, tk_p), (1, 0, 2, 3))
    bias_p = bias_p.reshape(2, N_KV, GROUP * ATT_BLOCK, tk_p)
    mask_p = jnp.stack([jnp.tile(geo[0][0], (GROUP, 1)), jnp.tile(geo[1][0], (GROUP, 1))]).astype(F32)
    o0 = 2 * ATT_BLOCK
    seg_m, seg_s = (N_META,), (N_META, n_buf, DT)
    grp_s = SAMPLE_ATT_GROUP
    bias_m = _block_diag_keys(_bias_group_rows(bias_all[:, o0:o0 + N_META, :N_META], N_META), B, seg_m)
    mask_mg = _block_diag_keys(jnp.tile(mask_m, (GROUP, 1)).astype(F32), B, seg_m)
    o1 = o0 + N_META
    bias_s = _block_diag_keys(_bias_group_rows(bias_all[:, o1:o1 + DT, :tk_s], DT), grp_s, seg_s)
    mask_sg = _block_diag_keys(jnp.tile(mask_s, (GROUP, 1)).astype(F32), grp_s, seg_s)

    def sink_rows(sinks, t, n_seq=1):
        rows = jnp.repeat(sinks.astype(F32).reshape(N_KV, GROUP, 1), t, axis=2).reshape(N_KV, GROUP * t, 1)
        return jnp.tile(rows, (1, n_seq, 1))

    swa_p, swa_s, ml_p, ml_s = [], [], [], []
    c_samp = None
    assert n_samp % SMALL_CHUNK == 0 and SMALL_CHUNK % DT == 0
    for layer in range(depth):
        j = layer // 2
        nw = norm_w[layer].astype(F32).reshape(1, D)
        if layer % 2 == 0:
            w_in = swa_w_in[j].astype(BF16)
            w_out = swa_w_out[j].astype(BF16)
            qn = swa_q_norm[j].astype(F32).reshape(1, HEAD_DIM)
            kn = swa_k_norm[j].astype(F32).reshape(1, HEAD_DIM)
            q_r, k_r, v_r, g_r = _swa_proj(xr, nw, w_in, qn, kn, ROW_TILE)
            q_s, k_s, v_s, g_s = _swa_proj(xs, nw, w_in, qn, kn, SMALL_ROW_TILE)
            mk = k_s[s_meta].reshape(B, N_META, KV_W)
            mv = v_s[s_meta].reshape(B, N_META, KV_W)
            xr = _swa_prompt(q_r, k_r, v_r, g_r, xr, mk, mv, bias_p, mask_p,
                             sink_rows(swa_sinks[j], ATT_BLOCK), w_out, B, n_blk)
            k_new = k_s[:n_samp]
            v_new = v_s[:n_samp]
            ck = cache_swa_k[j].astype(F32).reshape(DB * n_buf, KV_W)
            cv = cache_swa_v[j].astype(F32).reshape(DB * n_buf, KV_W)
            cmk = cache_swa_meta_k[j].astype(F32).reshape(DB * N_META, KV_W)
            cmv = cache_swa_meta_v[j].astype(F32).reshape(DB * N_META, KV_W)
            o_samp = _swa_small(_to_group_rows(q_s[:n_samp], DB, DT), [cmk, ck, k_new], [cmv, cv, v_new], seg_s,
                                bias_s, mask_sg, sink_rows(swa_sinks[j], DT, grp_s), grp_s, GROUP * DT)
            o_meta = _swa_small(_to_group_rows(q_s[s_meta], B, N_META), [k_s[s_meta]], [v_s[s_meta]], seg_m,
                                bias_m, mask_mg, sink_rows(swa_sinks[j], N_META, B), B, GROUP * N_META)
            o_small = small_rows(_from_group_rows(o_samp, DB, DT), _from_group_rows(o_meta, B, N_META))
            xs = _resid_matmul(o_small, w_out, xs, SMALL_ROW_TILE, gate=g_s)
            n_keep = min(WINDOW, T + N_META)
            last = lambda a: a.reshape(B, T, KV_W)[:, T - n_keep:].reshape(B, n_keep, N_KV, HEAD_DIM)
            swa_p.append((last(k_r), last(v_r),
                          mk.reshape(B, N_META, N_KV, HEAD_DIM), mv.reshape(B, N_META, N_KV, HEAD_DIM)))
            win = lambda c, new: jnp.concatenate(
                [c.reshape(DB, n_buf, KV_W), new.reshape(DB, DT, KV_W)], axis=1)[:, DT:].reshape(DB, n_buf, N_KV, HEAD_DIM)
            swa_s.append((win(ck, k_new), win(cv, v_new)))
        else:
            w_in = mlstm_w_in[j].astype(BF16)
            w_out = mlstm_w_out[j].astype(BF16)
            wts = (mlstm_conv_w[j].astype(F32), mlstm_conv_b[j].astype(F32).reshape(1, M_INNER),
                   mlstm_wq[j].astype(BF16), mlstm_wk[j].astype(BF16), mlstm_wv[j].astype(BF16),
                   mlstm_w_gates[j].astype(BF16), mlstm_b_gates[j].astype(F32).reshape(1, 2 * M_HEADS))
            hn = mlstm_head_norm[j].astype(F32).reshape(1, M_INNER)
            sk = mlstm_skip[j].astype(F32).reshape(1, M_INNER)
            p_r = _norm_proj(xr, nw, w_in, 512, M_INNER).reshape(B, T, 3 * M_INNER)
            p_s = _norm_proj(xs, nw, w_in, SMALL_ROW_TILE, M_INNER)
            xm_samp = p_s[:n_samp, :M_INNER].astype(F32).reshape(DB, DT, M_INNER)
            xm_meta = p_s[s_meta, :M_INNER].astype(F32).reshape(B, N_META, M_INNER)
            xpad_s = jnp.concatenate([state_mlstm_conv[j].astype(F32), xm_samp], axis=1)
            xpad_m = jnp.concatenate([jnp.zeros((B, CONV_W - 1, M_INNER), F32), xm_meta], axis=1)
            shifted = [small_rows(xpad_s[:, w:w + DT].reshape(n_samp, M_INNER),
                                  xpad_m[:, w:w + N_META].reshape(n_meta, M_INNER)) for w in range(CONV_W)]
            q_s, k_s, v_s, xc_s, gt_s = _qkv_small(shifted, wts, SMALL_ROW_TILE)
            cs_real = jnp.pad(xpad_m[:, N_META:], ((0, 0), (8 - (CONV_W - 1), 0), (0, 0)))
            q_r, k_r, v_r, xc_r, gt_r = _qkv_real(p_r, cs_real, wts, ROW_TILE)

            to3 = lambda a, sl, b, t: a[sl].reshape(b, t, a.shape[-1])
            zc = jnp.zeros((1, B, M_HEADS, M_HD, M_HD), F32)
            zn = jnp.zeros((1, B, M_HEADS, 1, M_HD), F32)
            zm = jnp.zeros((1, B, M_HEADS, 1, 128), F32)
            u_m, c_m, n_m, m_m = _mlstm_chunk(
                to3(q_s, s_meta, B, N_META), to3(k_s, s_meta, B, N_META), to3(v_s, s_meta, B, N_META),
                to3(xc_s, s_meta, B, N_META), to3(p_s, s_meta, B, N_META), to3(gt_s, s_meta, B, N_META),
                zc, zn, zm, 0, hn, sk, N_META, N_META)
            u_r, c_r, n_r, m_r = _mlstm_chunk(q_r, k_r, v_r, xc_r, p_r, gt_r, c_m, n_m, m_m, 0, hn, sk,
                                              PROMPT_CHUNK, PROMPT_CHUNK)
            s_samp = slice(0, n_samp)
            nb_s = n_samp // SMALL_CHUNK
            u_s, c_samp, n_s, m_s = _mlstm_chunk(
                to3(q_s, s_samp, nb_s, SMALL_CHUNK), to3(k_s, s_samp, nb_s, SMALL_CHUNK),
                to3(v_s, s_samp, nb_s, SMALL_CHUNK), to3(xc_s, s_samp, nb_s, SMALL_CHUNK),
                to3(p_s, s_samp, nb_s, SMALL_CHUNK), to3(gt_s, s_samp, nb_s, SMALL_CHUNK),
                state_mlstm_c.astype(F32), state_mlstm_n.astype(F32)[:, :, :, None, :],
                jnp.broadcast_to(state_mlstm_m.astype(F32)[..., None, None], state_mlstm_m.shape + (1, 128)),
                j, hn, sk, SMALL_CHUNK, DT, c_acc=c_samp, lyr_out=j, n_lyr_out=state_mlstm_c.shape[0])
            xr = _resid_matmul(u_r.reshape(n_real, M_INNER), w_out, xr, ROW_TILE)
            u_small = small_rows(u_s.reshape(n_samp, M_INNER), u_m.reshape(n_meta, M_INNER))
            xs = _resid_matmul(u_small, w_out, xs, SMALL_ROW_TILE)
            xm_r = p_r[:, T - (CONV_W - 1):, :M_INNER].astype(F32)
            ml_p.append((c_r[0], n_r.reshape(B, M_HEADS, M_HD), m_r[0, :, :, 0, 0], xm_r))
            ml_s.append((n_s.reshape(DB, M_HEADS, M_HD), m_s[0, :, :, 0, 0], xpad_s[:, DT:]))

    sd = state_mlstm_c.dtype
    stack = lambda rows, idx: jnp.stack([r[idx] for r in rows])
    return (xr.reshape(B, T, D), xs[:n_samp].reshape(DB, DT, D),
            stack(swa_p, 0), stack(swa_p, 1), stack(swa_p, 2), stack(swa_p, 3),
            stack(ml_p, 0).astype(sd), stack(ml_p, 1).astype(sd), stack(ml_p, 2).astype(sd), stack(ml_p, 3),
            stack(swa_s, 0), stack(swa_s, 1),
            c_samp.astype(sd), stack(ml_s, 0).astype(sd), stack(ml_s, 1).astype(sd), stack(ml_s, 2))
```

```python
import functools
import math

import jax
import jax.numpy as jnp
from jax import lax
from jax.experimental import pallas as pl
from jax.experimental.pallas import tpu as pltpu

F32 = jnp.float32
BF16 = jnp.bfloat16

D_MODEL = 1024
N_META = 16
EPS = 1e-6
NEG_INF = -1e30
N_HEADS = 16
HEAD_DIM = 64
N_KV = 4
GROUP = 4
ATT_W = N_HEADS * HEAD_DIM
KV_W = N_KV * HEAD_DIM
WINDOW = 128
ATT_BLOCK = 128
N_BUCKETS = 32
MAX_DISTANCE = 128
M_INNER = 2048
M_HEADS = 4
M_HD = 512
CONV_W = 4
PAST_LEN = 8192

VMEM_LIMIT_BYTES = 56 * 1024 * 1024
PROMPT_CHUNK = 256
SMALL_CHUNK = 16
ROW_TILE = 256
PROJ_T_ROW_TILE = 512
SMALL_ROW_TILE = 128
SAMPLE_ATT_GROUP = 8


def _cparams(*sem):
    return pltpu.CompilerParams(dimension_semantics=sem, vmem_limit_bytes=VMEM_LIMIT_BYTES)


def _rms(x, w):
    ms = jnp.mean(x * x, axis=-1, keepdims=True)
    return x * lax.rsqrt(ms + EPS) * w


def _sigmoid(x):
    return 1.0 / (1.0 + jnp.exp(-x))


def _silu(x):
    return x * _sigmoid(x)


def _dot(a, b):
    return jnp.dot(a, b, preferred_element_type=F32)


def _dot_nt(a, b):
    return lax.dot_general(a, b, (((1,), (1,)), ((), ())), preferred_element_type=F32)


def _dot_tn(a, b):
    return lax.dot_general(a, b, (((0,), (0,)), ((), ())), preferred_element_type=F32)


def _norm_proj_kernel(x_ref, nw_ref, w_ref, o_ref):
    h = _rms(x_ref[...], nw_ref[...]).astype(BF16)
    o_ref[...] = _dot(h, w_ref[...]).astype(o_ref.dtype)


def _norm_proj(x, nw, w, tm, tn):
    rows, d = x.shape
    n = w.shape[1]
    return pl.pallas_call(
        _norm_proj_kernel,
        grid=(n // tn, rows // tm),
        in_specs=[pl.BlockSpec((tm, d), lambda j, i: (i, 0)),
                  pl.BlockSpec((1, d), lambda j, i: (0, 0)),
                  pl.BlockSpec((d, tn), lambda j, i: (0, j))],
        out_specs=pl.BlockSpec((tm, tn), lambda j, i: (i, j)),
        out_shape=jax.ShapeDtypeStruct((rows, n), BF16),
        compiler_params=_cparams("parallel", "parallel"),
        name="norm_proj",
    )(x, nw, w)


def _swa_proj_kernel(x_ref, nw_ref, w_ref, qn_ref, kn_ref, q_ref, k_ref, v_ref, g_ref):
    h = _rms(x_ref[...], nw_ref[...]).astype(BF16)
    proj = _dot(h, w_ref[...])
    qn = qn_ref[...]
    kn = kn_ref[...]
    for hd in range(N_HEADS):
        sl = proj[:, hd * HEAD_DIM:(hd + 1) * HEAD_DIM]
        q_ref[:, hd * HEAD_DIM:(hd + 1) * HEAD_DIM] = _rms(sl, qn).astype(q_ref.dtype)
    for hd in range(N_KV):
        sl = proj[:, ATT_W + hd * HEAD_DIM:ATT_W + (hd + 1) * HEAD_DIM]
        k_ref[:, hd * HEAD_DIM:(hd + 1) * HEAD_DIM] = _rms(sl, kn)
    v_ref[...] = proj[:, ATT_W + KV_W:ATT_W + 2 * KV_W]
    g_ref[...] = _silu(proj[:, ATT_W + 2 * KV_W:])


def _swa_proj(x, nw, w, qn, kn, tm):
    rows, d = x.shape
    n = w.shape[1]
    row = lambda i: (i, 0)
    fixed = lambda i: (0, 0)
    return pl.pallas_call(
        _swa_proj_kernel,
        grid=(rows // tm,),
        in_specs=[pl.BlockSpec((tm, d), row), pl.BlockSpec((1, d), fixed), pl.BlockSpec((d, n), fixed),
                  pl.BlockSpec((1, HEAD_DIM), fixed), pl.BlockSpec((1, HEAD_DIM), fixed)],
        out_specs=[pl.BlockSpec((tm, ATT_W), row), pl.BlockSpec((tm, KV_W), row),
                   pl.BlockSpec((tm, KV_W), row), pl.BlockSpec((tm, ATT_W), row)],
        out_shape=[jax.ShapeDtypeStruct((rows, ATT_W), BF16), jax.ShapeDtypeStruct((rows, KV_W), F32),
                   jax.ShapeDtypeStruct((rows, KV_W), F32), jax.ShapeDtypeStruct((rows, ATT_W), F32)],
        compiler_params=_cparams("parallel"),
        name="swa_proj",
    )(x, nw, w, qn, kn)


def _swa_proj_t_kernel(x_ref, nw_ref, wt_ref, wk_ref, qn_ref, kn_ref, qt_ref, k_ref, vt_ref, gt_ref):
    h = _rms(x_ref[...], nw_ref[...]).astype(BF16)
    pt = _dot_nt(wt_ref[...], h)
    qn = qn_ref[...]
    for hd in range(N_HEADS):
        sl = pt[hd * HEAD_DIM:(hd + 1) * HEAD_DIM]
        ms = jnp.mean(sl * sl, axis=0, keepdims=True)
        qt_ref[hd * HEAD_DIM:(hd + 1) * HEAD_DIM, :] = (sl * lax.rsqrt(ms + EPS) * qn).astype(qt_ref.dtype)
    vt_ref[...] = pt[ATT_W:ATT_W + KV_W]
    gt_ref[...] = _silu(pt[ATT_W + KV_W:])
    kk = _dot(h, wk_ref[...])
    kn = kn_ref[...]
    for hd in range(N_KV):
        k_ref[:, hd * HEAD_DIM:(hd + 1) * HEAD_DIM] = _rms(kk[:, hd * HEAD_DIM:(hd + 1) * HEAD_DIM], kn)


def _swa_proj_t(x, nw, wt, wk, qn_col, kn, tm):
    rows, d = x.shape
    nt = wt.shape[0]
    row = lambda i: (i, 0)
    col = lambda i: (0, i)
    fixed = lambda i: (0, 0)
    return pl.pallas_call(
        _swa_proj_t_kernel,
        grid=(rows // tm,),
        in_specs=[pl.BlockSpec((tm, d), row), pl.BlockSpec((1, d), fixed), pl.BlockSpec((nt, d), fixed),
                  pl.BlockSpec((d, KV_W), fixed), pl.BlockSpec((HEAD_DIM, 1), fixed),
                  pl.BlockSpec((1, HEAD_DIM), fixed)],
        out_specs=[pl.BlockSpec((ATT_W, tm), col), pl.BlockSpec((tm, KV_W), row),
                   pl.BlockSpec((KV_W, tm), col), pl.BlockSpec((ATT_W, tm), col)],
        out_shape=[jax.ShapeDtypeStruct((ATT_W, rows), BF16), jax.ShapeDtypeStruct((rows, KV_W), F32),
                   jax.ShapeDtypeStruct((KV_W, rows), F32), jax.ShapeDtypeStruct((ATT_W, rows), F32)],
        compiler_params=_cparams("parallel"),
        name="swa_proj_t",
    )(x, nw, wt, wk, qn_col, kn)


def _resid_matmul_kernel(u_ref, w_ref, x_ref, o_ref):
    o_ref[...] = x_ref[...] + _dot(u_ref[...], w_ref[...])


def _gated_resid_matmul_kernel(a_ref, g_ref, w_ref, x_ref, o_ref):
    u = (a_ref[...] * g_ref[...]).astype(BF16)
    o_ref[...] = x_ref[...] + _dot(u, w_ref[...])


def _resid_matmul(u, w, x, tm, gate=None):
    rows, k = u.shape
    d = w.shape[1]
    row = lambda i: (i, 0)
    ins = [u] if gate is None else [u, gate]
    body = _resid_matmul_kernel if gate is None else _gated_resid_matmul_kernel
    return pl.pallas_call(
        body,
        grid=(rows // tm,),
        in_specs=[pl.BlockSpec((tm, k), row)] * len(ins)
                 + [pl.BlockSpec((k, d), lambda i: (0, 0)), pl.BlockSpec((tm, d), row)],
        out_specs=pl.BlockSpec((tm, d), row),
        out_shape=jax.ShapeDtypeStruct((rows, d), F32),
        compiler_params=_cparams("parallel"),
        name="resid_matmul",
    )(*ins, w, x)


def _bias_table_kernel(idx_ref, tab_ref, o_ref):
    idx = idx_ref[...]
    for hd in range(N_HEADS):
        acc = jnp.zeros(idx.shape, F32)
        for b in range(N_BUCKETS):
            acc = jnp.where(idx == b, tab_ref[b, hd], acc)
        o_ref[hd] = acc


def _bias_table(idx, rel_bias):
    r, c = idx.shape
    return pl.pallas_call(
        _bias_table_kernel,
        in_specs=[pl.BlockSpec((r, c), lambda: (0, 0)), pl.BlockSpec(memory_space=pltpu.SMEM)],
        out_specs=pl.BlockSpec((N_HEADS, r, c), lambda: (0, 0, 0)),
        out_shape=jax.ShapeDtypeStruct((N_HEADS, r, c), F32),
        name="bias_table",
    )(idx, rel_bias)


def _rel_bucket(dist):
    n = jnp.maximum(dist, 0)
    max_exact = N_BUCKETS // 2
    nf = jnp.maximum(n, 1).astype(F32)
    large = max_exact + (jnp.log(nf / max_exact) / math.log(MAX_DISTANCE / max_exact)
                         * (N_BUCKETS - max_exact)).astype(jnp.int32)
    large = jnp.minimum(large, N_BUCKETS - 1)
    return jnp.where(n < max_exact, n, large)


def _mask_and_bucket(q_pos, k_pos, k_valid):
    dist = q_pos[:, None] - k_pos[None, :]
    mask = k_valid[None, :] & (dist >= 0) & ((dist <= WINDOW) | (k_pos[None, :] < N_META))
    return mask, _rel_bucket(dist)


def _swa_prompt_kernel(qt_ref, kc_ref, kp_ref, vtc_ref, vtp_ref, mk_ref, mvt_ref, gt_ref, x_ref,
                       bias_ref, mask_ref, sink_ref, w_ref, o_ref, att_ref):
    kband = jnp.concatenate([kp_ref[...], kc_ref[...], mk_ref[...]], axis=0).astype(BF16)
    vtband = jnp.concatenate([vtp_ref[...], vtc_ref[...], mvt_ref[...]], axis=1).astype(BF16)
    qt = qt_ref[...]
    keep = mask_ref[...] > 0.5
    scale = HEAD_DIM ** -0.5
    for kv in range(N_KV):
        qt4 = jnp.concatenate(
            [qt[(kv * GROUP + g) * HEAD_DIM:(kv * GROUP + g + 1) * HEAD_DIM] for g in range(GROUP)], axis=1)
        kk = kband[:, kv * HEAD_DIM:(kv + 1) * HEAD_DIM]
        vv = vtband[kv * HEAD_DIM:(kv + 1) * HEAD_DIM]
        logits = _dot(kk, qt4) * scale + bias_ref[kv]
        logits = jnp.where(keep, logits, NEG_INF)
        sk = sink_ref[kv]
        m = jnp.maximum(jnp.max(logits, axis=0, keepdims=True), sk)
        p = jnp.exp(logits - m)
        den = jnp.sum(p, axis=0, keepdims=True) + jnp.exp(sk - m)
        ot4 = _dot(vv, p.astype(BF16)) * (1.0 / den)
        for g in range(GROUP):
            hd = kv * GROUP + g
            att_ref[hd * HEAD_DIM:(hd + 1) * HEAD_DIM, :] = ot4[:, g * ATT_BLOCK:(g + 1) * ATT_BLOCK]
    ut = (att_ref[...] * gt_ref[...]).astype(BF16)
    o_ref[...] = x_ref[...] + _dot_tn(ut, w_ref[...])


def _swa_prompt(qt, kn, vt, sgt, x, mk, mvt, bias_t, mask_t, sink_t, w_out, batch, n_blk):
    cur = lambda b, i: (b * n_blk + i, 0)
    prev = lambda b, i: (b * n_blk + jnp.maximum(i - 1, 0), 0)
    cur_t = lambda b, i: (0, b * n_blk + i)
    prev_t = lambda b, i: (0, b * n_blk + jnp.maximum(i - 1, 0))
    meta = lambda b, i: (b, 0, 0)
    tk = N_META + 2 * ATT_BLOCK
    nq = GROUP * ATT_BLOCK
    rows = kn.shape[0]
    return pl.pallas_call(
        _swa_prompt_kernel,
        grid=(batch, n_blk),
        in_specs=[pl.BlockSpec((ATT_W, ATT_BLOCK), cur_t),
                  pl.BlockSpec((ATT_BLOCK, KV_W), cur), pl.BlockSpec((ATT_BLOCK, KV_W), prev),
                  pl.BlockSpec((KV_W, ATT_BLOCK), cur_t), pl.BlockSpec((KV_W, ATT_BLOCK), prev_t),
                  pl.BlockSpec((None, N_META, KV_W), meta), pl.BlockSpec((None, KV_W, N_META), meta),
                  pl.BlockSpec((ATT_W, ATT_BLOCK), cur_t), pl.BlockSpec((ATT_BLOCK, D_MODEL), cur),
                  pl.BlockSpec((None, N_KV, tk, nq), lambda b, i: (jnp.minimum(i, 1), 0, 0, 0)),
                  pl.BlockSpec((None, tk, nq), lambda b, i: (jnp.minimum(i, 1), 0, 0)),
                  pl.BlockSpec((N_KV, 1, nq), lambda b, i: (0, 0, 0)),
                  pl.BlockSpec((ATT_W, D_MODEL), lambda b, i: (0, 0))],
        out_specs=pl.BlockSpec((ATT_BLOCK, D_MODEL), cur),
        out_shape=jax.ShapeDtypeStruct((rows, D_MODEL), F32),
        scratch_shapes=[pltpu.VMEM((ATT_W, ATT_BLOCK), F32)],
        compiler_params=_cparams("parallel", "parallel"),
        name="swa_prompt",
    )(qt, kn, kn, vt, vt, mk, mvt, sgt, x, bias_t, mask_t, sink_t, w_out)


def _swa_small_kernel(n_seg, *refs):
    q_ref = refs[0]
    k_refs = refs[1:1 + n_seg]
    v_refs = refs[1 + n_seg:1 + 2 * n_seg]
    bias_ref, mask_ref, sink_ref, o_ref = refs[1 + 2 * n_seg:]
    keep = mask_ref[...] > 0.5
    scale = HEAD_DIM ** -0.5
    kall = jnp.concatenate([r[...] for r in k_refs], axis=0).astype(BF16) if n_seg > 1 else k_refs[0][...].astype(BF16)
    vall = jnp.concatenate([r[...] for r in v_refs], axis=0).astype(BF16) if n_seg > 1 else v_refs[0][...].astype(BF16)
    for kv in range(N_KV):
        kk = kall[:, kv * HEAD_DIM:(kv + 1) * HEAD_DIM]
        vv = vall[:, kv * HEAD_DIM:(kv + 1) * HEAD_DIM]
        logits = _dot_nt(q_ref[kv], kk) * scale + bias_ref[kv]
        logits = jnp.where(keep, logits, NEG_INF)
        sk = sink_ref[kv]
        m = jnp.maximum(jnp.max(logits, axis=-1, keepdims=True), sk)
        p = jnp.exp(logits - m)
        den = jnp.sum(p, axis=-1, keepdims=True) + jnp.exp(sk - m)
        o_ref[kv] = _dot(p.astype(BF16), vv) / den


def _swa_small(q, k_segs, v_segs, seg_len, bias, mask, sink, group, rq):
    rows = q.shape[1]
    n_seg = len(k_segs)
    tk = group * sum(seg_len)
    seg_specs = [pl.BlockSpec((group * n, KV_W), lambda i: (i, 0)) for n in seg_len]
    return pl.pallas_call(
        functools.partial(_swa_small_kernel, n_seg),
        grid=(rows // (group * rq),),
        in_specs=[pl.BlockSpec((N_KV, group * rq, HEAD_DIM), lambda i: (0, i, 0))] + seg_specs + seg_specs
                 + [pl.BlockSpec((N_KV, group * rq, tk), lambda i: (0, 0, 0)),
                    pl.BlockSpec((group * rq, tk), lambda i: (0, 0)),
                    pl.BlockSpec((N_KV, group * rq, 1), lambda i: (0, 0, 0))],
        out_specs=pl.BlockSpec((N_KV, group * rq, HEAD_DIM), lambda i: (0, i, 0)),
        out_shape=jax.ShapeDtypeStruct((N_KV, rows, HEAD_DIM), F32),
        compiler_params=_cparams("parallel"),
        name="swa_small",
    )(q, *k_segs, *v_segs, bias, mask, sink)


def _to_group_rows(a, nb, t):
    a = a.reshape(nb, t, N_KV, GROUP, HEAD_DIM)
    return jnp.transpose(a, (2, 0, 3, 1, 4)).reshape(N_KV, nb * GROUP * t, HEAD_DIM)


def _from_group_rows(a, nb, t):
    a = a.reshape(N_KV, nb, GROUP, t, HEAD_DIM)
    return jnp.transpose(a, (1, 3, 0, 2, 4)).reshape(nb * t, ATT_W)


def _bias_group_rows(b, t):
    return b.reshape(N_KV, GROUP * t, b.shape[-1])


def _block_diag_keys(a, group, seg_len):
    eye = jnp.eye(group, dtype=a.dtype)
    out, o = [], 0
    for n in seg_len:
        blk = a[..., o:o + n]
        o += n
        big = eye[:, None, :, None] * blk[..., None, :, None, :]
        out.append(big.reshape(a.shape[:-2] + (group * a.shape[-2], group * n)))
    return jnp.concatenate(out, axis=-1)


def _conv_silu(xs, cw_ref, cb_ref):
    cw = cw_ref[...]
    acc = xs[0] * cw[0:1]
    for w in range(1, CONV_W):
        acc = acc + xs[w] * cw[w:w + 1]
    return _silu(acc + cb_ref[...])


def _norm_proj_conv_kernel(x_ref, nw_ref, w_ref, cs_ref, cw_ref, cb_ref, xm_ref, xc_ref, halo_ref):
    i = pl.program_id(1)
    h = _rms(x_ref[...], nw_ref[...]).astype(BF16)
    xm = _dot(h, w_ref[...])
    tm = xm.shape[0]
    xm_ref[...] = xm.astype(xm_ref.dtype)
    halo = jnp.where(i == 0, cs_ref[...], halo_ref[...])
    full = jnp.concatenate([halo, xm], axis=0)
    n = tm + 8
    xs = [pltpu.roll(full, n - (8 - (CONV_W - 1) + w), 0)[:tm] for w in range(CONV_W - 1)] + [xm]
    xc_ref[...] = _conv_silu(xs, cw_ref, cb_ref).astype(xc_ref.dtype)
    halo_ref[...] = xm[tm - 8:]


def _norm_proj_conv(x3, nw, w, cs, cw, cb, tm):
    b, t, d = x3.shape
    cur = lambda bi, i: (bi, i, 0)
    fixed = lambda bi, i: (0, 0)
    return pl.pallas_call(
        _norm_proj_conv_kernel,
        grid=(b, t // tm),
        in_specs=[pl.BlockSpec((None, tm, d), cur), pl.BlockSpec((1, d), fixed), pl.BlockSpec((d, M_INNER), fixed),
                  pl.BlockSpec((None, 8, M_INNER), lambda bi, i: (bi, 0, 0)),
                  pl.BlockSpec((CONV_W, M_INNER), fixed), pl.BlockSpec((1, M_INNER), fixed)],
        out_specs=[pl.BlockSpec((None, tm, M_INNER), cur)] * 2,
        out_shape=[jax.ShapeDtypeStruct((b, t, M_INNER), BF16)] * 2,
        scratch_shapes=[pltpu.VMEM((8, M_INNER), F32)],
        compiler_params=_cparams("parallel", "arbitrary"),
        name="norm_proj_conv",
    )(x3, nw, w, cs, cw, cb)


def _qkv_matmuls(xcb, xmb, wq_ref, wk_ref, wv_ref, wg_ref, bg_ref, q_ref, k_ref, v_ref, gt_ref):
    gates = bg_ref[...]
    for h in range(M_HEADS):
        sl = slice(h * M_HD, (h + 1) * M_HD)
        qh = _dot(xcb[:, sl], wq_ref[h]).astype(BF16)
        kh = (_dot(xcb[:, sl], wk_ref[h]) * (M_HD ** -0.5)).astype(BF16)
        vh = _dot(xmb[:, sl], wv_ref[h]).astype(BF16)
        q_ref[:, sl] = qh
        k_ref[:, sl] = kh
        v_ref[:, sl] = vh
        gates = (gates + _dot(qh, wg_ref[h * M_HD:(h + 1) * M_HD])
                 + _dot(kh, wg_ref[M_INNER + h * M_HD:M_INNER + (h + 1) * M_HD])
                 + _dot(vh, wg_ref[2 * M_INNER + h * M_HD:2 * M_INNER + (h + 1) * M_HD]))
    lane = lax.broadcasted_iota(jnp.int32, gates.shape, 1)
    log_f = jnp.minimum(gates, 0.0) - jnp.log1p(jnp.exp(-jnp.abs(gates)))
    gt_ref[...] = jnp.where(lane < M_HEADS, gates, log_f)


def _qkv_kernel(xc_ref, xm_ref, *rest):
    _qkv_matmuls(xc_ref[...], xm_ref[...], *rest)


def _qkv_shift_kernel(x0_ref, x1_ref, x2_ref, x3_ref, cw_ref, cb_ref, *rest):
    xs = [x0_ref[...], x1_ref[...], x2_ref[...], x3_ref[...]]
    xcb = _conv_silu(xs, cw_ref, cb_ref).astype(BF16)
    xc_ref = rest[-1]
    xc_ref[...] = xcb
    _qkv_matmuls(xcb, xs[CONV_W - 1].astype(BF16), *rest[:-1])


def _qkv_weight_specs(fixed2, fixed3):
    return [pl.BlockSpec((M_HEADS, M_HD, M_HD), fixed3), pl.BlockSpec((M_HEADS, M_HD, M_HD), fixed3),
            pl.BlockSpec((M_HEADS, M_HD, M_HD), fixed3),
            pl.BlockSpec((3 * M_INNER, 2 * M_HEADS), fixed2), pl.BlockSpec((1, 2 * M_HEADS), fixed2)]


def _qkv_real(xc3, xm3, wts, tm):
    b, t, _ = xc3.shape
    cur = lambda bi, i: (bi, i, 0)
    outs = [jax.ShapeDtypeStruct((b, t, M_INNER), BF16)] * 3 + [jax.ShapeDtypeStruct((b, t, 2 * M_HEADS), F32)]
    return pl.pallas_call(
        _qkv_kernel,
        grid=(b, t // tm),
        in_specs=[pl.BlockSpec((None, tm, M_INNER), cur)] * 2
                 + _qkv_weight_specs(lambda bi, i: (0, 0), lambda bi, i: (0, 0, 0)),
        out_specs=[pl.BlockSpec((None, tm, M_INNER), cur)] * 3 + [pl.BlockSpec((None, tm, 2 * M_HEADS), cur)],
        out_shape=outs,
        compiler_params=_cparams("parallel", "parallel"),
        name="mlstm_qkv_real",
    )(xc3, xm3, *wts)


def _qkv_small(xs, conv_wts, wts, tm):
    rows = xs[0].shape[0]
    row = lambda i: (i, 0)
    fixed = lambda i: (0, 0)
    outs = ([jax.ShapeDtypeStruct((rows, M_INNER), BF16)] * 3 + [jax.ShapeDtypeStruct((rows, 2 * M_HEADS), F32)]
            + [jax.ShapeDtypeStruct((rows, M_INNER), BF16)])
    return pl.pallas_call(
        _qkv_shift_kernel,
        grid=(rows // tm,),
        in_specs=[pl.BlockSpec((tm, M_INNER), row)] * CONV_W
                 + [pl.BlockSpec((CONV_W, M_INNER), fixed), pl.BlockSpec((1, M_INNER), fixed)]
                 + _qkv_weight_specs(fixed, lambda i: (0, 0, 0)),
        out_specs=[pl.BlockSpec((tm, M_INNER), row)] * 3 + [pl.BlockSpec((tm, 2 * M_HEADS), row)]
                  + [pl.BlockSpec((tm, M_INNER), row)],
        out_shape=outs,
        compiler_params=_cparams("parallel"),
        name="mlstm_qkv_small",
    )(*xs, *conv_wts, *wts)


def _chunk_kernel(seg, aliased, q_ref, k_ref, v_ref, xc_ref, z_ref, op_ref, gc_ref, gr_ref, c0_ref, n0_ref, m0_ref,
                  hn_ref, sk_ref, *rest):
    u_ref, c_ref, n_ref, m_ref = rest[1:] if aliased else rest
    h_idx = pl.program_id(1)
    c_idx = pl.program_id(2)
    L = q_ref.shape[0]
    nseg = L // seg
    hi = lax.Precision.HIGHEST

    @pl.when(c_idx == 0)
    def _():
        c_ref[...] = c0_ref[...]
        n_ref[...] = n0_ref[...]
        m_ref[...] = m0_ref[...]

    gc = gc_ref[...]
    gr = gr_ref[...]
    col_c = lax.broadcasted_iota(jnp.int32, gc.shape, 1)
    row_r = lax.broadcasted_iota(jnp.int32, gr.shape, 0)
    i_c = jnp.sum(jnp.where(col_c == h_idx, gc, 0.0), axis=1, keepdims=True)
    i_r = jnp.sum(jnp.where(row_r == h_idx, gr, 0.0), axis=0, keepdims=True)
    lf_c = jnp.where(col_c == M_HEADS + h_idx, gc, 0.0)
    lf_r = jnp.where(row_r == M_HEADS + h_idx, gr, 0.0)
    t_i = lax.broadcasted_iota(jnp.int32, (L, L), 0)
    j_i = lax.broadcasted_iota(jnp.int32, (L, L), 1)
    causal = j_i <= t_i
    upper = t_i <= j_i
    if nseg > 1:
        sid_c = lax.broadcasted_iota(jnp.int32, (L, 1), 0) // seg
        same = (t_i // seg) == (j_i // seg)
        causal = causal & same
        upper = upper & same
    tril = jnp.where(causal, 1.0, 0.0).astype(F32)
    triu = jnp.where(upper, 1.0, 0.0).astype(F32)
    b_c = jnp.sum(jnp.dot(tril, lf_c, precision=hi, preferred_element_type=F32), axis=1, keepdims=True)
    b_r = jnp.sum(jnp.dot(lf_r, triu, precision=hi, preferred_element_type=F32), axis=0, keepdims=True)
    if nseg > 1:
        g = jnp.sum(jnp.dot(jnp.where(same, 1.0, 0.0).astype(F32), lf_c, precision=hi, preferred_element_type=F32),
                    axis=1, keepdims=True)
        pick = lambda vals: sum(jnp.where(sid_c == s, vals[s], 0.0) for s in range(nseg))
        m_prev = pick([m_ref[s][:, 0:1] for s in range(nseg)])
    else:
        g = b_c[L - 1:L, :]
        m_prev = m_ref[0][:, 0:1]

    log_d = jnp.where(causal, b_c + (i_r - b_r), -jnp.inf)
    inter = b_c + m_prev
    m_t = jnp.maximum(inter, jnp.max(log_d, axis=-1, keepdims=True))
    w_intra = jnp.exp(log_d - m_t)
    w_inter = jnp.exp(inter - m_t)
    q = q_ref[...]
    k = k_ref[...]
    v = v_ref[...]
    s_mat = _dot_nt(q, k) * w_intra
    qf = q.astype(F32)
    if nseg > 1:
        q_c = pick([_dot(q, c_ref[s].astype(BF16)) for s in range(nseg)])
        q_n = pick([jnp.sum(qf * n_ref[s], axis=-1, keepdims=True) for s in range(nseg)])
    else:
        q_c = _dot(q, c_ref[0].astype(BF16))
        q_n = jnp.sum(qf * n_ref[0], axis=-1, keepdims=True)
    num = _dot(s_mat.astype(BF16), v) + w_inter * q_c
    den = jnp.sum(s_mat, axis=-1, keepdims=True) + w_inter * q_n
    h = num * (1.0 / jnp.maximum(jnp.abs(den), jnp.exp(-m_t)))

    lw = (g - b_c) + i_c
    kf = k.astype(F32)
    if nseg > 1:
        in_seg = [sid_c == s for s in range(nseg)]
        g_s = [jnp.max(jnp.where(in_seg[s], g, -jnp.inf), axis=0, keepdims=True) for s in range(nseg)]
        m_old = [m_ref[s][:, 0:1] for s in range(nseg)]
        m_new = [jnp.maximum(g_s[s] + m_old[s], jnp.max(jnp.where(in_seg[s], lw, -jnp.inf), axis=0, keepdims=True))
                 for s in range(nseg)]
        kw = kf * jnp.exp(lw - pick(m_new))
        for s in range(nseg):
            decay = jnp.exp(g_s[s] + m_old[s] - m_new[s])
            kw_s = jnp.where(in_seg[s], kw, 0.0)
            c_ref[s] = decay * c_ref[s] + _dot_tn(kw_s.astype(BF16), v)
            n_ref[s] = decay * n_ref[s] + jnp.sum(kw_s, axis=0, keepdims=True)
            m_ref[s] = jnp.broadcast_to(m_new[s], m_ref.shape[1:])
    else:
        m_new = jnp.maximum(g + m_prev, jnp.max(lw, axis=0, keepdims=True))
        decay = jnp.exp(g + m_prev - m_new)
        kw = kf * jnp.exp(lw - m_new)
        c_ref[0] = decay * c_ref[0] + _dot_tn(kw.astype(BF16), v)
        n_ref[0] = decay * n_ref[0] + jnp.sum(kw, axis=0, keepdims=True)
        m_ref[0] = jnp.broadcast_to(m_new, m_ref.shape[1:])

    mu = jnp.mean(h, axis=-1, keepdims=True)
    hc = h - mu
    var = jnp.mean(hc * hc, axis=-1, keepdims=True)
    h_out = _sigmoid(op_ref[...].astype(F32)) * (hc * lax.rsqrt(var + EPS) * hn_ref[...])
    u = (h_out + sk_ref[...] * xc_ref[...].astype(F32)) * _silu(z_ref[...].astype(F32))
    u_ref[...] = u.astype(u_ref.dtype)


def _mlstm_chunk(q, k, v, xc, zo, gates, c0, n0, m0, lyr_in, head_norm, skip, chunk, seg,
                 c_acc=None, lyr_out=0, n_lyr_out=1):
    nb, t, _ = q.shape
    nc = t // chunk
    nseg = chunk // seg
    assert nseg == 1 or nc == 1
    n_seq = nb * nseg
    gates_t = jnp.swapaxes(gates, 1, 2)
    tok = lambda bi, h, c: (bi, c, h)
    hd_blk = pl.BlockSpec((None, chunk, M_HD), tok)
    st_in = lambda bi, h, c: (lyr_in, bi, h, 0, 0)
    st_out = lambda bi, h, c: (0, bi, h, 0, 0)
    c_out = lambda bi, h, c: (lyr_out, bi, h, 0, 0)
    head_row = pl.BlockSpec((1, M_HD), lambda bi, h, c: (0, h))
    in_specs = [hd_blk, hd_blk, hd_blk, hd_blk,
                hd_blk,
                pl.BlockSpec((None, chunk, M_HD), lambda bi, h, c: (bi, c, M_HEADS + h)),
                pl.BlockSpec((None, chunk, 2 * M_HEADS), lambda bi, h, c: (bi, c, 0)),
                pl.BlockSpec((None, 2 * M_HEADS, chunk), lambda bi, h, c: (bi, 0, c)),
                pl.BlockSpec((None, nseg, None, M_HD, M_HD), st_in),
                pl.BlockSpec((None, nseg, None, 1, M_HD), st_in),
                pl.BlockSpec((None, nseg, None, 1, 128), st_in),
                head_row, head_row]
    args = [q, k, v, xc, zo, zo, gates, gates_t, c0, n0, m0, head_norm, skip]
    aliases = {}
    if c_acc is not None:
        in_specs.append(pl.BlockSpec(memory_space=pl.ANY))
        aliases = {len(args): 1}
        args.append(c_acc)
    return pl.pallas_call(
        functools.partial(_chunk_kernel, seg, c_acc is not None),
        grid=(nb, M_HEADS, nc),
        in_specs=in_specs,
        out_specs=[hd_blk,
                   pl.BlockSpec((None, nseg, None, M_HD, M_HD), c_out),
                   pl.BlockSpec((None, nseg, None, 1, M_HD), st_out),
                   pl.BlockSpec((None, nseg, None, 1, 128), st_out)],
        out_shape=[jax.ShapeDtypeStruct((nb, t, M_INNER), BF16),
                   jax.ShapeDtypeStruct((n_lyr_out, n_seq, M_HEADS, M_HD, M_HD), F32),
                   jax.ShapeDtypeStruct((1, n_seq, M_HEADS, 1, M_HD), F32),
                   jax.ShapeDtypeStruct((1, n_seq, M_HEADS, 1, 128), F32)],
        input_output_aliases=aliases,
        compiler_params=_cparams("parallel", "parallel", "arbitrary"),
        name="mlstm_chunk",
    )(*args)


def kernel(x_prompt, x_sample, cache_swa_k, cache_swa_v, cache_swa_meta_k, cache_swa_meta_v, state_mlstm_c, state_mlstm_n, state_mlstm_m, state_mlstm_conv, meta_tokens, rel_bias, norm_w, swa_w_in, swa_q_norm, swa_k_norm, swa_sinks, swa_w_out, mlstm_w_in, mlstm_conv_w, mlstm_conv_b, mlstm_wq, mlstm_wk, mlstm_wv, mlstm_w_gates, mlstm_b_gates, mlstm_head_norm, mlstm_skip, mlstm_w_out):
    B, T, D = x_prompt.shape
    DB, DT, _ = x_sample.shape
    depth = norm_w.shape[0]
    n_buf = cache_swa_k.shape[2]
    n_blk = T // ATT_BLOCK
    n_real, n_samp, n_meta = B * T, DB * DT, B * N_META
    n_small = -(-(n_samp + n_meta) // SMALL_ROW_TILE) * SMALL_ROW_TILE
    pad_small = n_small - n_samp - n_meta
    s_meta = slice(n_samp, n_samp + n_meta)

    def small_rows(samp, meta):
        return jnp.concatenate([samp, meta, jnp.zeros((pad_small,) + samp.shape[1:], samp.dtype)], axis=0)

    xr = x_prompt.reshape(n_real, D)
    xs = small_rows(x_sample.reshape(n_samp, D),
                    jnp.broadcast_to(meta_tokens.astype(x_prompt.dtype)[None], (B, N_META, D)).reshape(n_meta, D))

    ar = lambda n: jnp.arange(n, dtype=jnp.int32)
    meta_pos = ar(N_META)
    geo = []
    for blk in (0, 1):
        q_pos = N_META + blk * ATT_BLOCK + ar(ATT_BLOCK)
        band_pos = N_META + (blk - 1) * ATT_BLOCK + ar(2 * ATT_BLOCK)
        k_pos = jnp.concatenate([meta_pos, band_pos])
        k_valid = jnp.concatenate([jnp.ones((N_META,), bool), band_pos >= N_META])
        geo.append(_mask_and_bucket(q_pos, k_pos, k_valid))
    tk_p = N_META + 2 * ATT_BLOCK
    mask_m, buck_m = _mask_and_bucket(meta_pos, meta_pos, jnp.ones((N_META,), bool))
    buf_pos = PAST_LEN - n_buf + ar(n_buf)
    new_pos = PAST_LEN + ar(DT)
    k_pos_s = jnp.concatenate([meta_pos, buf_pos, new_pos])
    k_valid_s = jnp.concatenate([jnp.ones((N_META,), bool), buf_pos >= N_META, jnp.ones((DT,), bool)])
    mask_s, buck_s = _mask_and_bucket(new_pos, k_pos_s, k_valid_s)
    tk_s = N_META + n_buf + DT

    def place(a, rows):
        return jnp.pad(a, ((0, rows - a.shape[0]), (0, tk_p - a.shape[1])))

    idx_all = jnp.concatenate([geo[0][1], geo[1][1], place(buck_m, N_META), place(buck_s, 8)], axis=0)
    bias_all = _bias_table(idx_all, rel_bias.astype(F32))
    bias_p = jnp.transpose(bias_all[:, :2 * ATT_BLOCK].reshape(N_HEADS, 2, ATT_BLOCK, tk_p), (1, 0, 2, 3))
    bias_p = bias_p.reshape(2, N_KV, GROUP * ATT_BLOCK, tk_p)
    mask_p = jnp.stack([jnp.tile(geo[0][0], (GROUP, 1)), jnp.tile(geo[1][0], (GROUP, 1))]).astype(F32)
    key_major = lambda a: jnp.swapaxes(jnp.concatenate([a[..., N_META:], a[..., :N_META]], axis=-1), -1, -2)
    bias_pt, mask_pt = key_major(bias_p), key_major(mask_p)
    o0 = 2 * ATT_BLOCK
    seg_m, seg_s = (N_META,), (N_META, n_buf, DT)
    grp_s = SAMPLE_ATT_GROUP
    bias_m = _block_diag_keys(_bias_group_rows(bias_all[:, o0:o0 + N_META, :N_META], N_META), B, seg_m)
    mask_mg = _block_diag_keys(jnp.tile(mask_m, (GROUP, 1)).astype(F32), B, seg_m)
    o1 = o0 + N_META
    bias_s = _block_diag_keys(_bias_group_rows(bias_all[:, o1:o1 + DT, :tk_s], DT), grp_s, seg_s)
    mask_sg = _block_diag_keys(jnp.tile(mask_s, (GROUP, 1)).astype(F32), grp_s, seg_s)

    def sink_rows(sinks, t, n_seq=1):
        rows = jnp.repeat(sinks.astype(F32).reshape(N_KV, GROUP, 1), t, axis=2).reshape(N_KV, GROUP * t, 1)
        return jnp.tile(rows, (1, n_seq, 1))

    swa_p, swa_s, ml_p, ml_s = [], [], [], []
    c_samp = None
    assert n_samp % SMALL_CHUNK == 0 and SMALL_CHUNK % DT == 0
    for layer in range(depth):
        j = layer // 2
        nw = norm_w[layer].astype(F32).reshape(1, D)
        if layer % 2 == 0:
            w_in = swa_w_in[j].astype(BF16)
            w_out = swa_w_out[j].astype(BF16)
            qn = swa_q_norm[j].astype(F32).reshape(1, HEAD_DIM)
            kn = swa_k_norm[j].astype(F32).reshape(1, HEAD_DIM)
            w_t = jnp.concatenate([swa_w_in[j][:, :ATT_W].T, swa_w_in[j][:, ATT_W + KV_W:ATT_W + 2 * KV_W].T,
                                   swa_w_in[j][:, ATT_W + 2 * KV_W:].T], axis=0).astype(BF16)
            qt_r, k_r, vt_r, gt_r = _swa_proj_t(xr, nw, w_t, w_in[:, ATT_W:ATT_W + KV_W], qn.reshape(HEAD_DIM, 1), kn,
                                                PROJ_T_ROW_TILE)
            q_s, k_s, v_s, g_s = _swa_proj(xs, nw, w_in, qn, kn, SMALL_ROW_TILE)
            mk = k_s[s_meta].reshape(B, N_META, KV_W)
            mv = v_s[s_meta].reshape(B, N_META, KV_W)
            xr = _swa_prompt(qt_r, k_r, vt_r, gt_r, xr, mk, jnp.swapaxes(mv, 1, 2), bias_pt, mask_pt,
                             jnp.swapaxes(sink_rows(swa_sinks[j], ATT_BLOCK), 1, 2), w_out, B, n_blk)
            k_new = k_s[:n_samp]
            v_new = v_s[:n_samp]
            ck = cache_swa_k[j].astype(F32).reshape(DB * n_buf, KV_W)
            cv = cache_swa_v[j].astype(F32).reshape(DB * n_buf, KV_W)
            cmk = cache_swa_meta_k[j].astype(F32).reshape(DB * N_META, KV_W)
            cmv = cache_swa_meta_v[j].astype(F32).reshape(DB * N_META, KV_W)
            o_samp = _swa_small(_to_group_rows(q_s[:n_samp], DB, DT), [cmk, ck, k_new], [cmv, cv, v_new], seg_s,
                                bias_s, mask_sg, sink_rows(swa_sinks[j], DT, grp_s), grp_s, GROUP * DT)
            o_meta = _swa_small(_to_group_rows(q_s[s_meta], B, N_META), [k_s[s_meta]], [v_s[s_meta]], seg_m,
                                bias_m, mask_mg, sink_rows(swa_sinks[j], N_META, B), B, GROUP * N_META)
            o_small = small_rows(_from_group_rows(o_samp, DB, DT), _from_group_rows(o_meta, B, N_META))
            xs = _resid_matmul(o_small, w_out, xs, SMALL_ROW_TILE, gate=g_s)
            n_keep = min(WINDOW, T + N_META)
            last = lambda a: a.reshape(B, T, KV_W)[:, T - n_keep:].reshape(B, n_keep, N_KV, HEAD_DIM)
            v_last = jnp.swapaxes(vt_r.reshape(KV_W, B, T)[:, :, T - n_keep:], 0, 1)
            swa_p.append((last(k_r), jnp.swapaxes(v_last, 1, 2).reshape(B, n_keep, N_KV, HEAD_DIM),
                          mk.reshape(B, N_META, N_KV, HEAD_DIM), mv.reshape(B, N_META, N_KV, HEAD_DIM)))
            win = lambda c, new: jnp.concatenate(
                [c.reshape(DB, n_buf, KV_W), new.reshape(DB, DT, KV_W)], axis=1)[:, DT:].reshape(DB, n_buf, N_KV, HEAD_DIM)
            swa_s.append((win(ck, k_new), win(cv, v_new)))
        else:
            w_in = mlstm_w_in[j].astype(BF16)
            w_out = mlstm_w_out[j].astype(BF16)
            conv_wts = (mlstm_conv_w[j].astype(F32), mlstm_conv_b[j].astype(F32).reshape(1, M_INNER))
            wts = (mlstm_wq[j].astype(BF16), mlstm_wk[j].astype(BF16), mlstm_wv[j].astype(BF16),
                   mlstm_w_gates[j].astype(BF16), mlstm_b_gates[j].astype(F32).reshape(1, 2 * M_HEADS))
            hn = mlstm_head_norm[j].astype(F32).reshape(1, M_INNER)
            sk = mlstm_skip[j].astype(F32).reshape(1, M_INNER)
            p_s = _norm_proj(xs, nw, w_in, SMALL_ROW_TILE, M_INNER)
            xm_samp = p_s[:n_samp, :M_INNER].astype(F32).reshape(DB, DT, M_INNER)
            xm_meta = p_s[s_meta, :M_INNER].astype(F32).reshape(B, N_META, M_INNER)
            xpad_s = jnp.concatenate([state_mlstm_conv[j].astype(F32), xm_samp], axis=1)
            xpad_m = jnp.concatenate([jnp.zeros((B, CONV_W - 1, M_INNER), F32), xm_meta], axis=1)
            shifted = [small_rows(xpad_s[:, w:w + DT].reshape(n_samp, M_INNER),
                                  xpad_m[:, w:w + N_META].reshape(n_meta, M_INNER)) for w in range(CONV_W)]
            q_s, k_s, v_s, gt_s, xc_s = _qkv_small(shifted, conv_wts, wts, SMALL_ROW_TILE)
            zo_s = p_s[:, M_INNER:]
            cs_real = jnp.pad(xpad_m[:, N_META:], ((0, 0), (8 - (CONV_W - 1), 0), (0, 0)))
            xm_r, xc_r = _norm_proj_conv(xr.reshape(B, T, D), nw, w_in[:, :M_INNER], cs_real, *conv_wts, 512)
            zo_r = _norm_proj(xr, nw, w_in[:, M_INNER:], 512, M_INNER).reshape(B, T, 2 * M_INNER)
            q_r, k_r, v_r, gt_r = _qkv_real(xc_r, xm_r, wts, ROW_TILE)

            to3 = lambda a, sl, b, t: a[sl].reshape(b, t, a.shape[-1])
            zc = jnp.zeros((1, B, M_HEADS, M_HD, M_HD), F32)
            zn = jnp.zeros((1, B, M_HEADS, 1, M_HD), F32)
            zm = jnp.zeros((1, B, M_HEADS, 1, 128), F32)
            u_m, c_m, n_m, m_m = _mlstm_chunk(
                to3(q_s, s_meta, B, N_META), to3(k_s, s_meta, B, N_META), to3(v_s, s_meta, B, N_META),
                to3(xc_s, s_meta, B, N_META), to3(zo_s, s_meta, B, N_META), to3(gt_s, s_meta, B, N_META),
                zc, zn, zm, 0, hn, sk, N_META, N_META)
            u_r, c_r, n_r, m_r = _mlstm_chunk(q_r, k_r, v_r, xc_r, zo_r, gt_r, c_m, n_m, m_m, 0, hn, sk,
                                              PROMPT_CHUNK, PROMPT_CHUNK)
            s_samp = slice(0, n_samp)
            nb_s = n_samp // SMALL_CHUNK
            u_s, c_samp, n_s, m_s = _mlstm_chunk(
                to3(q_s, s_samp, nb_s, SMALL_CHUNK), to3(k_s, s_samp, nb_s, SMALL_CHUNK),
                to3(v_s, s_samp, nb_s, SMALL_CHUNK), to3(xc_s, s_samp, nb_s, SMALL_CHUNK),
                to3(zo_s, s_samp, nb_s, SMALL_CHUNK), to3(gt_s, s_samp, nb_s, SMALL_CHUNK),
                state_mlstm_c.astype(F32), state_mlstm_n.astype(F32)[:, :, :, None, :],
                jnp.broadcast_to(state_mlstm_m.astype(F32)[..., None, None], state_mlstm_m.shape + (1, 128)),
                j, hn, sk, SMALL_CHUNK, DT, c_acc=c_samp, lyr_out=j, n_lyr_out=state_mlstm_c.shape[0])
            xr = _resid_matmul(u_r.reshape(n_real, M_INNER), w_out, xr, ROW_TILE)
            u_small = small_rows(u_s.reshape(n_samp, M_INNER), u_m.reshape(n_meta, M_INNER))
            xs = _resid_matmul(u_small, w_out, xs, SMALL_ROW_TILE)
            ml_p.append((c_r[0], n_r.reshape(B, M_HEADS, M_HD), m_r[0, :, :, 0, 0],
                         xm_r[:, T - (CONV_W - 1):].astype(F32)))
            ml_s.append((n_s.reshape(DB, M_HEADS, M_HD), m_s[0, :, :, 0, 0], xpad_s[:, DT:]))

    sd = state_mlstm_c.dtype
    stack = lambda rows, idx: jnp.stack([r[idx] for r in rows])
    return (xr.reshape(B, T, D), xs[:n_samp].reshape(DB, DT, D),
            stack(swa_p, 0), stack(swa_p, 1), stack(swa_p, 2), stack(swa_p, 3),
            stack(ml_p, 0).astype(sd), stack(ml_p, 1).astype(sd), stack(ml_p, 2).astype(sd), stack(ml_p, 3),
            stack(swa_s, 0), stack(swa_s, 1),
            c_samp.astype(sd), stack(ml_s, 0).astype(sd), stack(ml_s, 1).astype(sd), stack(ml_s, 2))
```

```python
import functools
import math

import jax
import jax.numpy as jnp
from jax import lax
from jax.experimental import pallas as pl
from jax.experimental.pallas import tpu as pltpu

F32 = jnp.float32
BF16 = jnp.bfloat16

D_MODEL = 1024
N_META = 16
EPS = 1e-6
NEG_INF = -1e30
N_HEADS = 16
HEAD_DIM = 64
N_KV = 4
GROUP = 4
ATT_W = N_HEADS * HEAD_DIM
KV_W = N_KV * HEAD_DIM
WINDOW = 128
ATT_BLOCK = 128
N_BUCKETS = 32
MAX_DISTANCE = 128
M_INNER = 2048
M_HEADS = 4
M_HD = 512
CONV_W = 4
PAST_LEN = 8192

VMEM_LIMIT_BYTES = 56 * 1024 * 1024
PROMPT_CHUNK = 256
PROMPT_HEADS_PER_STEP = 4
SMALL_CHUNK = 16
ROW_TILE = 256
PROJ_T_ROW_TILE = 512
SMALL_ROW_TILE = 128
SAMPLE_ATT_GROUP = 8


def _cparams(*sem):
    return pltpu.CompilerParams(dimension_semantics=sem, vmem_limit_bytes=VMEM_LIMIT_BYTES)


def _rms(x, w):
    ms = jnp.mean(x * x, axis=-1, keepdims=True)
    return x * lax.rsqrt(ms + EPS) * w


def _sigmoid(x):
    return 1.0 / (1.0 + jnp.exp(-x))


def _silu(x):
    return x * _sigmoid(x)


def _dot(a, b):
    return jnp.dot(a, b, preferred_element_type=F32)


def _dot_nt(a, b):
    return lax.dot_general(a, b, (((1,), (1,)), ((), ())), preferred_element_type=F32)


def _dot_tn(a, b):
    return lax.dot_general(a, b, (((0,), (0,)), ((), ())), preferred_element_type=F32)


def _norm_proj_kernel(x_ref, nw_ref, w_ref, o_ref):
    h = _rms(x_ref[...], nw_ref[...]).astype(BF16)
    o_ref[...] = _dot(h, w_ref[...]).astype(o_ref.dtype)


def _norm_proj(x, nw, w, tm, tn):
    rows, d = x.shape
    n = w.shape[1]
    return pl.pallas_call(
        _norm_proj_kernel,
        grid=(n // tn, rows // tm),
        in_specs=[pl.BlockSpec((tm, d), lambda j, i: (i, 0)),
                  pl.BlockSpec((1, d), lambda j, i: (0, 0)),
                  pl.BlockSpec((d, tn), lambda j, i: (0, j))],
        out_specs=pl.BlockSpec((tm, tn), lambda j, i: (i, j)),
        out_shape=jax.ShapeDtypeStruct((rows, n), BF16),
        compiler_params=_cparams("parallel", "parallel"),
        name="norm_proj",
    )(x, nw, w)


def _swa_proj_kernel(x_ref, nw_ref, w_ref, qn_ref, kn_ref, q_ref, k_ref, v_ref, g_ref):
    h = _rms(x_ref[...], nw_ref[...]).astype(BF16)
    proj = _dot(h, w_ref[...])
    qn = qn_ref[...]
    kn = kn_ref[...]
    for hd in range(N_HEADS):
        sl = proj[:, hd * HEAD_DIM:(hd + 1) * HEAD_DIM]
        q_ref[:, hd * HEAD_DIM:(hd + 1) * HEAD_DIM] = _rms(sl, qn).astype(q_ref.dtype)
    for hd in range(N_KV):
        sl = proj[:, ATT_W + hd * HEAD_DIM:ATT_W + (hd + 1) * HEAD_DIM]
        k_ref[:, hd * HEAD_DIM:(hd + 1) * HEAD_DIM] = _rms(sl, kn)
    v_ref[...] = proj[:, ATT_W + KV_W:ATT_W + 2 * KV_W]
    g_ref[...] = _silu(proj[:, ATT_W + 2 * KV_W:])


def _swa_proj(x, nw, w, qn, kn, tm):
    rows, d = x.shape
    n = w.shape[1]
    row = lambda i: (i, 0)
    fixed = lambda i: (0, 0)
    return pl.pallas_call(
        _swa_proj_kernel,
        grid=(rows // tm,),
        in_specs=[pl.BlockSpec((tm, d), row), pl.BlockSpec((1, d), fixed), pl.BlockSpec((d, n), fixed),
                  pl.BlockSpec((1, HEAD_DIM), fixed), pl.BlockSpec((1, HEAD_DIM), fixed)],
        out_specs=[pl.BlockSpec((tm, ATT_W), row), pl.BlockSpec((tm, KV_W), row),
                   pl.BlockSpec((tm, KV_W), row), pl.BlockSpec((tm, ATT_W), row)],
        out_shape=[jax.ShapeDtypeStruct((rows, ATT_W), BF16), jax.ShapeDtypeStruct((rows, KV_W), F32),
                   jax.ShapeDtypeStruct((rows, KV_W), F32), jax.ShapeDtypeStruct((rows, ATT_W), F32)],
        compiler_params=_cparams("parallel"),
        name="swa_proj",
    )(x, nw, w, qn, kn)


def _swa_proj_t_kernel(x_ref, nw_ref, wt_ref, qn_ref, kn_ref, qt_ref, k_ref, vt_ref, gt_ref):
    h = _rms(x_ref[...], nw_ref[...]).astype(BF16)
    pt = _dot_nt(wt_ref[...], h)

    def head_norm(r0, gain):
        sl = pt[r0:r0 + HEAD_DIM]
        ms = jnp.mean(sl * sl, axis=0, keepdims=True)
        return sl * lax.rsqrt(ms + EPS) * gain

    qn = qn_ref[...]
    kn = kn_ref[...]
    for hd in range(N_HEADS):
        qt_ref[hd * HEAD_DIM:(hd + 1) * HEAD_DIM, :] = head_norm(hd * HEAD_DIM, qn).astype(qt_ref.dtype)
    kt = jnp.concatenate([head_norm(ATT_W + hd * HEAD_DIM, kn) for hd in range(N_KV)], axis=0)
    k_ref[...] = kt.T
    vt_ref[...] = pt[ATT_W + KV_W:ATT_W + 2 * KV_W]
    gt_ref[...] = _silu(pt[ATT_W + 2 * KV_W:])


def _swa_proj_t(x, nw, wt, qn_col, kn_col, tm):
    rows, d = x.shape
    nt = wt.shape[0]
    row = lambda i: (i, 0)
    col = lambda i: (0, i)
    fixed = lambda i: (0, 0)
    return pl.pallas_call(
        _swa_proj_t_kernel,
        grid=(rows // tm,),
        in_specs=[pl.BlockSpec((tm, d), row), pl.BlockSpec((1, d), fixed), pl.BlockSpec((nt, d), fixed),
                  pl.BlockSpec((HEAD_DIM, 1), fixed), pl.BlockSpec((HEAD_DIM, 1), fixed)],
        out_specs=[pl.BlockSpec((ATT_W, tm), col), pl.BlockSpec((tm, KV_W), row),
                   pl.BlockSpec((KV_W, tm), col), pl.BlockSpec((ATT_W, tm), col)],
        out_shape=[jax.ShapeDtypeStruct((ATT_W, rows), BF16), jax.ShapeDtypeStruct((rows, KV_W), F32),
                   jax.ShapeDtypeStruct((KV_W, rows), F32), jax.ShapeDtypeStruct((ATT_W, rows), F32)],
        compiler_params=_cparams("parallel"),
        name="swa_proj_t",
    )(x, nw, wt, qn_col, kn_col)


def _resid_matmul_kernel(u_ref, w_ref, x_ref, o_ref):
    o_ref[...] = x_ref[...] + _dot(u_ref[...], w_ref[...])


def _gated_resid_matmul_kernel(a_ref, g_ref, w_ref, x_ref, o_ref):
    u = (a_ref[...] * g_ref[...]).astype(BF16)
    o_ref[...] = x_ref[...] + _dot(u, w_ref[...])


def _resid_matmul(u, w, x, tm, gate=None):
    rows, k = u.shape
    d = w.shape[1]
    row = lambda i: (i, 0)
    ins = [u] if gate is None else [u, gate]
    body = _resid_matmul_kernel if gate is None else _gated_resid_matmul_kernel
    return pl.pallas_call(
        body,
        grid=(rows // tm,),
        in_specs=[pl.BlockSpec((tm, k), row)] * len(ins)
                 + [pl.BlockSpec((k, d), lambda i: (0, 0)), pl.BlockSpec((tm, d), row)],
        out_specs=pl.BlockSpec((tm, d), row),
        out_shape=jax.ShapeDtypeStruct((rows, d), F32),
        compiler_params=_cparams("parallel"),
        name="resid_matmul",
    )(*ins, w, x)


def _bias_table_kernel(idx_ref, tab_ref, o_ref):
    idx = idx_ref[...]
    for hd in range(N_HEADS):
        acc = jnp.zeros(idx.shape, F32)
        for b in range(N_BUCKETS):
            acc = jnp.where(idx == b, tab_ref[b, hd], acc)
        o_ref[hd] = acc


def _bias_table(idx, rel_bias):
    r, c = idx.shape
    return pl.pallas_call(
        _bias_table_kernel,
        in_specs=[pl.BlockSpec((r, c), lambda: (0, 0)), pl.BlockSpec(memory_space=pltpu.SMEM)],
        out_specs=pl.BlockSpec((N_HEADS, r, c), lambda: (0, 0, 0)),
        out_shape=jax.ShapeDtypeStruct((N_HEADS, r, c), F32),
        name="bias_table",
    )(idx, rel_bias)


def _rel_bucket(dist):
    n = jnp.maximum(dist, 0)
    max_exact = N_BUCKETS // 2
    nf = jnp.maximum(n, 1).astype(F32)
    large = max_exact + (jnp.log(nf / max_exact) / math.log(MAX_DISTANCE / max_exact)
                         * (N_BUCKETS - max_exact)).astype(jnp.int32)
    large = jnp.minimum(large, N_BUCKETS - 1)
    return jnp.where(n < max_exact, n, large)


def _mask_and_bucket(q_pos, k_pos, k_valid):
    dist = q_pos[:, None] - k_pos[None, :]
    mask = k_valid[None, :] & (dist >= 0) & ((dist <= WINDOW) | (k_pos[None, :] < N_META))
    return mask, _rel_bucket(dist)


def _swa_prompt_kernel(qt_ref, kc_ref, kp_ref, vtc_ref, vtp_ref, mk_ref, mvt_ref, gt_ref, x_ref,
                       bias_ref, mask_ref, sink_ref, w_ref, o_ref, att_ref):
    kband = jnp.concatenate([kp_ref[...], kc_ref[...], mk_ref[...]], axis=0).astype(BF16)
    vtband = jnp.concatenate([vtp_ref[...], vtc_ref[...], mvt_ref[...]], axis=1).astype(BF16)
    qt = qt_ref[...]
    keep = mask_ref[...] > 0.5
    for kv in range(N_KV):
        qt4 = jnp.concatenate(
            [qt[(kv * GROUP + g) * HEAD_DIM:(kv * GROUP + g + 1) * HEAD_DIM] for g in range(GROUP)], axis=1)
        kk = kband[:, kv * HEAD_DIM:(kv + 1) * HEAD_DIM]
        vv = vtband[kv * HEAD_DIM:(kv + 1) * HEAD_DIM]
        logits = _dot(kk, qt4) + bias_ref[kv]
        logits = jnp.where(keep, logits, NEG_INF)
        sk = sink_ref[kv]
        m = jnp.maximum(jnp.max(logits, axis=0, keepdims=True), sk)
        p = jnp.exp(logits - m)
        den = jnp.sum(p, axis=0, keepdims=True) + jnp.exp(sk - m)
        ot4 = _dot(vv, p.astype(BF16)) * (1.0 / den)
        for g in range(GROUP):
            hd = kv * GROUP + g
            att_ref[hd * HEAD_DIM:(hd + 1) * HEAD_DIM, :] = ot4[:, g * ATT_BLOCK:(g + 1) * ATT_BLOCK]
    ut = (att_ref[...] * gt_ref[...]).astype(BF16)
    o_ref[...] = x_ref[...] + _dot_tn(ut, w_ref[...])


def _swa_prompt(qt, kn, vt, sgt, x, mk, mvt, bias_t, mask_t, sink_t, w_out, batch, n_blk):
    cur = lambda b, i: (b * n_blk + i, 0)
    prev = lambda b, i: (b * n_blk + jnp.maximum(i - 1, 0), 0)
    cur_t = lambda b, i: (0, b * n_blk + i)
    prev_t = lambda b, i: (0, b * n_blk + jnp.maximum(i - 1, 0))
    meta = lambda b, i: (b, 0, 0)
    tk = N_META + 2 * ATT_BLOCK
    nq = GROUP * ATT_BLOCK
    rows = kn.shape[0]
    return pl.pallas_call(
        _swa_prompt_kernel,
        grid=(batch, n_blk),
        in_specs=[pl.BlockSpec((ATT_W, ATT_BLOCK), cur_t),
                  pl.BlockSpec((ATT_BLOCK, KV_W), cur), pl.BlockSpec((ATT_BLOCK, KV_W), prev),
                  pl.BlockSpec((KV_W, ATT_BLOCK), cur_t), pl.BlockSpec((KV_W, ATT_BLOCK), prev_t),
                  pl.BlockSpec((None, N_META, KV_W), meta), pl.BlockSpec((None, KV_W, N_META), meta),
                  pl.BlockSpec((ATT_W, ATT_BLOCK), cur_t), pl.BlockSpec((ATT_BLOCK, D_MODEL), cur),
                  pl.BlockSpec((None, N_KV, tk, nq), lambda b, i: (jnp.minimum(i, 1), 0, 0, 0)),
                  pl.BlockSpec((None, tk, nq), lambda b, i: (jnp.minimum(i, 1), 0, 0)),
                  pl.BlockSpec((N_KV, 1, nq), lambda b, i: (0, 0, 0)),
                  pl.BlockSpec((ATT_W, D_MODEL), lambda b, i: (0, 0))],
        out_specs=pl.BlockSpec((ATT_BLOCK, D_MODEL), cur),
        out_shape=jax.ShapeDtypeStruct((rows, D_MODEL), F32),
        scratch_shapes=[pltpu.VMEM((ATT_W, ATT_BLOCK), F32)],
        compiler_params=_cparams("parallel", "parallel"),
        name="swa_prompt",
    )(qt, kn, kn, vt, vt, mk, mvt, sgt, x, bias_t, mask_t, sink_t, w_out)


def _swa_small_kernel(n_seg, *refs):
    q_ref = refs[0]
    k_refs = refs[1:1 + n_seg]
    v_refs = refs[1 + n_seg:1 + 2 * n_seg]
    bias_ref, mask_ref, sink_ref, o_ref = refs[1 + 2 * n_seg:]
    keep = mask_ref[...] > 0.5
    scale = HEAD_DIM ** -0.5
    kall = jnp.concatenate([r[...] for r in k_refs], axis=0).astype(BF16) if n_seg > 1 else k_refs[0][...].astype(BF16)
    vall = jnp.concatenate([r[...] for r in v_refs], axis=0).astype(BF16) if n_seg > 1 else v_refs[0][...].astype(BF16)
    for kv in range(N_KV):
        kk = kall[:, kv * HEAD_DIM:(kv + 1) * HEAD_DIM]
        vv = vall[:, kv * HEAD_DIM:(kv + 1) * HEAD_DIM]
        logits = _dot_nt(q_ref[kv], kk) * scale + bias_ref[kv]
        logits = jnp.where(keep, logits, NEG_INF)
        sk = sink_ref[kv]
        m = jnp.maximum(jnp.max(logits, axis=-1, keepdims=True), sk)
        p = jnp.exp(logits - m)
        den = jnp.sum(p, axis=-1, keepdims=True) + jnp.exp(sk - m)
        o_ref[kv] = _dot(p.astype(BF16), vv) / den


def _swa_small(q, k_segs, v_segs, seg_len, bias, mask, sink, group, rq):
    rows = q.shape[1]
    n_seg = len(k_segs)
    tk = group * sum(seg_len)
    seg_specs = [pl.BlockSpec((group * n, KV_W), lambda i: (i, 0)) for n in seg_len]
    return pl.pallas_call(
        functools.partial(_swa_small_kernel, n_seg),
        grid=(rows // (group * rq),),
        in_specs=[pl.BlockSpec((N_KV, group * rq, HEAD_DIM), lambda i: (0, i, 0))] + seg_specs + seg_specs
                 + [pl.BlockSpec((N_KV, group * rq, tk), lambda i: (0, 0, 0)),
                    pl.BlockSpec((group * rq, tk), lambda i: (0, 0)),
                    pl.BlockSpec((N_KV, group * rq, 1), lambda i: (0, 0, 0))],
        out_specs=pl.BlockSpec((N_KV, group * rq, HEAD_DIM), lambda i: (0, i, 0)),
        out_shape=jax.ShapeDtypeStruct((N_KV, rows, HEAD_DIM), F32),
        compiler_params=_cparams("parallel"),
        name="swa_small",
    )(q, *k_segs, *v_segs, bias, mask, sink)


def _to_group_rows(a, nb, t):
    a = a.reshape(nb, t, N_KV, GROUP, HEAD_DIM)
    return jnp.transpose(a, (2, 0, 3, 1, 4)).reshape(N_KV, nb * GROUP * t, HEAD_DIM)


def _from_group_rows(a, nb, t):
    a = a.reshape(N_KV, nb, GROUP, t, HEAD_DIM)
    return jnp.transpose(a, (1, 3, 0, 2, 4)).reshape(nb * t, ATT_W)


def _bias_group_rows(b, t):
    return b.reshape(N_KV, GROUP * t, b.shape[-1])


def _block_diag_keys(a, group, seg_len):
    eye = jnp.eye(group, dtype=a.dtype)
    out, o = [], 0
    for n in seg_len:
        blk = a[..., o:o + n]
        o += n
        big = eye[:, None, :, None] * blk[..., None, :, None, :]
        out.append(big.reshape(a.shape[:-2] + (group * a.shape[-2], group * n)))
    return jnp.concatenate(out, axis=-1)


def _conv_silu(xs, cw_ref, cb_ref):
    cw = cw_ref[...]
    acc = xs[0] * cw[0:1]
    for w in range(1, CONV_W):
        acc = acc + xs[w] * cw[w:w + 1]
    return _silu(acc + cb_ref[...])


def _norm_proj_conv_kernel(x_ref, nw_ref, w_ref, cs_ref, cw_ref, cb_ref, xm_ref, xc_ref, halo_ref):
    i = pl.program_id(1)
    h = _rms(x_ref[...], nw_ref[...]).astype(BF16)
    xm = _dot(h, w_ref[...])
    tm = xm.shape[0]
    xm_ref[...] = xm.astype(xm_ref.dtype)
    halo = jnp.where(i == 0, cs_ref[...], halo_ref[...])
    full = jnp.concatenate([halo, xm], axis=0)
    n = tm + 8
    xs = [pltpu.roll(full, n - (8 - (CONV_W - 1) + w), 0)[:tm] for w in range(CONV_W - 1)] + [xm]
    xc_ref[...] = _conv_silu(xs, cw_ref, cb_ref).astype(xc_ref.dtype)
    halo_ref[...] = xm[tm - 8:]


def _norm_proj_conv(x3, nw, w, cs, cw, cb, tm):
    b, t, d = x3.shape
    cur = lambda bi, i: (bi, i, 0)
    fixed = lambda bi, i: (0, 0)
    return pl.pallas_call(
        _norm_proj_conv_kernel,
        grid=(b, t // tm),
        in_specs=[pl.BlockSpec((None, tm, d), cur), pl.BlockSpec((1, d), fixed), pl.BlockSpec((d, M_INNER), fixed),
                  pl.BlockSpec((None, 8, M_INNER), lambda bi, i: (bi, 0, 0)),
                  pl.BlockSpec((CONV_W, M_INNER), fixed), pl.BlockSpec((1, M_INNER), fixed)],
        out_specs=[pl.BlockSpec((None, tm, M_INNER), cur)] * 2,
        out_shape=[jax.ShapeDtypeStruct((b, t, M_INNER), BF16)] * 2,
        scratch_shapes=[pltpu.VMEM((8, M_INNER), F32)],
        compiler_params=_cparams("parallel", "arbitrary"),
        name="norm_proj_conv",
    )(x3, nw, w, cs, cw, cb)


def _qkv_matmuls(xcb, xmb, wq_ref, wk_ref, wv_ref, wg_ref, bg_ref, q_ref, k_ref, v_ref, gt_ref):
    gates = bg_ref[...]
    for h in range(M_HEADS):
        sl = slice(h * M_HD, (h + 1) * M_HD)
        qh = _dot(xcb[:, sl], wq_ref[h]).astype(BF16)
        kh = (_dot(xcb[:, sl], wk_ref[h]) * (M_HD ** -0.5)).astype(BF16)
        vh = _dot(xmb[:, sl], wv_ref[h]).astype(BF16)
        q_ref[:, sl] = qh
        k_ref[:, sl] = kh
        v_ref[:, sl] = vh
        gates = (gates + _dot(qh, wg_ref[h * M_HD:(h + 1) * M_HD])
                 + _dot(kh, wg_ref[M_INNER + h * M_HD:M_INNER + (h + 1) * M_HD])
                 + _dot(vh, wg_ref[2 * M_INNER + h * M_HD:2 * M_INNER + (h + 1) * M_HD]))
    lane = lax.broadcasted_iota(jnp.int32, gates.shape, 1)
    log_f = jnp.minimum(gates, 0.0) - jnp.log1p(jnp.exp(-jnp.abs(gates)))
    gt_ref[...] = jnp.where(lane < M_HEADS, gates, log_f)


def _qkv_kernel(xc_ref, xm_ref, *rest):
    _qkv_matmuls(xc_ref[...], xm_ref[...], *rest)


def _qkv_shift_kernel(x0_ref, x1_ref, x2_ref, x3_ref, cw_ref, cb_ref, *rest):
    xs = [x0_ref[...], x1_ref[...], x2_ref[...], x3_ref[...]]
    xcb = _conv_silu(xs, cw_ref, cb_ref).astype(BF16)
    xc_ref = rest[-1]
    xc_ref[...] = xcb
    _qkv_matmuls(xcb, xs[CONV_W - 1].astype(BF16), *rest[:-1])


def _qkv_weight_specs(fixed2, fixed3):
    return [pl.BlockSpec((M_HEADS, M_HD, M_HD), fixed3), pl.BlockSpec((M_HEADS, M_HD, M_HD), fixed3),
            pl.BlockSpec((M_HEADS, M_HD, M_HD), fixed3),
            pl.BlockSpec((3 * M_INNER, 2 * M_HEADS), fixed2), pl.BlockSpec((1, 2 * M_HEADS), fixed2)]


def _qkv_real(xc3, xm3, wts, tm):
    b, t, _ = xc3.shape
    cur = lambda bi, i: (bi, i, 0)
    outs = [jax.ShapeDtypeStruct((b, t, M_INNER), BF16)] * 3 + [jax.ShapeDtypeStruct((b, t, 2 * M_HEADS), F32)]
    return pl.pallas_call(
        _qkv_kernel,
        grid=(b, t // tm),
        in_specs=[pl.BlockSpec((None, tm, M_INNER), cur)] * 2
                 + _qkv_weight_specs(lambda bi, i: (0, 0), lambda bi, i: (0, 0, 0)),
        out_specs=[pl.BlockSpec((None, tm, M_INNER), cur)] * 3 + [pl.BlockSpec((None, tm, 2 * M_HEADS), cur)],
        out_shape=outs,
        compiler_params=_cparams("parallel", "parallel"),
        name="mlstm_qkv_real",
    )(xc3, xm3, *wts)


def _qkv_small(xs, conv_wts, wts, tm):
    rows = xs[0].shape[0]
    row = lambda i: (i, 0)
    fixed = lambda i: (0, 0)
    outs = ([jax.ShapeDtypeStruct((rows, M_INNER), BF16)] * 3 + [jax.ShapeDtypeStruct((rows, 2 * M_HEADS), F32)]
            + [jax.ShapeDtypeStruct((rows, M_INNER), BF16)])
    return pl.pallas_call(
        _qkv_shift_kernel,
        grid=(rows // tm,),
        in_specs=[pl.BlockSpec((tm, M_INNER), row)] * CONV_W
                 + [pl.BlockSpec((CONV_W, M_INNER), fixed), pl.BlockSpec((1, M_INNER), fixed)]
                 + _qkv_weight_specs(fixed, lambda i: (0, 0, 0)),
        out_specs=[pl.BlockSpec((tm, M_INNER), row)] * 3 + [pl.BlockSpec((tm, 2 * M_HEADS), row)]
                  + [pl.BlockSpec((tm, M_INNER), row)],
        out_shape=outs,
        compiler_params=_cparams("parallel"),
        name="mlstm_qkv_small",
    )(*xs, *conv_wts, *wts)


def _chunk_kernel(seg, hps, aliased, q_ref, k_ref, v_ref, xc_ref, z_ref, op_ref, gc_ref, gr_ref, c0_ref, n0_ref,
                  m0_ref, hn_ref, sk_ref, *rest):
    u_ref, c_ref, n_ref, m_ref = rest[1:] if aliased else rest
    h_blk = pl.program_id(1)
    c_idx = pl.program_id(2)
    L = q_ref.shape[0]
    nseg = L // seg
    hi = lax.Precision.HIGHEST

    @pl.when(c_idx == 0)
    def _():
        c_ref[...] = c0_ref[...]
        n_ref[...] = n0_ref[...]
        m_ref[...] = m0_ref[...]

    gc = gc_ref[...]
    gr = gr_ref[...]
    col_c = lax.broadcasted_iota(jnp.int32, gc.shape, 1)
    row_r = lax.broadcasted_iota(jnp.int32, gr.shape, 0)
    t_i = lax.broadcasted_iota(jnp.int32, (L, L), 0)
    j_i = lax.broadcasted_iota(jnp.int32, (L, L), 1)
    causal = j_i <= t_i
    upper = t_i <= j_i
    if nseg > 1:
        sid_c = lax.broadcasted_iota(jnp.int32, (L, 1), 0) // seg
        in_seg = [sid_c == s for s in range(nseg)]
        pick = lambda vals: sum(jnp.where(in_seg[s], vals[s], 0.0) for s in range(nseg))
        same = (t_i // seg) == (j_i // seg)
        causal = causal & same
        upper = upper & same
    tril = jnp.where(causal, 1.0, 0.0).astype(F32)
    triu = jnp.where(upper, 1.0, 0.0).astype(F32)
    lf_c = jnp.where(col_c >= M_HEADS, gc, 0.0)
    lf_r = jnp.where(row_r >= M_HEADS, gr, 0.0)
    b_all_c = jnp.dot(tril, lf_c, precision=hi, preferred_element_type=F32)
    b_all_r = jnp.dot(lf_r, triu, precision=hi, preferred_element_type=F32)
    if nseg > 1:
        g_all = jnp.dot(jnp.where(same, 1.0, 0.0).astype(F32), lf_c, precision=hi, preferred_element_type=F32)

    for hh in range(hps):
        h_idx = h_blk * hps + hh
        cols = slice(hh * M_HD, (hh + 1) * M_HD)
        sel_c = lambda a, c: jnp.sum(jnp.where(col_c == c, a, 0.0), axis=1, keepdims=True)
        sel_r = lambda a, r: jnp.sum(jnp.where(row_r == r, a, 0.0), axis=0, keepdims=True)
        i_c, i_r = sel_c(gc, h_idx), sel_r(gr, h_idx)
        b_c, b_r = sel_c(b_all_c, M_HEADS + h_idx), sel_r(b_all_r, M_HEADS + h_idx)
        if nseg > 1:
            g = sel_c(g_all, M_HEADS + h_idx)
            m_old = [m_ref[s, hh][:, 0:1] for s in range(nseg)]
            m_prev = pick(m_old)
        else:
            g = b_c[L - 1:L, :]
            m_prev = m_ref[0, hh][:, 0:1]

        log_d = jnp.where(causal, b_c + (i_r - b_r), -jnp.inf)
        inter = b_c + m_prev
        m_t = jnp.maximum(inter, jnp.max(log_d, axis=-1, keepdims=True))
        w_intra = jnp.exp(log_d - m_t)
        w_inter = jnp.exp(inter - m_t)
        q = q_ref[:, cols]
        k = k_ref[:, cols]
        v = v_ref[:, cols]
        s_mat = _dot_nt(q, k) * w_intra
        qf = q.astype(F32)
        if nseg > 1:
            q_c = pick([_dot(q, c_ref[s, hh].astype(BF16)) for s in range(nseg)])
            q_n = pick([jnp.sum(qf * n_ref[s, hh], axis=-1, keepdims=True) for s in range(nseg)])
        else:
            q_c = _dot(q, c_ref[0, hh].astype(BF16))
            q_n = jnp.sum(qf * n_ref[0, hh], axis=-1, keepdims=True)
        num = _dot(s_mat.astype(BF16), v) + w_inter * q_c
        den = jnp.sum(s_mat, axis=-1, keepdims=True) + w_inter * q_n
        h = num * (1.0 / jnp.maximum(jnp.abs(den), jnp.exp(-m_t)))

        lw = (g - b_c) + i_c
        kf = k.astype(F32)
        if nseg > 1:
            g_s = [jnp.max(jnp.where(in_seg[s], g, -jnp.inf), axis=0, keepdims=True) for s in range(nseg)]
            m_new = [jnp.maximum(g_s[s] + m_old[s],
                                 jnp.max(jnp.where(in_seg[s], lw, -jnp.inf), axis=0, keepdims=True))
                     for s in range(nseg)]
            kw = kf * jnp.exp(lw - pick(m_new))
            for s in range(nseg):
                decay = jnp.exp(g_s[s] + m_old[s] - m_new[s])
                kw_s = jnp.where(in_seg[s], kw, 0.0)
                c_ref[s, hh] = decay * c_ref[s, hh] + _dot_tn(kw_s.astype(BF16), v)
                n_ref[s, hh] = decay * n_ref[s, hh] + jnp.sum(kw_s, axis=0, keepdims=True)
                m_ref[s, hh] = jnp.broadcast_to(m_new[s], m_ref.shape[2:])
        else:
            m_new = jnp.maximum(g + m_prev, jnp.max(lw, axis=0, keepdims=True))
            decay = jnp.exp(g + m_prev - m_new)
            kw = kf * jnp.exp(lw - m_new)
            c_ref[0, hh] = decay * c_ref[0, hh] + _dot_tn(kw.astype(BF16), v)
            n_ref[0, hh] = decay * n_ref[0, hh] + jnp.sum(kw, axis=0, keepdims=True)
            m_ref[0, hh] = jnp.broadcast_to(m_new, m_ref.shape[2:])

        mu = jnp.mean(h, axis=-1, keepdims=True)
        hc = h - mu
        var = jnp.mean(hc * hc, axis=-1, keepdims=True)
        h_out = _sigmoid(op_ref[:, cols].astype(F32)) * (hc * lax.rsqrt(var + EPS) * hn_ref[:, cols])
        u = (h_out + sk_ref[:, cols] * xc_ref[:, cols].astype(F32)) * _silu(z_ref[:, cols].astype(F32))
        u_ref[:, cols] = u.astype(u_ref.dtype)


def _mlstm_chunk(q, k, v, xc, zo, gates, c0, n0, m0, lyr_in, head_norm, skip, chunk, seg, hps,
                 c_acc=None, lyr_out=0, n_lyr_out=1):
    nb, t, _ = q.shape
    nc = t // chunk
    nseg = chunk // seg
    assert nseg == 1 or nc == 1
    n_seq = nb * nseg
    n_hblk = M_HEADS // hps
    w = hps * M_HD
    gates_t = jnp.swapaxes(gates, 1, 2)
    tok = lambda bi, h, c: (bi, c, h)
    hd_blk = pl.BlockSpec((None, chunk, w), tok)
    st_in = lambda bi, h, c: (lyr_in, bi, h, 0, 0)
    st_out = lambda bi, h, c: (0, bi, h, 0, 0)
    c_out = lambda bi, h, c: (lyr_out, bi, h, 0, 0)
    head_row = pl.BlockSpec((1, w), lambda bi, h, c: (0, h))
    in_specs = [hd_blk, hd_blk, hd_blk, hd_blk,
                hd_blk,
                pl.BlockSpec((None, chunk, w), lambda bi, h, c: (bi, c, n_hblk + h)),
                pl.BlockSpec((None, chunk, 2 * M_HEADS), lambda bi, h, c: (bi, c, 0)),
                pl.BlockSpec((None, 2 * M_HEADS, chunk), lambda bi, h, c: (bi, 0, c)),
                pl.BlockSpec((None, nseg, hps, M_HD, M_HD), st_in),
                pl.BlockSpec((None, nseg, hps, 1, M_HD), st_in),
                pl.BlockSpec((None, nseg, hps, 1, 128), st_in),
                head_row, head_row]
    args = [q, k, v, xc, zo, zo, gates, gates_t, c0, n0, m0, head_norm, skip]
    aliases = {}
    if c_acc is not None:
        in_specs.append(pl.BlockSpec(memory_space=pl.ANY))
        aliases = {len(args): 1}
        args.append(c_acc)
    return pl.pallas_call(
        functools.partial(_chunk_kernel, seg, hps, c_acc is not None),
        grid=(nb, n_hblk, nc),
        in_specs=in_specs,
        out_specs=[hd_blk,
                   pl.BlockSpec((None, nseg, hps, M_HD, M_HD), c_out),
                   pl.BlockSpec((None, nseg, hps, 1, M_HD), st_out),
                   pl.BlockSpec((None, nseg, hps, 1, 128), st_out)],
        out_shape=[jax.ShapeDtypeStruct((nb, t, M_INNER), BF16),
                   jax.ShapeDtypeStruct((n_lyr_out, n_seq, M_HEADS, M_HD, M_HD), F32),
                   jax.ShapeDtypeStruct((1, n_seq, M_HEADS, 1, M_HD), F32),
                   jax.ShapeDtypeStruct((1, n_seq, M_HEADS, 1, 128), F32)],
        input_output_aliases=aliases,
        compiler_params=_cparams("parallel", "parallel", "arbitrary"),
        name="mlstm_chunk",
    )(*args)


def kernel(x_prompt, x_sample, cache_swa_k, cache_swa_v, cache_swa_meta_k, cache_swa_meta_v, state_mlstm_c, state_mlstm_n, state_mlstm_m, state_mlstm_conv, meta_tokens, rel_bias, norm_w, swa_w_in, swa_q_norm, swa_k_norm, swa_sinks, swa_w_out, mlstm_w_in, mlstm_conv_w, mlstm_conv_b, mlstm_wq, mlstm_wk, mlstm_wv, mlstm_w_gates, mlstm_b_gates, mlstm_head_norm, mlstm_skip, mlstm_w_out):
    B, T, D = x_prompt.shape
    DB, DT, _ = x_sample.shape
    depth = norm_w.shape[0]
    n_buf = cache_swa_k.shape[2]
    n_blk = T // ATT_BLOCK
    n_real, n_samp, n_meta = B * T, DB * DT, B * N_META
    n_small = -(-(n_samp + n_meta) // SMALL_ROW_TILE) * SMALL_ROW_TILE
    pad_small = n_small - n_samp - n_meta
    s_meta = slice(n_samp, n_samp + n_meta)

    def small_rows(samp, meta):
        return jnp.concatenate([samp, meta, jnp.zeros((pad_small,) + samp.shape[1:], samp.dtype)], axis=0)

    xr = x_prompt.reshape(n_real, D)
    xs = small_rows(x_sample.reshape(n_samp, D),
                    jnp.broadcast_to(meta_tokens.astype(x_prompt.dtype)[None], (B, N_META, D)).reshape(n_meta, D))

    ar = lambda n: jnp.arange(n, dtype=jnp.int32)
    meta_pos = ar(N_META)
    geo = []
    for blk in (0, 1):
        q_pos = N_META + blk * ATT_BLOCK + ar(ATT_BLOCK)
        band_pos = N_META + (blk - 1) * ATT_BLOCK + ar(2 * ATT_BLOCK)
        k_pos = jnp.concatenate([meta_pos, band_pos])
        k_valid = jnp.concatenate([jnp.ones((N_META,), bool), band_pos >= N_META])
        geo.append(_mask_and_bucket(q_pos, k_pos, k_valid))
    tk_p = N_META + 2 * ATT_BLOCK
    mask_m, buck_m = _mask_and_bucket(meta_pos, meta_pos, jnp.ones((N_META,), bool))
    buf_pos = PAST_LEN - n_buf + ar(n_buf)
    new_pos = PAST_LEN + ar(DT)
    k_pos_s = jnp.concatenate([meta_pos, buf_pos, new_pos])
    k_valid_s = jnp.concatenate([jnp.ones((N_META,), bool), buf_pos >= N_META, jnp.ones((DT,), bool)])
    mask_s, buck_s = _mask_and_bucket(new_pos, k_pos_s, k_valid_s)
    tk_s = N_META + n_buf + DT

    def place(a, rows):
        return jnp.pad(a, ((0, rows - a.shape[0]), (0, tk_p - a.shape[1])))

    idx_all = jnp.concatenate([geo[0][1], geo[1][1], place(buck_m, N_META), place(buck_s, 8)], axis=0)
    bias_all = _bias_table(idx_all, rel_bias.astype(F32))
    bias_p = jnp.transpose(bias_all[:, :2 * ATT_BLOCK].reshape(N_HEADS, 2, ATT_BLOCK, tk_p), (1, 0, 2, 3))
    bias_p = bias_p.reshape(2, N_KV, GROUP * ATT_BLOCK, tk_p)
    mask_p = jnp.stack([jnp.tile(geo[0][0], (GROUP, 1)), jnp.tile(geo[1][0], (GROUP, 1))]).astype(F32)
    key_major = lambda a: jnp.swapaxes(jnp.concatenate([a[..., N_META:], a[..., :N_META]], axis=-1), -1, -2)
    bias_pt, mask_pt = key_major(bias_p), key_major(mask_p)
    o0 = 2 * ATT_BLOCK
    seg_m, seg_s = (N_META,), (N_META, n_buf, DT)
    grp_s = SAMPLE_ATT_GROUP
    bias_m = _block_diag_keys(_bias_group_rows(bias_all[:, o0:o0 + N_META, :N_META], N_META), B, seg_m)
    mask_mg = _block_diag_keys(jnp.tile(mask_m, (GROUP, 1)).astype(F32), B, seg_m)
    o1 = o0 + N_META
    bias_s = _block_diag_keys(_bias_group_rows(bias_all[:, o1:o1 + DT, :tk_s], DT), grp_s, seg_s)
    mask_sg = _block_diag_keys(jnp.tile(mask_s, (GROUP, 1)).astype(F32), grp_s, seg_s)

    def sink_rows(sinks, t, n_seq=1):
        rows = jnp.repeat(sinks.astype(F32).reshape(N_KV, GROUP, 1), t, axis=2).reshape(N_KV, GROUP * t, 1)
        return jnp.tile(rows, (1, n_seq, 1))

    swa_p, swa_s, ml_p, ml_s = [], [], [], []
    c_samp = None
    assert n_samp % SMALL_CHUNK == 0 and SMALL_CHUNK % DT == 0
    for layer in range(depth):
        j = layer // 2
        nw = norm_w[layer].astype(F32).reshape(1, D)
        if layer % 2 == 0:
            w_in = swa_w_in[j].astype(BF16)
            w_out = swa_w_out[j].astype(BF16)
            qn = swa_q_norm[j].astype(F32).reshape(1, HEAD_DIM)
            kn = swa_k_norm[j].astype(F32).reshape(1, HEAD_DIM)
            qn_scaled = (qn * (HEAD_DIM ** -0.5)).reshape(HEAD_DIM, 1)
            qt_r, k_r, vt_r, gt_r = _swa_proj_t(xr, nw, w_in.T, qn_scaled, kn.reshape(HEAD_DIM, 1), PROJ_T_ROW_TILE)
            q_s, k_s, v_s, g_s = _swa_proj(xs, nw, w_in, qn, kn, SMALL_ROW_TILE)
            mk = k_s[s_meta].reshape(B, N_META, KV_W)
            mv = v_s[s_meta].reshape(B, N_META, KV_W)
            xr = _swa_prompt(qt_r, k_r, vt_r, gt_r, xr, mk, jnp.swapaxes(mv, 1, 2), bias_pt, mask_pt,
                             jnp.swapaxes(sink_rows(swa_sinks[j], ATT_BLOCK), 1, 2), w_out, B, n_blk)
            k_new = k_s[:n_samp]
            v_new = v_s[:n_samp]
            ck = cache_swa_k[j].astype(F32).reshape(DB * n_buf, KV_W)
            cv = cache_swa_v[j].astype(F32).reshape(DB * n_buf, KV_W)
            cmk = cache_swa_meta_k[j].astype(F32).reshape(DB * N_META, KV_W)
            cmv = cache_swa_meta_v[j].astype(F32).reshape(DB * N_META, KV_W)
            o_samp = _swa_small(_to_group_rows(q_s[:n_samp], DB, DT), [cmk, ck, k_new], [cmv, cv, v_new], seg_s,
                                bias_s, mask_sg, sink_rows(swa_sinks[j], DT, grp_s), grp_s, GROUP * DT)
            o_meta = _swa_small(_to_group_rows(q_s[s_meta], B, N_META), [k_s[s_meta]], [v_s[s_meta]], seg_m,
                                bias_m, mask_mg, sink_rows(swa_sinks[j], N_META, B), B, GROUP * N_META)
            o_small = small_rows(_from_group_rows(o_samp, DB, DT), _from_group_rows(o_meta, B, N_META))
            xs = _resid_matmul(o_small, w_out, xs, SMALL_ROW_TILE, gate=g_s)
            n_keep = min(WINDOW, T + N_META)
            last = lambda a: a.reshape(B, T, KV_W)[:, T - n_keep:].reshape(B, n_keep, N_KV, HEAD_DIM)
            v_last = jnp.swapaxes(vt_r.reshape(KV_W, B, T)[:, :, T - n_keep:], 0, 1)
            swa_p.append((last(k_r), jnp.swapaxes(v_last, 1, 2).reshape(B, n_keep, N_KV, HEAD_DIM),
                          mk.reshape(B, N_META, N_KV, HEAD_DIM), mv.reshape(B, N_META, N_KV, HEAD_DIM)))
            win = lambda c, new: jnp.concatenate(
                [c.reshape(DB, n_buf, KV_W), new.reshape(DB, DT, KV_W)], axis=1)[:, DT:].reshape(DB, n_buf, N_KV, HEAD_DIM)
            swa_s.append((win(ck, k_new), win(cv, v_new)))
        else:
            w_in = mlstm_w_in[j].astype(BF16)
            w_out = mlstm_w_out[j].astype(BF16)
            conv_wts = (mlstm_conv_w[j].astype(F32), mlstm_conv_b[j].astype(F32).reshape(1, M_INNER))
            wts = (mlstm_wq[j].astype(BF16), mlstm_wk[j].astype(BF16), mlstm_wv[j].astype(BF16),
                   mlstm_w_gates[j].astype(BF16), mlstm_b_gates[j].astype(F32).reshape(1, 2 * M_HEADS))
            hn = mlstm_head_norm[j].astype(F32).reshape(1, M_INNER)
            sk = mlstm_skip[j].astype(F32).reshape(1, M_INNER)
            p_s = _norm_proj(xs, nw, w_in, SMALL_ROW_TILE, M_INNER)
            xm_samp = p_s[:n_samp, :M_INNER].astype(F32).reshape(DB, DT, M_INNER)
            xm_meta = p_s[s_meta, :M_INNER].astype(F32).reshape(B, N_META, M_INNER)
            xpad_s = jnp.concatenate([state_mlstm_conv[j].astype(F32), xm_samp], axis=1)
            xpad_m = jnp.concatenate([jnp.zeros((B, CONV_W - 1, M_INNER), F32), xm_meta], axis=1)
            shifted = [small_rows(xpad_s[:, w:w + DT].reshape(n_samp, M_INNER),
                                  xpad_m[:, w:w + N_META].reshape(n_meta, M_INNER)) for w in range(CONV_W)]
            q_s, k_s, v_s, gt_s, xc_s = _qkv_small(shifted, conv_wts, wts, SMALL_ROW_TILE)
            zo_s = p_s[:, M_INNER:]
            cs_real = jnp.pad(xpad_m[:, N_META:], ((0, 0), (8 - (CONV_W - 1), 0), (0, 0)))
            xm_r, xc_r = _norm_proj_conv(xr.reshape(B, T, D), nw, w_in[:, :M_INNER], cs_real, *conv_wts, 512)
            zo_r = _norm_proj(xr, nw, w_in[:, M_INNER:], 512, M_INNER).reshape(B, T, 2 * M_INNER)
            q_r, k_r, v_r, gt_r = _qkv_real(xc_r, xm_r, wts, ROW_TILE)

            to3 = lambda a, sl, b, t: a[sl].reshape(b, t, a.shape[-1])
            zc = jnp.zeros((1, B, M_HEADS, M_HD, M_HD), F32)
            zn = jnp.zeros((1, B, M_HEADS, 1, M_HD), F32)
            zm = jnp.zeros((1, B, M_HEADS, 1, 128), F32)
            u_m, c_m, n_m, m_m = _mlstm_chunk(
                to3(q_s, s_meta, B, N_META), to3(k_s, s_meta, B, N_META), to3(v_s, s_meta, B, N_META),
                to3(xc_s, s_meta, B, N_META), to3(zo_s, s_meta, B, N_META), to3(gt_s, s_meta, B, N_META),
                zc, zn, zm, 0, hn, sk, N_META, N_META, M_HEADS)
            u_r, c_r, n_r, m_r = _mlstm_chunk(q_r, k_r, v_r, xc_r, zo_r, gt_r, c_m, n_m, m_m, 0, hn, sk,
                                              PROMPT_CHUNK, PROMPT_CHUNK, PROMPT_HEADS_PER_STEP)
            s_samp = slice(0, n_samp)
            nb_s = n_samp // SMALL_CHUNK
            u_s, c_samp, n_s, m_s = _mlstm_chunk(
                to3(q_s, s_samp, nb_s, SMALL_CHUNK), to3(k_s, s_samp, nb_s, SMALL_CHUNK),
                to3(v_s, s_samp, nb_s, SMALL_CHUNK), to3(xc_s, s_samp, nb_s, SMALL_CHUNK),
                to3(zo_s, s_samp, nb_s, SMALL_CHUNK), to3(gt_s, s_samp, nb_s, SMALL_CHUNK),
                state_mlstm_c.astype(F32), state_mlstm_n.astype(F32)[:, :, :, None, :],
                jnp.broadcast_to(state_mlstm_m.astype(F32)[..., None, None], state_mlstm_m.shape + (1, 128)),
                j, hn, sk, SMALL_CHUNK, DT, 1, c_acc=c_samp, lyr_out=j, n_lyr_out=state_mlstm_c.shape[0])
            xr = _resid_matmul(u_r.reshape(n_real, M_INNER), w_out, xr, ROW_TILE)
            u_small = small_rows(u_s.reshape(n_samp, M_INNER), u_m.reshape(n_meta, M_INNER))
            xs = _resid_matmul(u_small, w_out, xs, SMALL_ROW_TILE)
            ml_p.append((c_r[0], n_r.reshape(B, M_HEADS, M_HD), m_r[0, :, :, 0, 0],
                         xm_r[:, T - (CONV_W - 1):].astype(F32)))
            ml_s.append((n_s.reshape(DB, M_HEADS, M_HD), m_s[0, :, :, 0, 0], xpad_s[:, DT:]))

    sd = state_mlstm_c.dtype
    stack = lambda rows, idx: jnp.stack([r[idx] for r in rows])
    return (xr.reshape(B, T, D), xs[:n_samp].reshape(DB, DT, D),
            stack(swa_p, 0), stack(swa_p, 1), stack(swa_p, 2), stack(swa_p, 3),
            stack(ml_p, 0).astype(sd), stack(ml_p, 1).astype(sd), stack(ml_p, 2).astype(sd), stack(ml_p, 3),
            stack(swa_s, 0), stack(swa_s, 1),
            c_samp.astype(sd), stack(ml_s, 0).astype(sd), stack(ml_s, 1).astype(sd), stack(ml_s, 2))
```

```python
import functools
import math

import jax
import jax.numpy as jnp
from jax import lax
from jax.experimental import pallas as pl
from jax.experimental.pallas import tpu as pltpu

F32 = jnp.float32
BF16 = jnp.bfloat16

D_MODEL = 1024
N_META = 16
EPS = 1e-6
NEG_INF = -1e30
N_HEADS = 16
HEAD_DIM = 64
N_KV = 4
GROUP = 4
ATT_W = N_HEADS * HEAD_DIM
KV_W = N_KV * HEAD_DIM
WINDOW = 128
ATT_BLOCK = 128
N_BUCKETS = 32
MAX_DISTANCE = 128
M_INNER = 2048
M_HEADS = 4
M_HD = 512
CONV_W = 4
PAST_LEN = 8192

VMEM_LIMIT_BYTES = 56 * 1024 * 1024
PROMPT_CHUNK = 256
SMALL_CHUNK = 16
ROW_TILE = 256
PROJ_T_ROW_TILE = 512
SMALL_ROW_TILE = 128
SAMPLE_ATT_GROUP = 8


def _cparams(*sem):
    return pltpu.CompilerParams(dimension_semantics=sem, vmem_limit_bytes=VMEM_LIMIT_BYTES)


def _rms(x, w):
    ms = jnp.mean(x * x, axis=-1, keepdims=True)
    return x * lax.rsqrt(ms + EPS) * w


def _sigmoid(x):
    return 1.0 / (1.0 + jnp.exp(-x))


def _silu(x):
    return x * _sigmoid(x)


def _dot(a, b):
    return jnp.dot(a, b, preferred_element_type=F32)


def _dot_nt(a, b):
    return lax.dot_general(a, b, (((1,), (1,)), ((), ())), preferred_element_type=F32)


def _dot_tn(a, b):
    return lax.dot_general(a, b, (((0,), (0,)), ((), ())), preferred_element_type=F32)


def _norm_proj_kernel(x_ref, nw_ref, w_ref, o_ref):
    h = _rms(x_ref[...], nw_ref[...]).astype(BF16)
    o_ref[...] = _dot(h, w_ref[...]).astype(o_ref.dtype)


def _norm_proj(x, nw, w, tm, tn):
    rows, d = x.shape
    n = w.shape[1]
    return pl.pallas_call(
        _norm_proj_kernel,
        grid=(n // tn, rows // tm),
        in_specs=[pl.BlockSpec((tm, d), lambda j, i: (i, 0)),
                  pl.BlockSpec((1, d), lambda j, i: (0, 0)),
                  pl.BlockSpec((d, tn), lambda j, i: (0, j))],
        out_specs=pl.BlockSpec((tm, tn), lambda j, i: (i, j)),
        out_shape=jax.ShapeDtypeStruct((rows, n), BF16),
        compiler_params=_cparams("parallel", "parallel"),
        name="norm_proj",
    )(x, nw, w)


def _swa_proj_kernel(x_ref, nw_ref, w_ref, qn_ref, kn_ref, q_ref, k_ref, v_ref, g_ref):
    h = _rms(x_ref[...], nw_ref[...]).astype(BF16)
    proj = _dot(h, w_ref[...])
    qn = qn_ref[...]
    kn = kn_ref[...]
    for hd in range(N_HEADS):
        sl = proj[:, hd * HEAD_DIM:(hd + 1) * HEAD_DIM]
        q_ref[:, hd * HEAD_DIM:(hd + 1) * HEAD_DIM] = _rms(sl, qn).astype(q_ref.dtype)
    for hd in range(N_KV):
        sl = proj[:, ATT_W + hd * HEAD_DIM:ATT_W + (hd + 1) * HEAD_DIM]
        k_ref[:, hd * HEAD_DIM:(hd + 1) * HEAD_DIM] = _rms(sl, kn)
    v_ref[...] = proj[:, ATT_W + KV_W:ATT_W + 2 * KV_W]
    g_ref[...] = _silu(proj[:, ATT_W + 2 * KV_W:])


def _swa_proj(x, nw, w, qn, kn, tm):
    rows, d = x.shape
    n = w.shape[1]
    row = lambda i: (i, 0)
    fixed = lambda i: (0, 0)
    return pl.pallas_call(
        _swa_proj_kernel,
        grid=(rows // tm,),
        in_specs=[pl.BlockSpec((tm, d), row), pl.BlockSpec((1, d), fixed), pl.BlockSpec((d, n), fixed),
                  pl.BlockSpec((1, HEAD_DIM), fixed), pl.BlockSpec((1, HEAD_DIM), fixed)],
        out_specs=[pl.BlockSpec((tm, ATT_W), row), pl.BlockSpec((tm, KV_W), row),
                   pl.BlockSpec((tm, KV_W), row), pl.BlockSpec((tm, ATT_W), row)],
        out_shape=[jax.ShapeDtypeStruct((rows, ATT_W), BF16), jax.ShapeDtypeStruct((rows, KV_W), F32),
                   jax.ShapeDtypeStruct((rows, KV_W), F32), jax.ShapeDtypeStruct((rows, ATT_W), F32)],
        compiler_params=_cparams("parallel"),
        name="swa_proj",
    )(x, nw, w, qn, kn)


def _swa_proj_t_kernel(x_ref, nw_ref, wt_ref, qn_ref, kn_ref, qt_ref, k_ref, vt_ref, gt_ref):
    h = _rms(x_ref[...], nw_ref[...]).astype(BF16)
    pt = _dot_nt(wt_ref[...], h)

    def head_norm(r0, gain):
        sl = pt[r0:r0 + HEAD_DIM]
        ms = jnp.mean(sl * sl, axis=0, keepdims=True)
        return sl * lax.rsqrt(ms + EPS) * gain

    qn = qn_ref[...]
    kn = kn_ref[...]
    for hd in range(N_HEADS):
        qt_ref[hd * HEAD_DIM:(hd + 1) * HEAD_DIM, :] = head_norm(hd * HEAD_DIM, qn).astype(qt_ref.dtype)
    kt = jnp.concatenate([head_norm(ATT_W + hd * HEAD_DIM, kn) for hd in range(N_KV)], axis=0)
    k_ref[...] = kt.T
    vt_ref[...] = pt[ATT_W + KV_W:ATT_W + 2 * KV_W]
    gt_ref[...] = _silu(pt[ATT_W + 2 * KV_W:])


def _swa_proj_t(x, nw, wt, qn_col, kn_col, tm):
    rows, d = x.shape
    nt = wt.shape[0]
    row = lambda i: (i, 0)
    col = lambda i: (0, i)
    fixed = lambda i: (0, 0)
    return pl.pallas_call(
        _swa_proj_t_kernel,
        grid=(rows // tm,),
        in_specs=[pl.BlockSpec((tm, d), row), pl.BlockSpec((1, d), fixed), pl.BlockSpec((nt, d), fixed),
                  pl.BlockSpec((HEAD_DIM, 1), fixed), pl.BlockSpec((HEAD_DIM, 1), fixed)],
        out_specs=[pl.BlockSpec((ATT_W, tm), col), pl.BlockSpec((tm, KV_W), row),
                   pl.BlockSpec((KV_W, tm), col), pl.BlockSpec((ATT_W, tm), col)],
        out_shape=[jax.ShapeDtypeStruct((ATT_W, rows), BF16), jax.ShapeDtypeStruct((rows, KV_W), F32),
                   jax.ShapeDtypeStruct((KV_W, rows), F32), jax.ShapeDtypeStruct((ATT_W, rows), F32)],
        compiler_params=_cparams("parallel"),
        name="swa_proj_t",
    )(x, nw, wt, qn_col, kn_col)


def _resid_matmul_kernel(u_ref, w_ref, x_ref, o_ref):
    o_ref[...] = x_ref[...] + _dot(u_ref[...], w_ref[...])


def _gated_resid_matmul_kernel(a_ref, g_ref, w_ref, x_ref, o_ref):
    u = (a_ref[...] * g_ref[...]).astype(BF16)
    o_ref[...] = x_ref[...] + _dot(u, w_ref[...])


def _resid_matmul(u, w, x, tm, gate=None):
    rows, k = u.shape
    d = w.shape[1]
    row = lambda i: (i, 0)
    ins = [u] if gate is None else [u, gate]
    body = _resid_matmul_kernel if gate is None else _gated_resid_matmul_kernel
    return pl.pallas_call(
        body,
        grid=(rows // tm,),
        in_specs=[pl.BlockSpec((tm, k), row)] * len(ins)
                 + [pl.BlockSpec((k, d), lambda i: (0, 0)), pl.BlockSpec((tm, d), row)],
        out_specs=pl.BlockSpec((tm, d), row),
        out_shape=jax.ShapeDtypeStruct((rows, d), F32),
        compiler_params=_cparams("parallel"),
        name="resid_matmul",
    )(*ins, w, x)


def _bias_table_kernel(idx_ref, tab_ref, o_ref):
    idx = idx_ref[...]
    for hd in range(N_HEADS):
        acc = jnp.zeros(idx.shape, F32)
        for b in range(N_BUCKETS):
            acc = jnp.where(idx == b, tab_ref[b, hd], acc)
        o_ref[hd] = acc


def _bias_table(idx, rel_bias):
    r, c = idx.shape
    return pl.pallas_call(
        _bias_table_kernel,
        in_specs=[pl.BlockSpec((r, c), lambda: (0, 0)), pl.BlockSpec(memory_space=pltpu.SMEM)],
        out_specs=pl.BlockSpec((N_HEADS, r, c), lambda: (0, 0, 0)),
        out_shape=jax.ShapeDtypeStruct((N_HEADS, r, c), F32),
        name="bias_table",
    )(idx, rel_bias)


def _rel_bucket(dist):
    n = jnp.maximum(dist, 0)
    max_exact = N_BUCKETS // 2
    nf = jnp.maximum(n, 1).astype(F32)
    large = max_exact + (jnp.log(nf / max_exact) / math.log(MAX_DISTANCE / max_exact)
                         * (N_BUCKETS - max_exact)).astype(jnp.int32)
    large = jnp.minimum(large, N_BUCKETS - 1)
    return jnp.where(n < max_exact, n, large)


def _mask_and_bucket(q_pos, k_pos, k_valid):
    dist = q_pos[:, None] - k_pos[None, :]
    mask = k_valid[None, :] & (dist >= 0) & ((dist <= WINDOW) | (k_pos[None, :] < N_META))
    return mask, _rel_bucket(dist)


def _swa_prompt_kernel(qt_ref, kc_ref, kp_ref, vtc_ref, vtp_ref, mk_ref, mvt_ref, gt_ref, x_ref,
                       bias_ref, mask_ref, sink_ref, w_ref, o_ref, att_ref):
    kband = jnp.concatenate([kp_ref[...], kc_ref[...], mk_ref[...]], axis=0).astype(BF16)
    vtband = jnp.concatenate([vtp_ref[...], vtc_ref[...], mvt_ref[...]], axis=1).astype(BF16)
    qt = qt_ref[...]
    keep = mask_ref[...] > 0.5
    raw = []
    for kv in range(N_KV):
        qt4 = jnp.concatenate(
            [qt[(kv * GROUP + g) * HEAD_DIM:(kv * GROUP + g + 1) * HEAD_DIM] for g in range(GROUP)], axis=1)
        raw.append(_dot(kband[:, kv * HEAD_DIM:(kv + 1) * HEAD_DIM], qt4))
    for kv in range(N_KV):
        vv = vtband[kv * HEAD_DIM:(kv + 1) * HEAD_DIM]
        logits = jnp.where(keep, raw[kv] + bias_ref[kv], NEG_INF)
        sk = sink_ref[kv]
        m = jnp.maximum(jnp.max(logits, axis=0, keepdims=True), sk)
        p = jnp.exp(logits - m)
        den = jnp.sum(p, axis=0, keepdims=True) + jnp.exp(sk - m)
        ot4 = _dot(vv, p.astype(BF16)) * (1.0 / den)
        for g in range(GROUP):
            hd = kv * GROUP + g
            att_ref[hd * HEAD_DIM:(hd + 1) * HEAD_DIM, :] = ot4[:, g * ATT_BLOCK:(g + 1) * ATT_BLOCK]
    ut = (att_ref[...] * gt_ref[...]).astype(BF16)
    o_ref[...] = x_ref[...] + _dot_tn(ut, w_ref[...])


def _swa_prompt(qt, kn, vt, sgt, x, mk, mvt, bias_t, mask_t, sink_t, w_out, batch, n_blk):
    cur = lambda b, i: (b * n_blk + i, 0)
    prev = lambda b, i: (b * n_blk + jnp.maximum(i - 1, 0), 0)
    cur_t = lambda b, i: (0, b * n_blk + i)
    prev_t = lambda b, i: (0, b * n_blk + jnp.maximum(i - 1, 0))
    meta = lambda b, i: (b, 0, 0)
    tk = N_META + 2 * ATT_BLOCK
    nq = GROUP * ATT_BLOCK
    rows = kn.shape[0]
    return pl.pallas_call(
        _swa_prompt_kernel,
        grid=(batch, n_blk),
        in_specs=[pl.BlockSpec((ATT_W, ATT_BLOCK), cur_t),
                  pl.BlockSpec((ATT_BLOCK, KV_W), cur), pl.BlockSpec((ATT_BLOCK, KV_W), prev),
                  pl.BlockSpec((KV_W, ATT_BLOCK), cur_t), pl.BlockSpec((KV_W, ATT_BLOCK), prev_t),
                  pl.BlockSpec((None, N_META, KV_W), meta), pl.BlockSpec((None, KV_W, N_META), meta),
                  pl.BlockSpec((ATT_W, ATT_BLOCK), cur_t), pl.BlockSpec((ATT_BLOCK, D_MODEL), cur),
                  pl.BlockSpec((None, N_KV, tk, nq), lambda b, i: (jnp.minimum(i, 1), 0, 0, 0)),
                  pl.BlockSpec((None, tk, nq), lambda b, i: (jnp.minimum(i, 1), 0, 0)),
                  pl.BlockSpec((N_KV, 1, nq), lambda b, i: (0, 0, 0)),
                  pl.BlockSpec((ATT_W, D_MODEL), lambda b, i: (0, 0))],
        out_specs=pl.BlockSpec((ATT_BLOCK, D_MODEL), cur),
        out_shape=jax.ShapeDtypeStruct((rows, D_MODEL), F32),
        scratch_shapes=[pltpu.VMEM((ATT_W, ATT_BLOCK), F32)],
        compiler_params=_cparams("parallel", "parallel"),
        name="swa_prompt",
    )(qt, kn, kn, vt, vt, mk, mvt, sgt, x, bias_t, mask_t, sink_t, w_out)


def _swa_small_kernel(n_seg, *refs):
    q_ref = refs[0]
    k_refs = refs[1:1 + n_seg]
    v_refs = refs[1 + n_seg:1 + 2 * n_seg]
    bias_ref, mask_ref, sink_ref, o_ref = refs[1 + 2 * n_seg:]
    keep = mask_ref[...] > 0.5
    scale = HEAD_DIM ** -0.5
    kall = jnp.concatenate([r[...] for r in k_refs], axis=0).astype(BF16) if n_seg > 1 else k_refs[0][...].astype(BF16)
    vall = jnp.concatenate([r[...] for r in v_refs], axis=0).astype(BF16) if n_seg > 1 else v_refs[0][...].astype(BF16)
    for kv in range(N_KV):
        kk = kall[:, kv * HEAD_DIM:(kv + 1) * HEAD_DIM]
        vv = vall[:, kv * HEAD_DIM:(kv + 1) * HEAD_DIM]
        logits = _dot_nt(q_ref[kv], kk) * scale + bias_ref[kv]
        logits = jnp.where(keep, logits, NEG_INF)
        sk = sink_ref[kv]
        m = jnp.maximum(jnp.max(logits, axis=-1, keepdims=True), sk)
        p = jnp.exp(logits - m)
        den = jnp.sum(p, axis=-1, keepdims=True) + jnp.exp(sk - m)
        o_ref[kv] = _dot(p.astype(BF16), vv) / den


def _swa_small(q, k_segs, v_segs, seg_len, bias, mask, sink, group, rq):
    rows = q.shape[1]
    n_seg = len(k_segs)
    tk = group * sum(seg_len)
    seg_specs = [pl.BlockSpec((group * n, KV_W), lambda i: (i, 0)) for n in seg_len]
    return pl.pallas_call(
        functools.partial(_swa_small_kernel, n_seg),
        grid=(rows // (group * rq),),
        in_specs=[pl.BlockSpec((N_KV, group * rq, HEAD_DIM), lambda i: (0, i, 0))] + seg_specs + seg_specs
                 + [pl.BlockSpec((N_KV, group * rq, tk), lambda i: (0, 0, 0)),
                    pl.BlockSpec((group * rq, tk), lambda i: (0, 0)),
                    pl.BlockSpec((N_KV, group * rq, 1), lambda i: (0, 0, 0))],
        out_specs=pl.BlockSpec((N_KV, group * rq, HEAD_DIM), lambda i: (0, i, 0)),
        out_shape=jax.ShapeDtypeStruct((N_KV, rows, HEAD_DIM), F32),
        compiler_params=_cparams("parallel"),
        name="swa_small",
    )(q, *k_segs, *v_segs, bias, mask, sink)


def _to_group_rows(a, nb, t):
    a = a.reshape(nb, t, N_KV, GROUP, HEAD_DIM)
    return jnp.transpose(a, (2, 0, 3, 1, 4)).reshape(N_KV, nb * GROUP * t, HEAD_DIM)


def _from_group_rows(a, nb, t):
    a = a.reshape(N_KV, nb, GROUP, t, HEAD_DIM)
    return jnp.transpose(a, (1, 3, 0, 2, 4)).reshape(nb * t, ATT_W)


def _bias_group_rows(b, t):
    return b.reshape(N_KV, GROUP * t, b.shape[-1])


def _block_diag_keys(a, group, seg_len):
    eye = jnp.eye(group, dtype=a.dtype)
    out, o = [], 0
    for n in seg_len:
        blk = a[..., o:o + n]
        o += n
        big = eye[:, None, :, None] * blk[..., None, :, None, :]
        out.append(big.reshape(a.shape[:-2] + (group * a.shape[-2], group * n)))
    return jnp.concatenate(out, axis=-1)


def _conv_silu(xs, cw_ref, cb_ref):
    cw = cw_ref[...]
    acc = xs[0] * cw[0:1]
    for w in range(1, CONV_W):
        acc = acc + xs[w] * cw[w:w + 1]
    return _silu(acc + cb_ref[...])


def _norm_proj_conv_kernel(x_ref, nw_ref, w_ref, cs_ref, cw_ref, cb_ref, xm_ref, xc_ref, halo_ref):
    i = pl.program_id(1)
    h = _rms(x_ref[...], nw_ref[...]).astype(BF16)
    xm = _dot(h, w_ref[...])
    tm = xm.shape[0]
    xm_ref[...] = xm.astype(xm_ref.dtype)
    halo = jnp.where(i == 0, cs_ref[...], halo_ref[...])
    full = jnp.concatenate([halo, xm], axis=0)
    n = tm + 8
    xs = [pltpu.roll(full, n - (8 - (CONV_W - 1) + w), 0)[:tm] for w in range(CONV_W - 1)] + [xm]
    xc_ref[...] = _conv_silu(xs, cw_ref, cb_ref).astype(xc_ref.dtype)
    halo_ref[...] = xm[tm - 8:]


def _norm_proj_conv(x3, nw, w, cs, cw, cb, tm):
    b, t, d = x3.shape
    cur = lambda bi, i: (bi, i, 0)
    fixed = lambda bi, i: (0, 0)
    return pl.pallas_call(
        _norm_proj_conv_kernel,
        grid=(b, t // tm),
        in_specs=[pl.BlockSpec((None, tm, d), cur), pl.BlockSpec((1, d), fixed), pl.BlockSpec((d, M_INNER), fixed),
                  pl.BlockSpec((None, 8, M_INNER), lambda bi, i: (bi, 0, 0)),
                  pl.BlockSpec((CONV_W, M_INNER), fixed), pl.BlockSpec((1, M_INNER), fixed)],
        out_specs=[pl.BlockSpec((None, tm, M_INNER), cur)] * 2,
        out_shape=[jax.ShapeDtypeStruct((b, t, M_INNER), BF16)] * 2,
        scratch_shapes=[pltpu.VMEM((8, M_INNER), F32)],
        compiler_params=_cparams("parallel", "arbitrary"),
        name="norm_proj_conv",
    )(x3, nw, w, cs, cw, cb)


def _qkv_matmuls(xcb, xmb, wq_ref, wk_ref, wv_ref, wg_ref, bg_ref, q_ref, k_ref, v_ref, gt_ref):
    gates = bg_ref[...]
    for h in range(M_HEADS):
        sl = slice(h * M_HD, (h + 1) * M_HD)
        qh = _dot(xcb[:, sl], wq_ref[h]).astype(BF16)
        kh = (_dot(xcb[:, sl], wk_ref[h]) * (M_HD ** -0.5)).astype(BF16)
        vh = _dot(xmb[:, sl], wv_ref[h]).astype(BF16)
        q_ref[:, sl] = qh
        k_ref[:, sl] = kh
        v_ref[:, sl] = vh
        gates = (gates + _dot(qh, wg_ref[h * M_HD:(h + 1) * M_HD])
                 + _dot(kh, wg_ref[M_INNER + h * M_HD:M_INNER + (h + 1) * M_HD])
                 + _dot(vh, wg_ref[2 * M_INNER + h * M_HD:2 * M_INNER + (h + 1) * M_HD]))
    lane = lax.broadcasted_iota(jnp.int32, gates.shape, 1)
    log_f = jnp.minimum(gates, 0.0) - jnp.log1p(jnp.exp(-jnp.abs(gates)))
    gt_ref[...] = jnp.where(lane < M_HEADS, gates, log_f)


def _qkv_kernel(xc_ref, xm_ref, *rest):
    _qkv_matmuls(xc_ref[...], xm_ref[...], *rest)


def _qkv_shift_kernel(x0_ref, x1_ref, x2_ref, x3_ref, cw_ref, cb_ref, *rest):
    xs = [x0_ref[...], x1_ref[...], x2_ref[...], x3_ref[...]]
    xcb = _conv_silu(xs, cw_ref, cb_ref).astype(BF16)
    xc_ref = rest[-1]
    xc_ref[...] = xcb
    _qkv_matmuls(xcb, xs[CONV_W - 1].astype(BF16), *rest[:-1])


def _qkv_weight_specs(fixed2, fixed3):
    return [pl.BlockSpec((M_HEADS, M_HD, M_HD), fixed3), pl.BlockSpec((M_HEADS, M_HD, M_HD), fixed3),
            pl.BlockSpec((M_HEADS, M_HD, M_HD), fixed3),
            pl.BlockSpec((3 * M_INNER, 2 * M_HEADS), fixed2), pl.BlockSpec((1, 2 * M_HEADS), fixed2)]


def _qkv_real(xc3, xm3, wts, tm):
    b, t, _ = xc3.shape
    cur = lambda bi, i: (bi, i, 0)
    outs = [jax.ShapeDtypeStruct((b, t, M_INNER), BF16)] * 3 + [jax.ShapeDtypeStruct((b, t, 2 * M_HEADS), F32)]
    return pl.pallas_call(
        _qkv_kernel,
        grid=(b, t // tm),
        in_specs=[pl.BlockSpec((None, tm, M_INNER), cur)] * 2
                 + _qkv_weight_specs(lambda bi, i: (0, 0), lambda bi, i: (0, 0, 0)),
        out_specs=[pl.BlockSpec((None, tm, M_INNER), cur)] * 3 + [pl.BlockSpec((None, tm, 2 * M_HEADS), cur)],
        out_shape=outs,
        compiler_params=_cparams("parallel", "parallel"),
        name="mlstm_qkv_real",
    )(xc3, xm3, *wts)


def _qkv_small(xs, conv_wts, wts, tm):
    rows = xs[0].shape[0]
    row = lambda i: (i, 0)
    fixed = lambda i: (0, 0)
    outs = ([jax.ShapeDtypeStruct((rows, M_INNER), BF16)] * 3 + [jax.ShapeDtypeStruct((rows, 2 * M_HEADS), F32)]
            + [jax.ShapeDtypeStruct((rows, M_INNER), BF16)])
    return pl.pallas_call(
        _qkv_shift_kernel,
        grid=(rows // tm,),
        in_specs=[pl.BlockSpec((tm, M_INNER), row)] * CONV_W
                 + [pl.BlockSpec((CONV_W, M_INNER), fixed), pl.BlockSpec((1, M_INNER), fixed)]
                 + _qkv_weight_specs(fixed, lambda i: (0, 0, 0)),
        out_specs=[pl.BlockSpec((tm, M_INNER), row)] * 3 + [pl.BlockSpec((tm, 2 * M_HEADS), row)]
                  + [pl.BlockSpec((tm, M_INNER), row)],
        out_shape=outs,
        compiler_params=_cparams("parallel"),
        name="mlstm_qkv_small",
    )(*xs, *conv_wts, *wts)


def _chunk_kernel(seg, hps, aliased, fused, q_ref, k_ref, v_ref, xc_ref, z_ref, op_ref, gc_ref, gr_ref, c0_ref, n0_ref,
                  m0_ref, hn_ref, sk_ref, *rest):
    if fused:
        x_ref, w_ref = rest[:2]
        rest = rest[2:]
    if aliased:
        rest = rest[1:]
    out_ref, c_ref, n_ref, m_ref = rest[:4]
    u_ref = rest[4] if fused else out_ref
    h_blk = pl.program_id(1)
    c_idx = pl.program_id(2)
    L = q_ref.shape[0]
    nseg = L // seg
    hi = lax.Precision.HIGHEST

    @pl.when(c_idx == 0)
    def _():
        c_ref[...] = c0_ref[...]
        n_ref[...] = n0_ref[...]
        m_ref[...] = m0_ref[...]

    gc = gc_ref[...]
    gr = gr_ref[...]
    col_c = lax.broadcasted_iota(jnp.int32, gc.shape, 1)
    row_r = lax.broadcasted_iota(jnp.int32, gr.shape, 0)
    t_i = lax.broadcasted_iota(jnp.int32, (L, L), 0)
    j_i = lax.broadcasted_iota(jnp.int32, (L, L), 1)
    causal = j_i <= t_i
    upper = t_i <= j_i
    if nseg > 1:
        sid_c = lax.broadcasted_iota(jnp.int32, (L, 1), 0) // seg
        in_seg = [sid_c == s for s in range(nseg)]
        pick = lambda vals: sum(jnp.where(in_seg[s], vals[s], 0.0) for s in range(nseg))
        same = (t_i // seg) == (j_i // seg)
        causal = causal & same
        upper = upper & same
    tril = jnp.where(causal, 1.0, 0.0).astype(F32)
    triu = jnp.where(upper, 1.0, 0.0).astype(F32)
    lf_c = jnp.where(col_c >= M_HEADS, gc, 0.0)
    lf_r = jnp.where(row_r >= M_HEADS, gr, 0.0)
    b_all_c = jnp.dot(tril, lf_c, precision=hi, preferred_element_type=F32)
    b_all_r = jnp.dot(lf_r, triu, precision=hi, preferred_element_type=F32)
    if nseg > 1:
        g_all = jnp.dot(jnp.where(same, 1.0, 0.0).astype(F32), lf_c, precision=hi, preferred_element_type=F32)

    for hh in range(hps):
        h_idx = h_blk * hps + hh
        cols = slice(hh * M_HD, (hh + 1) * M_HD)
        sel_c = lambda a, c: jnp.sum(jnp.where(col_c == c, a, 0.0), axis=1, keepdims=True)
        sel_r = lambda a, r: jnp.sum(jnp.where(row_r == r, a, 0.0), axis=0, keepdims=True)
        i_c, i_r = sel_c(gc, h_idx), sel_r(gr, h_idx)
        b_c, b_r = sel_c(b_all_c, M_HEADS + h_idx), sel_r(b_all_r, M_HEADS + h_idx)
        if nseg > 1:
            g = sel_c(g_all, M_HEADS + h_idx)
            m_old = [m_ref[s, hh][:, 0:1] for s in range(nseg)]
            m_prev = pick(m_old)
        else:
            g = b_c[L - 1:L, :]
            m_prev = m_ref[0, hh][:, 0:1]

        log_d = jnp.where(causal, b_c + (i_r - b_r), -jnp.inf)
        inter = b_c + m_prev
        m_t = jnp.maximum(inter, jnp.max(log_d, axis=-1, keepdims=True))
        w_intra = jnp.exp(log_d - m_t)
        w_inter = jnp.exp(inter - m_t)
        q = q_ref[:, cols]
        k = k_ref[:, cols]
        v = v_ref[:, cols]
        s_mat = _dot_nt(q, k) * w_intra
        qf = q.astype(F32)
        if nseg > 1:
            q_c = pick([_dot(q, c_ref[s, hh].astype(BF16)) for s in range(nseg)])
            q_n = pick([jnp.sum(qf * n_ref[s, hh], axis=-1, keepdims=True) for s in range(nseg)])
        else:
            q_c = _dot(q, c_ref[0, hh].astype(BF16))
            q_n = jnp.sum(qf * n_ref[0, hh], axis=-1, keepdims=True)
        num = _dot(s_mat.astype(BF16), v) + w_inter * q_c
        den = jnp.sum(s_mat, axis=-1, keepdims=True) + w_inter * q_n
        h = num * (1.0 / jnp.maximum(jnp.abs(den), jnp.exp(-m_t)))

        lw = (g - b_c) + i_c
        kf = k.astype(F32)
        if nseg > 1:
            g_s = [jnp.max(jnp.where(in_seg[s], g, -jnp.inf), axis=0, keepdims=True) for s in range(nseg)]
            m_new = [jnp.maximum(g_s[s] + m_old[s],
                                 jnp.max(jnp.where(in_seg[s], lw, -jnp.inf), axis=0, keepdims=True))
                     for s in range(nseg)]
            kw = kf * jnp.exp(lw - pick(m_new))
            for s in range(nseg):
                decay = jnp.exp(g_s[s] + m_old[s] - m_new[s])
                kw_s = jnp.where(in_seg[s], kw, 0.0)
                c_ref[s, hh] = decay * c_ref[s, hh] + _dot_tn(kw_s.astype(BF16), v)
                n_ref[s, hh] = decay * n_ref[s, hh] + jnp.sum(kw_s, axis=0, keepdims=True)
                m_ref[s, hh] = jnp.broadcast_to(m_new[s], m_ref.shape[2:])
        else:
            m_new = jnp.maximum(g + m_prev, jnp.max(lw, axis=0, keepdims=True))
            decay = jnp.exp(g + m_prev - m_new)
            kw = kf * jnp.exp(lw - m_new)
            c_ref[0, hh] = decay * c_ref[0, hh] + _dot_tn(kw.astype(BF16), v)
            n_ref[0, hh] = decay * n_ref[0, hh] + jnp.sum(kw, axis=0, keepdims=True)
            m_ref[0, hh] = jnp.broadcast_to(m_new, m_ref.shape[2:])

        mu = jnp.mean(h, axis=-1, keepdims=True)
        hc = h - mu
        var = jnp.mean(hc * hc, axis=-1, keepdims=True)
        h_out = _sigmoid(op_ref[:, cols].astype(F32)) * (hc * lax.rsqrt(var + EPS) * hn_ref[:, cols])
        u = (h_out + sk_ref[:, cols] * xc_ref[:, cols].astype(F32)) * _silu(z_ref[:, cols].astype(F32))
        u_ref[:, cols] = u.astype(u_ref.dtype)
    if fused:
        out_ref[...] = x_ref[...] + _dot(u_ref[...], w_ref[...])


def _mlstm_chunk(q, k, v, xc, zo, gates, c0, n0, m0, lyr_in, head_norm, skip, chunk, seg, hps,
                 c_acc=None, lyr_out=0, n_lyr_out=1, x=None, w_out=None):
    nb, t, _ = q.shape
    nc = t // chunk
    nseg = chunk // seg
    assert nseg == 1 or nc == 1
    n_seq = nb * nseg
    n_hblk = M_HEADS // hps
    w = hps * M_HD
    gates_t = jnp.swapaxes(gates, 1, 2)
    tok = lambda bi, h, c: (bi, c, h)
    hd_blk = pl.BlockSpec((None, chunk, w), tok)
    st_in = lambda bi, h, c: (lyr_in, bi, h, 0, 0)
    st_out = lambda bi, h, c: (0, bi, h, 0, 0)
    c_out = lambda bi, h, c: (lyr_out, bi, h, 0, 0)
    head_row = pl.BlockSpec((1, w), lambda bi, h, c: (0, h))
    in_specs = [hd_blk, hd_blk, hd_blk, hd_blk,
                hd_blk,
                pl.BlockSpec((None, chunk, w), lambda bi, h, c: (bi, c, n_hblk + h)),
                pl.BlockSpec((None, chunk, 2 * M_HEADS), lambda bi, h, c: (bi, c, 0)),
                pl.BlockSpec((None, 2 * M_HEADS, chunk), lambda bi, h, c: (bi, 0, c)),
                pl.BlockSpec((None, nseg, hps, M_HD, M_HD), st_in),
                pl.BlockSpec((None, nseg, hps, 1, M_HD), st_in),
                pl.BlockSpec((None, nseg, hps, 1, 128), st_in),
                head_row, head_row]
    args = [q, k, v, xc, zo, zo, gates, gates_t, c0, n0, m0, head_norm, skip]
    fused = x is not None
    first_out = (hd_blk, jax.ShapeDtypeStruct((nb, t, M_INNER), BF16))
    scratch = []
    if fused:
        assert hps == M_HEADS
        d = w_out.shape[1]
        x_blk = pl.BlockSpec((None, chunk, d), lambda bi, h, c: (bi, c, 0))
        in_specs += [x_blk, pl.BlockSpec((M_INNER, d), lambda bi, h, c: (0, 0))]
        args += [x, w_out]
        first_out = (x_blk, jax.ShapeDtypeStruct((nb, t, d), F32))
        scratch = [pltpu.VMEM((chunk, M_INNER), BF16)]
    aliases = {}
    if c_acc is not None:
        in_specs.append(pl.BlockSpec(memory_space=pl.ANY))
        aliases = {len(args): 1}
        args.append(c_acc)
    return pl.pallas_call(
        functools.partial(_chunk_kernel, seg, hps, c_acc is not None, fused),
        grid=(nb, n_hblk, nc),
        in_specs=in_specs,
        out_specs=[first_out[0],
                   pl.BlockSpec((None, nseg, hps, M_HD, M_HD), c_out),
                   pl.BlockSpec((None, nseg, hps, 1, M_HD), st_out),
                   pl.BlockSpec((None, nseg, hps, 1, 128), st_out)],
        out_shape=[first_out[1],
                   jax.ShapeDtypeStruct((n_lyr_out, n_seq, M_HEADS, M_HD, M_HD), F32),
                   jax.ShapeDtypeStruct((1, n_seq, M_HEADS, 1, M_HD), F32),
                   jax.ShapeDtypeStruct((1, n_seq, M_HEADS, 1, 128), F32)],
        scratch_shapes=scratch,
        input_output_aliases=aliases,
        compiler_params=_cparams("parallel", "parallel", "arbitrary"),
        name="mlstm_chunk",
    )(*args)


def kernel(x_prompt, x_sample, cache_swa_k, cache_swa_v, cache_swa_meta_k, cache_swa_meta_v, state_mlstm_c, state_mlstm_n, state_mlstm_m, state_mlstm_conv, meta_tokens, rel_bias, norm_w, swa_w_in, swa_q_norm, swa_k_norm, swa_sinks, swa_w_out, mlstm_w_in, mlstm_conv_w, mlstm_conv_b, mlstm_wq, mlstm_wk, mlstm_wv, mlstm_w_gates, mlstm_b_gates, mlstm_head_norm, mlstm_skip, mlstm_w_out):
    B, T, D = x_prompt.shape
    DB, DT, _ = x_sample.shape
    depth = norm_w.shape[0]
    n_buf = cache_swa_k.shape[2]
    n_blk = T // ATT_BLOCK
    n_real, n_samp, n_meta = B * T, DB * DT, B * N_META
    n_small = -(-(n_samp + n_meta) // SMALL_ROW_TILE) * SMALL_ROW_TILE
    pad_small = n_small - n_samp - n_meta
    s_meta = slice(n_samp, n_samp + n_meta)

    def small_rows(samp, meta):
        return jnp.concatenate([samp, meta, jnp.zeros((pad_small,) + samp.shape[1:], samp.dtype)], axis=0)

    xr = x_prompt.reshape(n_real, D)
    xs = small_rows(x_sample.reshape(n_samp, D),
                    jnp.broadcast_to(meta_tokens.astype(x_prompt.dtype)[None], (B, N_META, D)).reshape(n_meta, D))

    ar = lambda n: jnp.arange(n, dtype=jnp.int32)
    meta_pos = ar(N_META)
    geo = []
    for blk in (0, 1):
        q_pos = N_META + blk * ATT_BLOCK + ar(ATT_BLOCK)
        band_pos = N_META + (blk - 1) * ATT_BLOCK + ar(2 * ATT_BLOCK)
        k_pos = jnp.concatenate([meta_pos, band_pos])
        k_valid = jnp.concatenate([jnp.ones((N_META,), bool), band_pos >= N_META])
        geo.append(_mask_and_bucket(q_pos, k_pos, k_valid))
    tk_p = N_META + 2 * ATT_BLOCK
    mask_m, buck_m = _mask_and_bucket(meta_pos, meta_pos, jnp.ones((N_META,), bool))
    buf_pos = PAST_LEN - n_buf + ar(n_buf)
    new_pos = PAST_LEN + ar(DT)
    k_pos_s = jnp.concatenate([meta_pos, buf_pos, new_pos])
    k_valid_s = jnp.concatenate([jnp.ones((N_META,), bool), buf_pos >= N_META, jnp.ones((DT,), bool)])
    mask_s, buck_s = _mask_and_bucket(new_pos, k_pos_s, k_valid_s)
    tk_s = N_META + n_buf + DT

    def place(a, rows):
        return jnp.pad(a, ((0, rows - a.shape[0]), (0, tk_p - a.shape[1])))

    idx_all = jnp.concatenate([geo[0][1], geo[1][1], place(buck_m, N_META), place(buck_s, 8)], axis=0)
    bias_all = _bias_table(idx_all, rel_bias.astype(F32))
    bias_p = jnp.transpose(bias_all[:, :2 * ATT_BLOCK].reshape(N_HEADS, 2, ATT_BLOCK, tk_p), (1, 0, 2, 3))
    bias_p = bias_p.reshape(2, N_KV, GROUP * ATT_BLOCK, tk_p)
    mask_p = jnp.stack([jnp.tile(geo[0][0], (GROUP, 1)), jnp.tile(geo[1][0], (GROUP, 1))]).astype(F32)
    key_major = lambda a: jnp.swapaxes(jnp.concatenate([a[..., N_META:], a[..., :N_META]], axis=-1), -1, -2)
    bias_pt, mask_pt = key_major(bias_p), key_major(mask_p)
    o0 = 2 * ATT_BLOCK
    seg_m, seg_s = (N_META,), (N_META, n_buf, DT)
    grp_s = SAMPLE_ATT_GROUP
    bias_m = _block_diag_keys(_bias_group_rows(bias_all[:, o0:o0 + N_META, :N_META], N_META), B, seg_m)
    mask_mg = _block_diag_keys(jnp.tile(mask_m, (GROUP, 1)).astype(F32), B, seg_m)
    o1 = o0 + N_META
    bias_s = _block_diag_keys(_bias_group_rows(bias_all[:, o1:o1 + DT, :tk_s], DT), grp_s, seg_s)
    mask_sg = _block_diag_keys(jnp.tile(mask_s, (GROUP, 1)).astype(F32), grp_s, seg_s)

    def sink_rows(sinks, t, n_seq=1):
        rows = jnp.repeat(sinks.astype(F32).reshape(N_KV, GROUP, 1), t, axis=2).reshape(N_KV, GROUP * t, 1)
        return jnp.tile(rows, (1, n_seq, 1))

    swa_p, swa_s, ml_p, ml_s = [], [], [], []
    c_samp = None
    assert n_samp % SMALL_CHUNK == 0 and SMALL_CHUNK % DT == 0
    for layer in range(depth):
        j = layer // 2
        nw = norm_w[layer].astype(F32).reshape(1, D)
        if layer % 2 == 0:
            w_in = swa_w_in[j].astype(BF16)
            w_out = swa_w_out[j].astype(BF16)
            qn = swa_q_norm[j].astype(F32).reshape(1, HEAD_DIM)
            kn = swa_k_norm[j].astype(F32).reshape(1, HEAD_DIM)
            qn_scaled = (qn * (HEAD_DIM ** -0.5)).reshape(HEAD_DIM, 1)
            qt_r, k_r, vt_r, gt_r = _swa_proj_t(xr, nw, w_in.T, qn_scaled, kn.reshape(HEAD_DIM, 1), PROJ_T_ROW_TILE)
            q_s, k_s, v_s, g_s = _swa_proj(xs, nw, w_in, qn, kn, SMALL_ROW_TILE)
            mk = k_s[s_meta].reshape(B, N_META, KV_W)
            mv = v_s[s_meta].reshape(B, N_META, KV_W)
            xr = _swa_prompt(qt_r, k_r, vt_r, gt_r, xr, mk, jnp.swapaxes(mv, 1, 2), bias_pt, mask_pt,
                             jnp.swapaxes(sink_rows(swa_sinks[j], ATT_BLOCK), 1, 2), w_out, B, n_blk)
            k_new = k_s[:n_samp]
            v_new = v_s[:n_samp]
            ck = cache_swa_k[j].astype(F32).reshape(DB * n_buf, KV_W)
            cv = cache_swa_v[j].astype(F32).reshape(DB * n_buf, KV_W)
            cmk = cache_swa_meta_k[j].astype(F32).reshape(DB * N_META, KV_W)
            cmv = cache_swa_meta_v[j].astype(F32).reshape(DB * N_META, KV_W)
            o_samp = _swa_small(_to_group_rows(q_s[:n_samp], DB, DT), [cmk, ck, k_new], [cmv, cv, v_new], seg_s,
                                bias_s, mask_sg, sink_rows(swa_sinks[j], DT, grp_s), grp_s, GROUP * DT)
            o_meta = _swa_small(_to_group_rows(q_s[s_meta], B, N_META), [k_s[s_meta]], [v_s[s_meta]], seg_m,
                                bias_m, mask_mg, sink_rows(swa_sinks[j], N_META, B), B, GROUP * N_META)
            o_small = small_rows(_from_group_rows(o_samp, DB, DT), _from_group_rows(o_meta, B, N_META))
            xs = _resid_matmul(o_small, w_out, xs, SMALL_ROW_TILE, gate=g_s)
            n_keep = min(WINDOW, T + N_META)
            last = lambda a: a.reshape(B, T, KV_W)[:, T - n_keep:].reshape(B, n_keep, N_KV, HEAD_DIM)
            v_last = jnp.stack([vt_r[:, (b + 1) * T - n_keep:(b + 1) * T].T for b in range(B)])
            swa_p.append((last(k_r), v_last.reshape(B, n_keep, N_KV, HEAD_DIM),
                          mk.reshape(B, N_META, N_KV, HEAD_DIM), mv.reshape(B, N_META, N_KV, HEAD_DIM)))
            swa_s.append((k_new.reshape(DB, DT, N_KV, HEAD_DIM), v_new.reshape(DB, DT, N_KV, HEAD_DIM)))
        else:
            w_in = mlstm_w_in[j].astype(BF16)
            w_out = mlstm_w_out[j].astype(BF16)
            conv_wts = (mlstm_conv_w[j].astype(F32), mlstm_conv_b[j].astype(F32).reshape(1, M_INNER))
            wts = (mlstm_wq[j].astype(BF16), mlstm_wk[j].astype(BF16), mlstm_wv[j].astype(BF16),
                   mlstm_w_gates[j].astype(BF16), mlstm_b_gates[j].astype(F32).reshape(1, 2 * M_HEADS))
            hn = mlstm_head_norm[j].astype(F32).reshape(1, M_INNER)
            sk = mlstm_skip[j].astype(F32).reshape(1, M_INNER)
            p_s = _norm_proj(xs, nw, w_in, SMALL_ROW_TILE, M_INNER)
            xm_samp = p_s[:n_samp, :M_INNER].astype(F32).reshape(DB, DT, M_INNER)
            xm_meta = p_s[s_meta, :M_INNER].astype(F32).reshape(B, N_META, M_INNER)
            xpad_s = jnp.concatenate([state_mlstm_conv[j].astype(F32), xm_samp], axis=1)
            xpad_m = jnp.concatenate([jnp.zeros((B, CONV_W - 1, M_INNER), F32), xm_meta], axis=1)
            shifted = [small_rows(xpad_s[:, w:w + DT].reshape(n_samp, M_INNER),
                                  xpad_m[:, w:w + N_META].reshape(n_meta, M_INNER)) for w in range(CONV_W)]
            q_s, k_s, v_s, gt_s, xc_s = _qkv_small(shifted, conv_wts, wts, SMALL_ROW_TILE)
            zo_s = p_s[:, M_INNER:]
            cs_real = jnp.pad(xpad_m[:, N_META:], ((0, 0), (8 - (CONV_W - 1), 0), (0, 0)))
            xm_r, xc_r = _norm_proj_conv(xr.reshape(B, T, D), nw, w_in[:, :M_INNER], cs_real, *conv_wts, 512)
            zo_r = _norm_proj(xr, nw, w_in[:, M_INNER:], 512, M_INNER).reshape(B, T, 2 * M_INNER)
            q_r, k_r, v_r, gt_r = _qkv_real(xc_r, xm_r, wts, ROW_TILE)

            to3 = lambda a, sl, b, t: a[sl].reshape(b, t, a.shape[-1])
            zc = jnp.zeros((1, B, M_HEADS, M_HD, M_HD), F32)
            zn = jnp.zeros((1, B, M_HEADS, 1, M_HD), F32)
            zm = jnp.zeros((1, B, M_HEADS, 1, 128), F32)
            u_m, c_m, n_m, m_m = _mlstm_chunk(
                to3(q_s, s_meta, B, N_META), to3(k_s, s_meta, B, N_META), to3(v_s, s_meta, B, N_META),
                to3(xc_s, s_meta, B, N_META), to3(zo_s, s_meta, B, N_META), to3(gt_s, s_meta, B, N_META),
                zc, zn, zm, 0, hn, sk, N_META, N_META, M_HEADS)
            x_new, c_r, n_r, m_r = _mlstm_chunk(q_r, k_r, v_r, xc_r, zo_r, gt_r, c_m, n_m, m_m, 0, hn, sk,
                                                PROMPT_CHUNK, PROMPT_CHUNK, M_HEADS,
                                                x=xr.reshape(B, T, D), w_out=w_out)
            s_samp = slice(0, n_samp)
            nb_s = n_samp // SMALL_CHUNK
            u_s, c_samp, n_s, m_s = _mlstm_chunk(
                to3(q_s, s_samp, nb_s, SMALL_CHUNK), to3(k_s, s_samp, nb_s, SMALL_CHUNK),
                to3(v_s, s_samp, nb_s, SMALL_CHUNK), to3(xc_s, s_samp, nb_s, SMALL_CHUNK),
                to3(zo_s, s_samp, nb_s, SMALL_CHUNK), to3(gt_s, s_samp, nb_s, SMALL_CHUNK),
                state_mlstm_c.astype(F32), state_mlstm_n.astype(F32)[:, :, :, None, :],
                jnp.broadcast_to(state_mlstm_m.astype(F32)[..., None, None], state_mlstm_m.shape + (1, 128)),
                j, hn, sk, SMALL_CHUNK, DT, 1, c_acc=c_samp, lyr_out=j, n_lyr_out=state_mlstm_c.shape[0])
            xr = x_new.reshape(n_real, D)
            u_small = small_rows(u_s.reshape(n_samp, M_INNER), u_m.reshape(n_meta, M_INNER))
            xs = _resid_matmul(u_small, w_out, xs, SMALL_ROW_TILE)
            ml_p.append((c_r[0], n_r.reshape(B, M_HEADS, M_HD), m_r[0, :, :, 0, 0],
                         xm_r[:, T - (CONV_W - 1):].astype(F32)))
            ml_s.append((n_s.reshape(DB, M_HEADS, M_HD), m_s[0, :, :, 0, 0], xpad_s[:, DT:]))

    sd = state_mlstm_c.dtype
    stack = lambda rows, idx: jnp.stack([r[idx] for r in rows])
    window = lambda cache, idx: jnp.concatenate([cache[:, :, DT:].astype(F32), stack(swa_s, idx)], axis=2)
    return (xr.reshape(B, T, D), xs[:n_samp].reshape(DB, DT, D),
            stack(swa_p, 0), stack(swa_p, 1), stack(swa_p, 2), stack(swa_p, 3),
            stack(ml_p, 0).astype(sd), stack(ml_p, 1).astype(sd), stack(ml_p, 2).astype(sd), stack(ml_p, 3),
            window(cache_swa_k, 0), window(cache_swa_v, 1),
            c_samp.astype(sd), stack(ml_s, 0).astype(sd), stack(ml_s, 1).astype(sd), stack(ml_s, 2))
```

```python
import functools
import math

import jax
import jax.numpy as jnp
from jax import lax
from jax.experimental import pallas as pl
from jax.experimental.pallas import tpu as pltpu

F32 = jnp.float32
BF16 = jnp.bfloat16

D_MODEL = 1024
N_META = 16
EPS = 1e-6
NEG_INF = -1e30
N_HEADS = 16
HEAD_DIM = 64
N_KV = 4
GROUP = 4
ATT_W = N_HEADS * HEAD_DIM
KV_W = N_KV * HEAD_DIM
WINDOW = 128
ATT_BLOCK = 128
N_BUCKETS = 32
MAX_DISTANCE = 128
M_INNER = 2048
M_HEADS = 4
M_HD = 512
CONV_W = 4
PAST_LEN = 8192

VMEM_LIMIT_BYTES = 56 * 1024 * 1024
PROMPT_CHUNK = 256
SMALL_CHUNK = 16
ROW_TILE = 256
PROJ_T_ROW_TILE = 512
SMALL_ROW_TILE = 128
SAMPLE_ATT_GROUP = 8


def _cparams(*sem):
    return pltpu.CompilerParams(dimension_semantics=sem, vmem_limit_bytes=VMEM_LIMIT_BYTES)


def _rms(x, w):
    ms = jnp.mean(x * x, axis=-1, keepdims=True)
    return x * lax.rsqrt(ms + EPS) * w


def _sigmoid(x):
    return 1.0 / (1.0 + jnp.exp(-x))


def _silu(x):
    return x * _sigmoid(x)


def _dot(a, b):
    return jnp.dot(a, b, preferred_element_type=F32)


def _dot_nt(a, b):
    return lax.dot_general(a, b, (((1,), (1,)), ((), ())), preferred_element_type=F32)


def _dot_tn(a, b):
    return lax.dot_general(a, b, (((0,), (0,)), ((), ())), preferred_element_type=F32)


def _norm_proj_kernel(x_ref, nw_ref, w_ref, o_ref):
    h = _rms(x_ref[...], nw_ref[...]).astype(BF16)
    o_ref[...] = _dot(h, w_ref[...]).astype(o_ref.dtype)


def _norm_proj(x, nw, w, tm, tn, col0=0):
    rows, d = x.shape
    n = w.shape[1] - col0
    j0 = col0 // tn
    return pl.pallas_call(
        _norm_proj_kernel,
        grid=(n // tn, rows // tm),
        in_specs=[pl.BlockSpec((tm, d), lambda j, i: (i, 0)),
                  pl.BlockSpec((1, d), lambda j, i: (0, 0)),
                  pl.BlockSpec((d, tn), lambda j, i: (0, j0 + j))],
        out_specs=pl.BlockSpec((tm, tn), lambda j, i: (i, j)),
        out_shape=jax.ShapeDtypeStruct((rows, n), BF16),
        compiler_params=_cparams("parallel", "parallel"),
        name="norm_proj",
    )(x, nw, w)


def _swa_proj_kernel(x_ref, nw_ref, w_ref, qn_ref, kn_ref, q_ref, k_ref, v_ref, g_ref):
    h = _rms(x_ref[...], nw_ref[...]).astype(BF16)
    proj = _dot(h, w_ref[...])
    qn = qn_ref[...]
    kn = kn_ref[...]
    for hd in range(N_HEADS):
        sl = proj[:, hd * HEAD_DIM:(hd + 1) * HEAD_DIM]
        q_ref[:, hd * HEAD_DIM:(hd + 1) * HEAD_DIM] = _rms(sl, qn).astype(q_ref.dtype)
    for hd in range(N_KV):
        sl = proj[:, ATT_W + hd * HEAD_DIM:ATT_W + (hd + 1) * HEAD_DIM]
        k_ref[:, hd * HEAD_DIM:(hd + 1) * HEAD_DIM] = _rms(sl, kn)
    v_ref[...] = proj[:, ATT_W + KV_W:ATT_W + 2 * KV_W]
    g_ref[...] = _silu(proj[:, ATT_W + 2 * KV_W:])


def _swa_proj(x, nw, w, qn, kn, tm):
    rows, d = x.shape
    n = w.shape[1]
    row = lambda i: (i, 0)
    fixed = lambda i: (0, 0)
    return pl.pallas_call(
        _swa_proj_kernel,
        grid=(rows // tm,),
        in_specs=[pl.BlockSpec((tm, d), row), pl.BlockSpec((1, d), fixed), pl.BlockSpec((d, n), fixed),
                  pl.BlockSpec((1, HEAD_DIM), fixed), pl.BlockSpec((1, HEAD_DIM), fixed)],
        out_specs=[pl.BlockSpec((tm, ATT_W), row), pl.BlockSpec((tm, KV_W), row),
                   pl.BlockSpec((tm, KV_W), row), pl.BlockSpec((tm, ATT_W), row)],
        out_shape=[jax.ShapeDtypeStruct((rows, ATT_W), BF16), jax.ShapeDtypeStruct((rows, KV_W), F32),
                   jax.ShapeDtypeStruct((rows, KV_W), F32), jax.ShapeDtypeStruct((rows, ATT_W), F32)],
        compiler_params=_cparams("parallel"),
        name="swa_proj",
    )(x, nw, w, qn, kn)


def _swa_proj_t_kernel(x_ref, nw_ref, wt_ref, qn_ref, kn_ref, qt_ref, k_ref, vt_ref, gt_ref):
    h = _rms(x_ref[...], nw_ref[...]).astype(BF16)
    pt = _dot_nt(wt_ref[...], h)

    def head_norm(r0, gain):
        sl = pt[r0:r0 + HEAD_DIM]
        ms = jnp.mean(sl * sl, axis=0, keepdims=True)
        return sl * lax.rsqrt(ms + EPS) * gain

    qn = qn_ref[...]
    kn = kn_ref[...]
    for hd in range(N_HEADS):
        qt_ref[hd * HEAD_DIM:(hd + 1) * HEAD_DIM, :] = head_norm(hd * HEAD_DIM, qn).astype(qt_ref.dtype)
    kt = jnp.concatenate([head_norm(ATT_W + hd * HEAD_DIM, kn) for hd in range(N_KV)], axis=0)
    k_ref[...] = kt.T
    vt_ref[...] = pt[ATT_W + KV_W:ATT_W + 2 * KV_W]
    gt_ref[...] = _silu(pt[ATT_W + 2 * KV_W:])


def _swa_proj_t(x, nw, wt, qn_col, kn_col, tm):
    rows, d = x.shape
    nt = wt.shape[0]
    row = lambda i: (i, 0)
    col = lambda i: (0, i)
    fixed = lambda i: (0, 0)
    return pl.pallas_call(
        _swa_proj_t_kernel,
        grid=(rows // tm,),
        in_specs=[pl.BlockSpec((tm, d), row), pl.BlockSpec((1, d), fixed), pl.BlockSpec((nt, d), fixed),
                  pl.BlockSpec((HEAD_DIM, 1), fixed), pl.BlockSpec((HEAD_DIM, 1), fixed)],
        out_specs=[pl.BlockSpec((ATT_W, tm), col), pl.BlockSpec((tm, KV_W), row),
                   pl.BlockSpec((KV_W, tm), col), pl.BlockSpec((ATT_W, tm), col)],
        out_shape=[jax.ShapeDtypeStruct((ATT_W, rows), BF16), jax.ShapeDtypeStruct((rows, KV_W), F32),
                   jax.ShapeDtypeStruct((KV_W, rows), F32), jax.ShapeDtypeStruct((ATT_W, rows), F32)],
        compiler_params=_cparams("parallel"),
        name="swa_proj_t",
    )(x, nw, wt, qn_col, kn_col)


def _resid_matmul_kernel(u_ref, w_ref, x_ref, o_ref):
    o_ref[...] = x_ref[...] + _dot(u_ref[...], w_ref[...])


def _gated_resid_matmul_kernel(a_ref, g_ref, w_ref, x_ref, o_ref):
    u = (a_ref[...] * g_ref[...]).astype(BF16)
    o_ref[...] = x_ref[...] + _dot(u, w_ref[...])


def _resid_matmul(u, w, x, tm, gate=None):
    rows, k = u.shape
    d = w.shape[1]
    row = lambda i: (i, 0)
    ins = [u] if gate is None else [u, gate]
    body = _resid_matmul_kernel if gate is None else _gated_resid_matmul_kernel
    return pl.pallas_call(
        body,
        grid=(rows // tm,),
        in_specs=[pl.BlockSpec((tm, k), row)] * len(ins)
                 + [pl.BlockSpec((k, d), lambda i: (0, 0)), pl.BlockSpec((tm, d), row)],
        out_specs=pl.BlockSpec((tm, d), row),
        out_shape=jax.ShapeDtypeStruct((rows, d), F32),
        compiler_params=_cparams("parallel"),
        name="resid_matmul",
    )(*ins, w, x)


def _bias_table_kernel(idx_ref, tab_ref, o_ref):
    idx = idx_ref[...]
    for hd in range(N_HEADS):
        acc = jnp.zeros(idx.shape, F32)
        for b in range(N_BUCKETS):
            acc = jnp.where(idx == b, tab_ref[b, hd], acc)
        o_ref[hd] = acc


def _bias_table(idx, rel_bias):
    r, c = idx.shape
    return pl.pallas_call(
        _bias_table_kernel,
        in_specs=[pl.BlockSpec((r, c), lambda: (0, 0)), pl.BlockSpec(memory_space=pltpu.SMEM)],
        out_specs=pl.BlockSpec((N_HEADS, r, c), lambda: (0, 0, 0)),
        out_shape=jax.ShapeDtypeStruct((N_HEADS, r, c), F32),
        name="bias_table",
    )(idx, rel_bias)


def _rel_bucket(dist):
    n = jnp.maximum(dist, 0)
    max_exact = N_BUCKETS // 2
    nf = jnp.maximum(n, 1).astype(F32)
    large = max_exact + (jnp.log(nf / max_exact) / math.log(MAX_DISTANCE / max_exact)
                         * (N_BUCKETS - max_exact)).astype(jnp.int32)
    large = jnp.minimum(large, N_BUCKETS - 1)
    return jnp.where(n < max_exact, n, large)


def _mask_and_bucket(q_pos, k_pos, k_valid):
    dist = q_pos[:, None] - k_pos[None, :]
    mask = k_valid[None, :] & (dist >= 0) & ((dist <= WINDOW) | (k_pos[None, :] < N_META))
    return mask, _rel_bucket(dist)


def _swa_prompt_kernel(qt_ref, kc_ref, kp_ref, vtc_ref, vtp_ref, mk_ref, mvt_ref, gt_ref, x_ref,
                       bias_ref, mask_ref, sink_ref, w_ref, o_ref, att_ref):
    kband = jnp.concatenate([kp_ref[...], kc_ref[...], mk_ref[...]], axis=0).astype(BF16)
    vtband = jnp.concatenate([vtp_ref[...], vtc_ref[...], mvt_ref[...]], axis=1).astype(BF16)
    qt = qt_ref[...]
    keep = mask_ref[...] > 0.5
    hp = GROUP
    for kv in range(N_KV):
        kk = kband[:, kv * HEAD_DIM:(kv + 1) * HEAD_DIM]
        vv = vtband[kv * HEAD_DIM:(kv + 1) * HEAD_DIM]
        for g0 in range(0, GROUP, hp):
            lanes = slice(g0 * ATT_BLOCK, (g0 + hp) * ATT_BLOCK)
            qtg = jnp.concatenate(
                [qt[(kv * GROUP + g) * HEAD_DIM:(kv * GROUP + g + 1) * HEAD_DIM] for g in range(g0, g0 + hp)], axis=1)
            logits = _dot(kk, qtg) + bias_ref[kv, :, lanes]
            logits = jnp.where(keep[:, lanes], logits, NEG_INF)
            sk = sink_ref[kv, :, lanes]
            m = jnp.maximum(jnp.max(logits, axis=0, keepdims=True), sk)
            p = jnp.exp(logits - m)
            den = jnp.sum(p, axis=0, keepdims=True) + jnp.exp(sk - m)
            otg = _dot(vv, p.astype(BF16)) * (1.0 / den)
            for g in range(hp):
                hd = kv * GROUP + g0 + g
                att_ref[hd * HEAD_DIM:(hd + 1) * HEAD_DIM, :] = otg[:, g * ATT_BLOCK:(g + 1) * ATT_BLOCK]
    ut = (att_ref[...] * gt_ref[...]).astype(BF16)
    o_ref[...] = x_ref[...] + _dot_tn(ut, w_ref[...])


def _swa_prompt(qt, kn, vt, sgt, x, mk, mvt, bias_t, mask_t, sink_t, w_out, batch, n_blk):
    cur = lambda b, i: (b * n_blk + i, 0)
    prev = lambda b, i: (b * n_blk + jnp.maximum(i - 1, 0), 0)
    cur_t = lambda b, i: (0, b * n_blk + i)
    prev_t = lambda b, i: (0, b * n_blk + jnp.maximum(i - 1, 0))
    meta = lambda b, i: (b, 0, 0)
    tk = N_META + 2 * ATT_BLOCK
    nq = GROUP * ATT_BLOCK
    rows = kn.shape[0]
    return pl.pallas_call(
        _swa_prompt_kernel,
        grid=(batch, n_blk),
        in_specs=[pl.BlockSpec((ATT_W, ATT_BLOCK), cur_t),
                  pl.BlockSpec((ATT_BLOCK, KV_W), cur), pl.BlockSpec((ATT_BLOCK, KV_W), prev),
                  pl.BlockSpec((KV_W, ATT_BLOCK), cur_t), pl.BlockSpec((KV_W, ATT_BLOCK), prev_t),
                  pl.BlockSpec((None, N_META, KV_W), meta), pl.BlockSpec((None, KV_W, N_META), meta),
                  pl.BlockSpec((ATT_W, ATT_BLOCK), cur_t), pl.BlockSpec((ATT_BLOCK, D_MODEL), cur),
                  pl.BlockSpec((None, N_KV, tk, nq), lambda b, i: (jnp.minimum(i, 1), 0, 0, 0)),
                  pl.BlockSpec((None, tk, nq), lambda b, i: (jnp.minimum(i, 1), 0, 0)),
                  pl.BlockSpec((N_KV, 1, nq), lambda b, i: (0, 0, 0)),
                  pl.BlockSpec((ATT_W, D_MODEL), lambda b, i: (0, 0))],
        out_specs=pl.BlockSpec((ATT_BLOCK, D_MODEL), cur),
        out_shape=jax.ShapeDtypeStruct((rows, D_MODEL), F32),
        scratch_shapes=[pltpu.VMEM((ATT_W, ATT_BLOCK), F32)],
        compiler_params=_cparams("parallel", "parallel"),
        name="swa_prompt",
    )(qt, kn, kn, vt, vt, mk, mvt, sgt, x, bias_t, mask_t, sink_t, w_out)


def _swa_small_kernel(n_seg, *refs):
    q_ref = refs[0]
    k_refs = refs[1:1 + n_seg]
    v_refs = refs[1 + n_seg:1 + 2 * n_seg]
    bias_ref, mask_ref, sink_ref, o_ref = refs[1 + 2 * n_seg:]
    keep = mask_ref[...] > 0.5
    scale = HEAD_DIM ** -0.5
    kall = jnp.concatenate([r[...] for r in k_refs], axis=0).astype(BF16) if n_seg > 1 else k_refs[0][...].astype(BF16)
    vall = jnp.concatenate([r[...] for r in v_refs], axis=0).astype(BF16) if n_seg > 1 else v_refs[0][...].astype(BF16)
    for kv in range(N_KV):
        kk = kall[:, kv * HEAD_DIM:(kv + 1) * HEAD_DIM]
        vv = vall[:, kv * HEAD_DIM:(kv + 1) * HEAD_DIM]
        logits = _dot_nt(q_ref[kv], kk) * scale + bias_ref[kv]
        logits = jnp.where(keep, logits, NEG_INF)
        sk = sink_ref[kv]
        m = jnp.maximum(jnp.max(logits, axis=-1, keepdims=True), sk)
        p = jnp.exp(logits - m)
        den = jnp.sum(p, axis=-1, keepdims=True) + jnp.exp(sk - m)
        o_ref[kv] = _dot(p.astype(BF16), vv) / den


def _swa_small(q, k_segs, v_segs, seg_len, bias, mask, sink, group, rq):
    rows = q.shape[1]
    n_seg = len(k_segs)
    tk = group * sum(seg_len)
    seg_specs = [pl.BlockSpec((group * n, KV_W), lambda i: (i, 0)) for n in seg_len]
    return pl.pallas_call(
        functools.partial(_swa_small_kernel, n_seg),
        grid=(rows // (group * rq),),
        in_specs=[pl.BlockSpec((N_KV, group * rq, HEAD_DIM), lambda i: (0, i, 0))] + seg_specs + seg_specs
                 + [pl.BlockSpec((N_KV, group * rq, tk), lambda i: (0, 0, 0)),
                    pl.BlockSpec((group * rq, tk), lambda i: (0, 0)),
                    pl.BlockSpec((N_KV, group * rq, 1), lambda i: (0, 0, 0))],
        out_specs=pl.BlockSpec((N_KV, group * rq, HEAD_DIM), lambda i: (0, i, 0)),
        out_shape=jax.ShapeDtypeStruct((N_KV, rows, HEAD_DIM), F32),
        compiler_params=_cparams("parallel"),
        name="swa_small",
    )(q, *k_segs, *v_segs, bias, mask, sink)


def _to_group_rows(a, nb, t):
    a = a.reshape(nb, t, N_KV, GROUP, HEAD_DIM)
    return jnp.transpose(a, (2, 0, 3, 1, 4)).reshape(N_KV, nb * GROUP * t, HEAD_DIM)


def _from_group_rows(a, nb, t):
    a = a.reshape(N_KV, nb, GROUP, t, HEAD_DIM)
    return jnp.transpose(a, (1, 3, 0, 2, 4)).reshape(nb * t, ATT_W)


def _bias_group_rows(b, t):
    return b.reshape(N_KV, GROUP * t, b.shape[-1])


def _block_diag_keys(a, group, seg_len):
    eye = jnp.eye(group, dtype=a.dtype)
    out, o = [], 0
    for n in seg_len:
        blk = a[..., o:o + n]
        o += n
        big = eye[:, None, :, None] * blk[..., None, :, None, :]
        out.append(big.reshape(a.shape[:-2] + (group * a.shape[-2], group * n)))
    return jnp.concatenate(out, axis=-1)


def _conv_silu(xs, cw_ref, cb_ref):
    cw = cw_ref[...]
    acc = xs[0] * cw[0:1]
    for w in range(1, CONV_W):
        acc = acc + xs[w] * cw[w:w + 1]
    return _silu(acc + cb_ref[...])


def _mlstm_in_kernel(x_ref, nw_ref, w_ref, cs_ref, cw_ref, cb_ref, wq_ref, wk_ref, wv_ref, wg_ref, bg_ref,
                     q_ref, k_ref, v_ref, gt_ref, xc_ref, tail_ref):
    i = pl.program_id(1)
    h = _rms(x_ref[...], nw_ref[...]).astype(BF16)
    xm = _dot(h, w_ref[...])
    tm = xm.shape[0]
    halo = jnp.where(i == 0, cs_ref[...], tail_ref[...])
    full = jnp.concatenate([halo, xm], axis=0)
    n = tm + 8
    xs = [pltpu.roll(full, n - (8 - (CONV_W - 1) + w), 0)[:tm] for w in range(CONV_W - 1)] + [xm]
    xcb = _conv_silu(xs, cw_ref, cb_ref).astype(BF16)
    xc_ref[...] = xcb
    tail_ref[...] = xm[tm - 8:]
    _qkv_matmuls(xcb, xm.astype(BF16), wq_ref, wk_ref, wv_ref, wg_ref, bg_ref, q_ref, k_ref, v_ref, gt_ref)


def _mlstm_in(x3, nw, w, cs, conv_wts, wts, tm):
    b, t, d = x3.shape
    cur = lambda bi, i: (bi, i, 0)
    seq = lambda bi, i: (bi, 0, 0)
    fixed = lambda bi, i: (0, 0)
    tok = pl.BlockSpec((None, tm, M_INNER), cur)
    return pl.pallas_call(
        _mlstm_in_kernel,
        grid=(b, t // tm),
        in_specs=[pl.BlockSpec((None, tm, d), cur), pl.BlockSpec((1, d), fixed), pl.BlockSpec((d, M_INNER), fixed),
                  pl.BlockSpec((None, 8, M_INNER), seq),
                  pl.BlockSpec((CONV_W, M_INNER), fixed), pl.BlockSpec((1, M_INNER), fixed)]
                 + _qkv_weight_specs(fixed, lambda bi, i: (0, 0, 0)),
        out_specs=[tok, tok, tok, pl.BlockSpec((None, tm, 2 * M_HEADS), cur), tok,
                   pl.BlockSpec((None, 8, M_INNER), seq)],
        out_shape=[jax.ShapeDtypeStruct((b, t, M_INNER), BF16)] * 3
                  + [jax.ShapeDtypeStruct((b, t, 2 * M_HEADS), F32), jax.ShapeDtypeStruct((b, t, M_INNER), BF16),
                     jax.ShapeDtypeStruct((b, 8, M_INNER), F32)],
        compiler_params=_cparams("parallel", "arbitrary"),
        name="mlstm_in",
    )(x3, nw, w, cs, *conv_wts, *wts)


def _qkv_matmuls(xcb, xmb, wq_ref, wk_ref, wv_ref, wg_ref, bg_ref, q_ref, k_ref, v_ref, gt_ref):
    gates = bg_ref[...]
    for h in range(M_HEADS):
        sl = slice(h * M_HD, (h + 1) * M_HD)
        qh = _dot(xcb[:, sl], wq_ref[h]).astype(BF16)
        kh = (_dot(xcb[:, sl], wk_ref[h]) * (M_HD ** -0.5)).astype(BF16)
        vh = _dot(xmb[:, sl], wv_ref[h]).astype(BF16)
        q_ref[:, sl] = qh
        k_ref[:, sl] = kh
        v_ref[:, sl] = vh
        gates = (gates + _dot(qh, wg_ref[h * M_HD:(h + 1) * M_HD])
                 + _dot(kh, wg_ref[M_INNER + h * M_HD:M_INNER + (h + 1) * M_HD])
                 + _dot(vh, wg_ref[2 * M_INNER + h * M_HD:2 * M_INNER + (h + 1) * M_HD]))
    lane = lax.broadcasted_iota(jnp.int32, gates.shape, 1)
    log_f = jnp.minimum(gates, 0.0) - jnp.log1p(jnp.exp(-jnp.abs(gates)))
    gt_ref[...] = jnp.where(lane < M_HEADS, gates, log_f)


def _qkv_shift_kernel(x0_ref, x1_ref, x2_ref, x3_ref, cw_ref, cb_ref, *rest):
    xs = [x0_ref[...], x1_ref[...], x2_ref[...], x3_ref[...]]
    xcb = _conv_silu(xs, cw_ref, cb_ref).astype(BF16)
    xc_ref = rest[-1]
    xc_ref[...] = xcb
    _qkv_matmuls(xcb, xs[CONV_W - 1].astype(BF16), *rest[:-1])


def _qkv_weight_specs(fixed2, fixed3):
    return [pl.BlockSpec((M_HEADS, M_HD, M_HD), fixed3), pl.BlockSpec((M_HEADS, M_HD, M_HD), fixed3),
            pl.BlockSpec((M_HEADS, M_HD, M_HD), fixed3),
            pl.BlockSpec((3 * M_INNER, 2 * M_HEADS), fixed2), pl.BlockSpec((1, 2 * M_HEADS), fixed2)]


def _qkv_small(xs, conv_wts, wts, tm):
    rows = xs[0].shape[0]
    row = lambda i: (i, 0)
    fixed = lambda i: (0, 0)
    outs = ([jax.ShapeDtypeStruct((rows, M_INNER), BF16)] * 3 + [jax.ShapeDtypeStruct((rows, 2 * M_HEADS), F32)]
            + [jax.ShapeDtypeStruct((rows, M_INNER), BF16)])
    return pl.pallas_call(
        _qkv_shift_kernel,
        grid=(rows // tm,),
        in_specs=[pl.BlockSpec((tm, M_INNER), row)] * CONV_W
                 + [pl.BlockSpec((CONV_W, M_INNER), fixed), pl.BlockSpec((1, M_INNER), fixed)]
                 + _qkv_weight_specs(fixed, lambda i: (0, 0, 0)),
        out_specs=[pl.BlockSpec((tm, M_INNER), row)] * 3 + [pl.BlockSpec((tm, 2 * M_HEADS), row)]
                  + [pl.BlockSpec((tm, M_INNER), row)],
        out_shape=outs,
        compiler_params=_cparams("parallel"),
        name="mlstm_qkv_small",
    )(*xs, *conv_wts, *wts)


def _chunk_kernel(seg, hps, aliased, fused, q_ref, k_ref, v_ref, xc_ref, z_ref, op_ref, gc_ref, gr_ref, c0_ref, n0_ref,
                  m0_ref, hn_ref, sk_ref, *rest):
    if fused:
        x_ref, w_ref = rest[:2]
        rest = rest[2:]
    if aliased:
        rest = rest[1:]
    out_ref, c_ref, n_ref, m_ref = rest[:4]
    u_ref = rest[4] if fused else out_ref
    h_blk = pl.program_id(1)
    c_idx = pl.program_id(2)
    L = q_ref.shape[0]
    nseg = L // seg
    hi = lax.Precision.HIGHEST

    @pl.when(c_idx == 0)
    def _():
        c_ref[...] = c0_ref[...]
        n_ref[...] = n0_ref[...]
        m_ref[...] = m0_ref[...]

    gc = gc_ref[...]
    gr = gr_ref[...]
    col_c = lax.broadcasted_iota(jnp.int32, gc.shape, 1)
    row_r = lax.broadcasted_iota(jnp.int32, gr.shape, 0)
    t_i = lax.broadcasted_iota(jnp.int32, (L, L), 0)
    j_i = lax.broadcasted_iota(jnp.int32, (L, L), 1)
    causal = j_i <= t_i
    upper = t_i <= j_i
    if nseg > 1:
        sid_c = lax.broadcasted_iota(jnp.int32, (L, 1), 0) // seg
        in_seg = [sid_c == s for s in range(nseg)]
        pick = lambda vals: sum(jnp.where(in_seg[s], vals[s], 0.0) for s in range(nseg))
        same = (t_i // seg) == (j_i // seg)
        causal = causal & same
        upper = upper & same
    tril = jnp.where(causal, 1.0, 0.0).astype(F32)
    triu = jnp.where(upper, 1.0, 0.0).astype(F32)
    lf_c = jnp.where(col_c >= M_HEADS, gc, 0.0)
    lf_r = jnp.where(row_r >= M_HEADS, gr, 0.0)
    b_all_c = jnp.dot(tril, lf_c, precision=hi, preferred_element_type=F32)
    b_all_r = jnp.dot(lf_r, triu, precision=hi, preferred_element_type=F32)
    if nseg > 1:
        g_all = jnp.dot(jnp.where(same, 1.0, 0.0).astype(F32), lf_c, precision=hi, preferred_element_type=F32)

    for hh in range(hps):
        h_idx = h_blk * hps + hh
        cols = slice(hh * M_HD, (hh + 1) * M_HD)
        sel_c = lambda a, c: jnp.sum(jnp.where(col_c == c, a, 0.0), axis=1, keepdims=True)
        sel_r = lambda a, r: jnp.sum(jnp.where(row_r == r, a, 0.0), axis=0, keepdims=True)
        i_c, i_r = sel_c(gc, h_idx), sel_r(gr, h_idx)
        b_c, b_r = sel_c(b_all_c, M_HEADS + h_idx), sel_r(b_all_r, M_HEADS + h_idx)
        if nseg > 1:
            g = sel_c(g_all, M_HEADS + h_idx)
            m_old = [m_ref[s, hh][:, 0:1] for s in range(nseg)]
            m_prev = pick(m_old)
        else:
            g = b_c[L - 1:L, :]
            m_prev = m_ref[0, hh][:, 0:1]

        log_d = jnp.where(causal, b_c + (i_r - b_r), -jnp.inf)
        inter = b_c + m_prev
        m_t = jnp.maximum(inter, jnp.max(log_d, axis=-1, keepdims=True))
        w_intra = jnp.exp(log_d - m_t)
        w_inter = jnp.exp(inter - m_t)
        q = q_ref[:, cols]
        k = k_ref[:, cols]
        v = v_ref[:, cols]
        s_mat = _dot_nt(q, k) * w_intra
        qf = q.astype(F32)
        if nseg > 1:
            q_c = pick([_dot(q, c_ref[s, hh].astype(BF16)) for s in range(nseg)])
            q_n = pick([jnp.sum(qf * n_ref[s, hh], axis=-1, keepdims=True) for s in range(nseg)])
        else:
            q_c = _dot(q, c_ref[0, hh].astype(BF16))
            q_n = jnp.sum(qf * n_ref[0, hh], axis=-1, keepdims=True)
        num = _dot(s_mat.astype(BF16), v) + w_inter * q_c
        den = jnp.sum(s_mat, axis=-1, keepdims=True) + w_inter * q_n
        h = num * (1.0 / jnp.maximum(jnp.abs(den), jnp.exp(-m_t)))

        lw = (g - b_c) + i_c
        kf = k.astype(F32)
        if nseg > 1:
            g_s = [jnp.max(jnp.where(in_seg[s], g, -jnp.inf), axis=0, keepdims=True) for s in range(nseg)]
            m_new = [jnp.maximum(g_s[s] + m_old[s],
                                 jnp.max(jnp.where(in_seg[s], lw, -jnp.inf), axis=0, keepdims=True))
                     for s in range(nseg)]
            kw = kf * jnp.exp(lw - pick(m_new))
            for s in range(nseg):
                decay = jnp.exp(g_s[s] + m_old[s] - m_new[s])
                kw_s = jnp.where(in_seg[s], kw, 0.0)
                c_ref[s, hh] = decay * c_ref[s, hh] + _dot_tn(kw_s.astype(BF16), v)
                n_ref[s, hh] = decay * n_ref[s, hh] + jnp.sum(kw_s, axis=0, keepdims=True)
                m_ref[s, hh] = jnp.broadcast_to(m_new[s], m_ref.shape[2:])
        else:
            m_new = jnp.maximum(g + m_prev, jnp.max(lw, axis=0, keepdims=True))
            decay = jnp.exp(g + m_prev - m_new)
            kw = kf * jnp.exp(lw - m_new)
            c_ref[0, hh] = decay * c_ref[0, hh] + _dot_tn(kw.astype(BF16), v)
            n_ref[0, hh] = decay * n_ref[0, hh] + jnp.sum(kw, axis=0, keepdims=True)
            m_ref[0, hh] = jnp.broadcast_to(m_new, m_ref.shape[2:])

        mu = jnp.mean(h, axis=-1, keepdims=True)
        hc = h - mu
        var = jnp.mean(hc * hc, axis=-1, keepdims=True)
        h_out = _sigmoid(op_ref[:, cols].astype(F32)) * (hc * lax.rsqrt(var + EPS) * hn_ref[:, cols])
        u = (h_out + sk_ref[:, cols] * xc_ref[:, cols].astype(F32)) * _silu(z_ref[:, cols].astype(F32))
        u_ref[:, cols] = u.astype(u_ref.dtype)
    if fused:
        out_ref[...] = x_ref[...] + _dot(u_ref[...], w_ref[...])


def _mlstm_chunk(q, k, v, xc, zo, gates, c0, n0, m0, lyr_in, head_norm, skip, chunk, seg, hps,
                 c_acc=None, lyr_out=0, n_lyr_out=1, x=None, w_out=None):
    nb, t, _ = q.shape
    nc = t // chunk
    nseg = chunk // seg
    assert nseg == 1 or nc == 1
    n_seq = nb * nseg
    n_hblk = M_HEADS // hps
    w = hps * M_HD
    gates_t = jnp.swapaxes(gates, 1, 2)
    tok = lambda bi, h, c: (bi, c, h)
    hd_blk = pl.BlockSpec((None, chunk, w), tok)
    st_in = lambda bi, h, c: (lyr_in, bi, h, 0, 0)
    st_out = lambda bi, h, c: (0, bi, h, 0, 0)
    c_out = lambda bi, h, c: (lyr_out, bi, h, 0, 0)
    head_row = pl.BlockSpec((1, w), lambda bi, h, c: (0, h))
    in_specs = [hd_blk, hd_blk, hd_blk, hd_blk,
                hd_blk,
                pl.BlockSpec((None, chunk, w), lambda bi, h, c: (bi, c, n_hblk + h)),
                pl.BlockSpec((None, chunk, 2 * M_HEADS), lambda bi, h, c: (bi, c, 0)),
                pl.BlockSpec((None, 2 * M_HEADS, chunk), lambda bi, h, c: (bi, 0, c)),
                pl.BlockSpec((None, nseg, hps, M_HD, M_HD), st_in),
                pl.BlockSpec((None, nseg, hps, 1, M_HD), st_in),
                pl.BlockSpec((None, nseg, hps, 1, 128), st_in),
                head_row, head_row]
    args = [q, k, v, xc, zo, zo, gates, gates_t, c0, n0, m0, head_norm, skip]
    fused = x is not None
    first_out = (hd_blk, jax.ShapeDtypeStruct((nb, t, M_INNER), BF16))
    scratch = []
    if fused:
        assert hps == M_HEADS
        d = w_out.shape[1]
        x_blk = pl.BlockSpec((None, chunk, d), lambda bi, h, c: (bi, c, 0))
        in_specs += [x_blk, pl.BlockSpec((M_INNER, d), lambda bi, h, c: (0, 0))]
        args += [x, w_out]
        first_out = (x_blk, jax.ShapeDtypeStruct((nb, t, d), F32))
        scratch = [pltpu.VMEM((chunk, M_INNER), BF16)]
    aliases = {}
    if c_acc is not None:
        in_specs.append(pl.BlockSpec(memory_space=pl.ANY))
        aliases = {len(args): 1}
        args.append(c_acc)
    return pl.pallas_call(
        functools.partial(_chunk_kernel, seg, hps, c_acc is not None, fused),
        grid=(nb, n_hblk, nc),
        in_specs=in_specs,
        out_specs=[first_out[0],
                   pl.BlockSpec((None, nseg, hps, M_HD, M_HD), c_out),
                   pl.BlockSpec((None, nseg, hps, 1, M_HD), st_out),
                   pl.BlockSpec((None, nseg, hps, 1, 128), st_out)],
        out_shape=[first_out[1],
                   jax.ShapeDtypeStruct((n_lyr_out, n_seq, M_HEADS, M_HD, M_HD), F32),
                   jax.ShapeDtypeStruct((1, n_seq, M_HEADS, 1, M_HD), F32),
                   jax.ShapeDtypeStruct((1, n_seq, M_HEADS, 1, 128), F32)],
        scratch_shapes=scratch,
        input_output_aliases=aliases,
        compiler_params=_cparams("parallel", "parallel", "arbitrary"),
        name="mlstm_chunk",
    )(*args)


def kernel(x_prompt, x_sample, cache_swa_k, cache_swa_v, cache_swa_meta_k, cache_swa_meta_v, state_mlstm_c, state_mlstm_n, state_mlstm_m, state_mlstm_conv, meta_tokens, rel_bias, norm_w, swa_w_in, swa_q_norm, swa_k_norm, swa_sinks, swa_w_out, mlstm_w_in, mlstm_conv_w, mlstm_conv_b, mlstm_wq, mlstm_wk, mlstm_wv, mlstm_w_gates, mlstm_b_gates, mlstm_head_norm, mlstm_skip, mlstm_w_out):
    B, T, D = x_prompt.shape
    DB, DT, _ = x_sample.shape
    depth = norm_w.shape[0]
    n_buf = cache_swa_k.shape[2]
    n_blk = T // ATT_BLOCK
    n_real, n_samp, n_meta = B * T, DB * DT, B * N_META
    n_small = -(-(n_samp + n_meta) // SMALL_ROW_TILE) * SMALL_ROW_TILE
    pad_small = n_small - n_samp - n_meta
    s_meta = slice(n_samp, n_samp + n_meta)

    def small_rows(samp, meta):
        return jnp.concatenate([samp, meta, jnp.zeros((pad_small,) + samp.shape[1:], samp.dtype)], axis=0)

    xr = x_prompt.reshape(n_real, D)
    xs = small_rows(x_sample.reshape(n_samp, D),
                    jnp.broadcast_to(meta_tokens.astype(x_prompt.dtype)[None], (B, N_META, D)).reshape(n_meta, D))

    ar = lambda n: jnp.arange(n, dtype=jnp.int32)
    meta_pos = ar(N_META)
    geo = []
    for blk in (0, 1):
        q_pos = N_META + blk * ATT_BLOCK + ar(ATT_BLOCK)
        band_pos = N_META + (blk - 1) * ATT_BLOCK + ar(2 * ATT_BLOCK)
        k_pos = jnp.concatenate([meta_pos, band_pos])
        k_valid = jnp.concatenate([jnp.ones((N_META,), bool), band_pos >= N_META])
        geo.append(_mask_and_bucket(q_pos, k_pos, k_valid))
    tk_p = N_META + 2 * ATT_BLOCK
    mask_m, buck_m = _mask_and_bucket(meta_pos, meta_pos, jnp.ones((N_META,), bool))
    buf_pos = PAST_LEN - n_buf + ar(n_buf)
    new_pos = PAST_LEN + ar(DT)
    k_pos_s = jnp.concatenate([meta_pos, buf_pos, new_pos])
    k_valid_s = jnp.concatenate([jnp.ones((N_META,), bool), buf_pos >= N_META, jnp.ones((DT,), bool)])
    mask_s, buck_s = _mask_and_bucket(new_pos, k_pos_s, k_valid_s)
    tk_s = N_META + n_buf + DT

    def place(a, rows):
        return jnp.pad(a, ((0, rows - a.shape[0]), (0, tk_p - a.shape[1])))

    idx_all = jnp.concatenate([geo[0][1], geo[1][1], place(buck_m, N_META), place(buck_s, 8)], axis=0)
    bias_all = _bias_table(idx_all, rel_bias.astype(F32))
    bias_p = jnp.transpose(bias_all[:, :2 * ATT_BLOCK].reshape(N_HEADS, 2, ATT_BLOCK, tk_p), (1, 0, 2, 3))
    bias_p = bias_p.reshape(2, N_KV, GROUP * ATT_BLOCK, tk_p)
    mask_p = jnp.stack([jnp.tile(geo[0][0], (GROUP, 1)), jnp.tile(geo[1][0], (GROUP, 1))]).astype(F32)
    key_major = lambda a: jnp.swapaxes(jnp.concatenate([a[..., N_META:], a[..., :N_META]], axis=-1), -1, -2)
    bias_pt, mask_pt = key_major(bias_p), key_major(mask_p)
    o0 = 2 * ATT_BLOCK
    seg_m, seg_s = (N_META,), (N_META, n_buf, DT)
    grp_s = SAMPLE_ATT_GROUP
    bias_m = _block_diag_keys(_bias_group_rows(bias_all[:, o0:o0 + N_META, :N_META], N_META), B, seg_m)
    mask_mg = _block_diag_keys(jnp.tile(mask_m, (GROUP, 1)).astype(F32), B, seg_m)
    o1 = o0 + N_META
    bias_s = _block_diag_keys(_bias_group_rows(bias_all[:, o1:o1 + DT, :tk_s], DT), grp_s, seg_s)
    mask_sg = _block_diag_keys(jnp.tile(mask_s, (GROUP, 1)).astype(F32), grp_s, seg_s)

    def sink_rows(sinks, t, n_seq=1):
        rows = jnp.repeat(sinks.astype(F32).reshape(N_KV, GROUP, 1), t, axis=2).reshape(N_KV, GROUP * t, 1)
        return jnp.tile(rows, (1, n_seq, 1))

    swa_p, swa_s, ml_p, ml_s = [], [], [], []
    c_samp = None
    assert n_samp % SMALL_CHUNK == 0 and SMALL_CHUNK % DT == 0
    for layer in range(depth):
        j = layer // 2
        nw = norm_w[layer].astype(F32).reshape(1, D)
        if layer % 2 == 0:
            w_in = swa_w_in[j].astype(BF16)
            w_out = swa_w_out[j].astype(BF16)
            qn = swa_q_norm[j].astype(F32).reshape(1, HEAD_DIM)
            kn = swa_k_norm[j].astype(F32).reshape(1, HEAD_DIM)
            qn_scaled = (qn * (HEAD_DIM ** -0.5)).reshape(HEAD_DIM, 1)
            qt_r, k_r, vt_r, gt_r = _swa_proj_t(xr, nw, w_in.T, qn_scaled, kn.reshape(HEAD_DIM, 1), PROJ_T_ROW_TILE)
            q_s, k_s, v_s, g_s = _swa_proj(xs, nw, w_in, qn, kn, SMALL_ROW_TILE)
            mk = k_s[s_meta].reshape(B, N_META, KV_W)
            mv = v_s[s_meta].reshape(B, N_META, KV_W)
            xr = _swa_prompt(qt_r, k_r, vt_r, gt_r, xr, mk, jnp.swapaxes(mv, 1, 2), bias_pt, mask_pt,
                             jnp.swapaxes(sink_rows(swa_sinks[j], ATT_BLOCK), 1, 2), w_out, B, n_blk)
            k_new = k_s[:n_samp]
            v_new = v_s[:n_samp]
            ck = cache_swa_k[j].astype(F32).reshape(DB * n_buf, KV_W)
            cv = cache_swa_v[j].astype(F32).reshape(DB * n_buf, KV_W)
            cmk = cache_swa_meta_k[j].astype(F32).reshape(DB * N_META, KV_W)
            cmv = cache_swa_meta_v[j].astype(F32).reshape(DB * N_META, KV_W)
            o_samp = _swa_small(_to_group_rows(q_s[:n_samp], DB, DT), [cmk, ck, k_new], [cmv, cv, v_new], seg_s,
                                bias_s, mask_sg, sink_rows(swa_sinks[j], DT, grp_s), grp_s, GROUP * DT)
            o_meta = _swa_small(_to_group_rows(q_s[s_meta], B, N_META), [k_s[s_meta]], [v_s[s_meta]], seg_m,
                                bias_m, mask_mg, sink_rows(swa_sinks[j], N_META, B), B, GROUP * N_META)
            o_small = small_rows(_from_group_rows(o_samp, DB, DT), _from_group_rows(o_meta, B, N_META))
            xs = _resid_matmul(o_small, w_out, xs, SMALL_ROW_TILE, gate=g_s)
            n_keep = min(WINDOW, T + N_META)
            last = lambda a: a.reshape(B, T, KV_W)[:, T - n_keep:].reshape(B, n_keep, N_KV, HEAD_DIM)
            v_last = jnp.stack([vt_r[:, (b + 1) * T - n_keep:(b + 1) * T].T for b in range(B)])
            swa_p.append((last(k_r), v_last.reshape(B, n_keep, N_KV, HEAD_DIM),
                          mk.reshape(B, N_META, N_KV, HEAD_DIM), mv.reshape(B, N_META, N_KV, HEAD_DIM)))
            swa_s.append((k_new.reshape(DB, DT, N_KV, HEAD_DIM), v_new.reshape(DB, DT, N_KV, HEAD_DIM)))
        else:
            w_in = mlstm_w_in[j].astype(BF16)
            w_out = mlstm_w_out[j].astype(BF16)
            conv_wts = (mlstm_conv_w[j].astype(F32), mlstm_conv_b[j].astype(F32).reshape(1, M_INNER))
            wts = (mlstm_wq[j].astype(BF16), mlstm_wk[j].astype(BF16), mlstm_wv[j].astype(BF16),
                   mlstm_w_gates[j].astype(BF16), mlstm_b_gates[j].astype(F32).reshape(1, 2 * M_HEADS))
            hn = mlstm_head_norm[j].astype(F32).reshape(1, M_INNER)
            sk = mlstm_skip[j].astype(F32).reshape(1, M_INNER)
            p_s = _norm_proj(xs, nw, w_in, SMALL_ROW_TILE, M_INNER)
            xm_samp = p_s[:n_samp, :M_INNER].astype(F32).reshape(DB, DT, M_INNER)
            xm_meta = p_s[s_meta, :M_INNER].astype(F32).reshape(B, N_META, M_INNER)
            xpad_s = jnp.concatenate([state_mlstm_conv[j].astype(F32), xm_samp], axis=1)
            xpad_m = jnp.concatenate([jnp.zeros((B, CONV_W - 1, M_INNER), F32), xm_meta], axis=1)
            shifted = [small_rows(xpad_s[:, w:w + DT].reshape(n_samp, M_INNER),
                                  xpad_m[:, w:w + N_META].reshape(n_meta, M_INNER)) for w in range(CONV_W)]
            q_s, k_s, v_s, gt_s, xc_s = _qkv_small(shifted, conv_wts, wts, SMALL_ROW_TILE)
            zo_s = p_s[:, M_INNER:]
            cs_real = jnp.pad(xpad_m[:, N_META:], ((0, 0), (8 - (CONV_W - 1), 0), (0, 0)))
            q_r, k_r, v_r, gt_r, xc_r, xm_tail = _mlstm_in(xr.reshape(B, T, D), nw, w_in, cs_real,
                                                           conv_wts, wts, ROW_TILE)
            zo_r = _norm_proj(xr, nw, w_in, 512, M_INNER, col0=M_INNER).reshape(B, T, 2 * M_INNER)

            to3 = lambda a, sl, b, t: a[sl].reshape(b, t, a.shape[-1])
            zc = jnp.zeros((1, B, M_HEADS, M_HD, M_HD), F32)
            zn = jnp.zeros((1, B, M_HEADS, 1, M_HD), F32)
            zm = jnp.zeros((1, B, M_HEADS, 1, 128), F32)
            u_m, c_m, n_m, m_m = _mlstm_chunk(
                to3(q_s, s_meta, B, N_META), to3(k_s, s_meta, B, N_META), to3(v_s, s_meta, B, N_META),
                to3(xc_s, s_meta, B, N_META), to3(zo_s, s_meta, B, N_META), to3(gt_s, s_meta, B, N_META),
                zc, zn, zm, 0, hn, sk, N_META, N_META, M_HEADS)
            x_new, c_r, n_r, m_r = _mlstm_chunk(q_r, k_r, v_r, xc_r, zo_r, gt_r, c_m, n_m, m_m, 0, hn, sk,
                                                PROMPT_CHUNK, PROMPT_CHUNK, M_HEADS,
                                                x=xr.reshape(B, T, D), w_out=w_out)
            s_samp = slice(0, n_samp)
            nb_s = n_samp // SMALL_CHUNK
            u_s, c_samp, n_s, m_s = _mlstm_chunk(
                to3(q_s, s_samp, nb_s, SMALL_CHUNK), to3(k_s, s_samp, nb_s, SMALL_CHUNK),
                to3(v_s, s_samp, nb_s, SMALL_CHUNK), to3(xc_s, s_samp, nb_s, SMALL_CHUNK),
                to3(zo_s, s_samp, nb_s, SMALL_CHUNK), to3(gt_s, s_samp, nb_s, SMALL_CHUNK),
                state_mlstm_c.astype(F32), state_mlstm_n.astype(F32)[:, :, :, None, :],
                jnp.broadcast_to(state_mlstm_m.astype(F32)[..., None, None], state_mlstm_m.shape + (1, 128)),
                j, hn, sk, SMALL_CHUNK, DT, 1, c_acc=c_samp, lyr_out=j, n_lyr_out=state_mlstm_c.shape[0])
            xr = x_new.reshape(n_real, D)
            u_small = small_rows(u_s.reshape(n_samp, M_INNER), u_m.reshape(n_meta, M_INNER))
            xs = _resid_matmul(u_small, w_out, xs, SMALL_ROW_TILE)
            ml_p.append((c_r[0], n_r.reshape(B, M_HEADS, M_HD), m_r[0, :, :, 0, 0],
                         xm_tail[:, 8 - (CONV_W - 1):]))
            ml_s.append((n_s.reshape(DB, M_HEADS, M_HD), m_s[0, :, :, 0, 0], xpad_s[:, DT:]))

    sd = state_mlstm_c.dtype
    stack = lambda rows, idx: jnp.stack([r[idx] for r in rows])
    window = lambda cache, idx: jnp.concatenate([cache[:, :, DT:].astype(F32), stack(swa_s, idx)], axis=2)
    return (xr.reshape(B, T, D), xs[:n_samp].reshape(DB, DT, D),
            stack(swa_p, 0), stack(swa_p, 1), stack(swa_p, 2), stack(swa_p, 3),
            stack(ml_p, 0).astype(sd), stack(ml_p, 1).astype(sd), stack(ml_p, 2).astype(sd), stack(ml_p, 3),
            window(cache_swa_k, 0), window(cache_swa_v, 1),
            c_samp.astype(sd), stack(ml_s, 0).astype(sd), stack(ml_s, 1).astype(sd), stack(ml_s, 2))
```

```python
import functools
import math

import jax
import jax.numpy as jnp
from jax import lax
from jax.experimental import pallas as pl
from jax.experimental.pallas import tpu as pltpu

F32 = jnp.float32
BF16 = jnp.bfloat16

D_MODEL = 1024
N_META = 16
EPS = 1e-6
NEG_INF = -1e30
N_HEADS = 16
HEAD_DIM = 64
N_KV = 4
GROUP = 4
ATT_W = N_HEADS * HEAD_DIM
KV_W = N_KV * HEAD_DIM
WINDOW = 128
ATT_BLOCK = 128
N_BUCKETS = 32
MAX_DISTANCE = 128
M_INNER = 2048
M_HEADS = 4
M_HD = 512
CONV_W = 4
PAST_LEN = 8192

VMEM_LIMIT_BYTES = 56 * 1024 * 1024
PROMPT_CHUNK = 256
SMALL_CHUNK = 16
ROW_TILE = 256
PROJ_T_ROW_TILE = 512
SMALL_ROW_TILE = 128
SAMPLE_ATT_GROUP = 8


def _cparams(*sem):
    return pltpu.CompilerParams(dimension_semantics=sem, vmem_limit_bytes=VMEM_LIMIT_BYTES)


def _rms(x, w):
    ms = jnp.mean(x * x, axis=-1, keepdims=True)
    return x * lax.rsqrt(ms + EPS) * w


def _sigmoid(x):
    return 1.0 / (1.0 + jnp.exp(-x))


def _silu(x):
    return x * _sigmoid(x)


def _dot(a, b):
    return jnp.dot(a, b, preferred_element_type=F32)


def _dot_nt(a, b):
    return lax.dot_general(a, b, (((1,), (1,)), ((), ())), preferred_element_type=F32)


def _dot_tn(a, b):
    return lax.dot_general(a, b, (((0,), (0,)), ((), ())), preferred_element_type=F32)


def _norm_proj_kernel(x_ref, nw_ref, w_ref, o_ref):
    h = _rms(x_ref[...], nw_ref[...]).astype(BF16)
    o_ref[...] = _dot(h, w_ref[...]).astype(o_ref.dtype)


def _norm_proj(x, nw, w, tm, tn, out_dtype):
    rows, d = x.shape
    n = w.shape[1]
    return pl.pallas_call(
        _norm_proj_kernel,
        grid=(n // tn, rows // tm),
        in_specs=[pl.BlockSpec((tm, d), lambda j, i: (i, 0)),
                  pl.BlockSpec((1, d), lambda j, i: (0, 0)),
                  pl.BlockSpec((d, tn), lambda j, i: (0, j))],
        out_specs=pl.BlockSpec((tm, tn), lambda j, i: (i, j)),
        out_shape=jax.ShapeDtypeStruct((rows, n), out_dtype),
        compiler_params=_cparams("parallel", "parallel"),
        name="norm_proj",
    )(x, nw, w)


def _swa_proj_kernel(x_ref, nw_ref, w_ref, qn_ref, kn_ref, q_ref, k_ref, v_ref, g_ref):
    h = _rms(x_ref[...], nw_ref[...]).astype(BF16)
    proj = _dot(h, w_ref[...])
    qn = qn_ref[...]
    kn = kn_ref[...]
    for hd in range(N_HEADS):
        sl = proj[:, hd * HEAD_DIM:(hd + 1) * HEAD_DIM]
        q_ref[:, hd * HEAD_DIM:(hd + 1) * HEAD_DIM] = _rms(sl, qn).astype(q_ref.dtype)
    for hd in range(N_KV):
        sl = proj[:, ATT_W + hd * HEAD_DIM:ATT_W + (hd + 1) * HEAD_DIM]
        k_ref[:, hd * HEAD_DIM:(hd + 1) * HEAD_DIM] = _rms(sl, kn)
    v_ref[...] = proj[:, ATT_W + KV_W:ATT_W + 2 * KV_W]
    g_ref[...] = _silu(proj[:, ATT_W + 2 * KV_W:])


def _swa_proj(x, nw, w, qn, kn, tm):
    rows, d = x.shape
    n = w.shape[1]
    row = lambda i: (i, 0)
    fixed = lambda i: (0, 0)
    return pl.pallas_call(
        _swa_proj_kernel,
        grid=(rows // tm,),
        in_specs=[pl.BlockSpec((tm, d), row), pl.BlockSpec((1, d), fixed), pl.BlockSpec((d, n), fixed),
                  pl.BlockSpec((1, HEAD_DIM), fixed), pl.BlockSpec((1, HEAD_DIM), fixed)],
        out_specs=[pl.BlockSpec((tm, ATT_W), row), pl.BlockSpec((tm, KV_W), row),
                   pl.BlockSpec((tm, KV_W), row), pl.BlockSpec((tm, ATT_W), row)],
        out_shape=[jax.ShapeDtypeStruct((rows, ATT_W), BF16), jax.ShapeDtypeStruct((rows, KV_W), F32),
                   jax.ShapeDtypeStruct((rows, KV_W), F32), jax.ShapeDtypeStruct((rows, ATT_W), F32)],
        compiler_params=_cparams("parallel"),
        name="swa_proj",
    )(x, nw, w, qn, kn)


def _swa_proj_t_kernel(x_ref, nw_ref, wt_ref, qn_ref, kn_ref, qt_ref, k_ref, vt_ref, gt_ref):
    h = _rms(x_ref[...], nw_ref[...]).astype(BF16)
    pt = _dot_nt(wt_ref[...], h)

    def head_norm(r0, gain):
        sl = pt[r0:r0 + HEAD_DIM]
        ms = jnp.mean(sl * sl, axis=0, keepdims=True)
        return sl * lax.rsqrt(ms + EPS) * gain

    qn = qn_ref[...]
    kn = kn_ref[...]
    for hd in range(N_HEADS):
        qt_ref[hd * HEAD_DIM:(hd + 1) * HEAD_DIM, :] = head_norm(hd * HEAD_DIM, qn).astype(qt_ref.dtype)
    kt = jnp.concatenate([head_norm(ATT_W + hd * HEAD_DIM, kn) for hd in range(N_KV)], axis=0)
    k_ref[...] = kt.T
    vt_ref[...] = pt[ATT_W + KV_W:ATT_W + 2 * KV_W]
    gt_ref[...] = _silu(pt[ATT_W + 2 * KV_W:])


def _swa_proj_t(x, nw, wt, qn_col, kn_col, tm):
    rows, d = x.shape
    nt = wt.shape[0]
    row = lambda i: (i, 0)
    col = lambda i: (0, i)
    fixed = lambda i: (0, 0)
    return pl.pallas_call(
        _swa_proj_t_kernel,
        grid=(rows // tm,),
        in_specs=[pl.BlockSpec((tm, d), row), pl.BlockSpec((1, d), fixed), pl.BlockSpec((nt, d), fixed),
                  pl.BlockSpec((HEAD_DIM, 1), fixed), pl.BlockSpec((HEAD_DIM, 1), fixed)],
        out_specs=[pl.BlockSpec((ATT_W, tm), col), pl.BlockSpec((tm, KV_W), row),
                   pl.BlockSpec((KV_W, tm), col), pl.BlockSpec((ATT_W, tm), col)],
        out_shape=[jax.ShapeDtypeStruct((ATT_W, rows), BF16), jax.ShapeDtypeStruct((rows, KV_W), F32),
                   jax.ShapeDtypeStruct((KV_W, rows), F32), jax.ShapeDtypeStruct((ATT_W, rows), F32)],
        compiler_params=_cparams("parallel"),
        name="swa_proj_t",
    )(x, nw, wt, qn_col, kn_col)


def _resid_matmul_kernel(u_ref, w_ref, x_ref, o_ref):
    o_ref[...] = x_ref[...] + _dot(u_ref[...], w_ref[...])


def _gated_resid_matmul_kernel(a_ref, g_ref, w_ref, x_ref, o_ref):
    u = (a_ref[...] * g_ref[...]).astype(BF16)
    o_ref[...] = x_ref[...] + _dot(u, w_ref[...])


def _resid_matmul(u, w, x, tm, gate=None):
    rows, k = u.shape
    d = w.shape[1]
    row = lambda i: (i, 0)
    ins = [u] if gate is None else [u, gate]
    body = _resid_matmul_kernel if gate is None else _gated_resid_matmul_kernel
    return pl.pallas_call(
        body,
        grid=(rows // tm,),
        in_specs=[pl.BlockSpec((tm, k), row)] * len(ins)
                 + [pl.BlockSpec((k, d), lambda i: (0, 0)), pl.BlockSpec((tm, d), row)],
        out_specs=pl.BlockSpec((tm, d), row),
        out_shape=jax.ShapeDtypeStruct((rows, d), F32),
        compiler_params=_cparams("parallel"),
        name="resid_matmul",
    )(*ins, w, x)


def _bias_table_kernel(idx_ref, tab_ref, o_ref):
    idx = idx_ref[...]
    for hd in range(N_HEADS):
        acc = jnp.zeros(idx.shape, F32)
        for b in range(N_BUCKETS):
            acc = jnp.where(idx == b, tab_ref[b, hd], acc)
        o_ref[hd] = acc


def _bias_table(idx, rel_bias):
    r, c = idx.shape
    return pl.pallas_call(
        _bias_table_kernel,
        in_specs=[pl.BlockSpec((r, c), lambda: (0, 0)), pl.BlockSpec(memory_space=pltpu.SMEM)],
        out_specs=pl.BlockSpec((N_HEADS, r, c), lambda: (0, 0, 0)),
        out_shape=jax.ShapeDtypeStruct((N_HEADS, r, c), F32),
        name="bias_table",
    )(idx, rel_bias)


def _rel_bucket(dist):
    n = jnp.maximum(dist, 0)
    max_exact = N_BUCKETS // 2
    nf = jnp.maximum(n, 1).astype(F32)
    large = max_exact + (jnp.log(nf / max_exact) / math.log(MAX_DISTANCE / max_exact)
                         * (N_BUCKETS - max_exact)).astype(jnp.int32)
    large = jnp.minimum(large, N_BUCKETS - 1)
    return jnp.where(n < max_exact, n, large)


def _mask_and_bucket(q_pos, k_pos, k_valid):
    dist = q_pos[:, None] - k_pos[None, :]
    mask = k_valid[None, :] & (dist >= 0) & ((dist <= WINDOW) | (k_pos[None, :] < N_META))
    return mask, _rel_bucket(dist)


def _swa_prompt_kernel(qt_ref, kc_ref, kp_ref, vtc_ref, vtp_ref, mk_ref, mvt_ref, gt_ref, x_ref,
                       bias_ref, mask_ref, sink_ref, w_ref, o_ref, att_ref):
    kband = jnp.concatenate([kp_ref[...], kc_ref[...], mk_ref[...]], axis=0).astype(BF16)
    vtband = jnp.concatenate([vtp_ref[...], vtc_ref[...], mvt_ref[...]], axis=1).astype(BF16)
    qt = qt_ref[...]
    keep = mask_ref[...] > 0.5
    hp = GROUP
    for kv in range(N_KV):
        kk = kband[:, kv * HEAD_DIM:(kv + 1) * HEAD_DIM]
        vv = vtband[kv * HEAD_DIM:(kv + 1) * HEAD_DIM]
        for g0 in range(0, GROUP, hp):
            lanes = slice(g0 * ATT_BLOCK, (g0 + hp) * ATT_BLOCK)
            qtg = jnp.concatenate(
                [qt[(kv * GROUP + g) * HEAD_DIM:(kv * GROUP + g + 1) * HEAD_DIM] for g in range(g0, g0 + hp)], axis=1)
            logits = _dot(kk, qtg) + bias_ref[kv, :, lanes]
            logits = jnp.where(keep[:, lanes], logits, NEG_INF)
            sk = sink_ref[kv, :, lanes]
            m = jnp.maximum(jnp.max(logits, axis=0, keepdims=True), sk)
            p = jnp.exp(logits - m)
            den = jnp.sum(p, axis=0, keepdims=True) + jnp.exp(sk - m)
            otg = _dot(vv, p.astype(BF16)) * (1.0 / den)
            for g in range(hp):
                hd = kv * GROUP + g0 + g
                att_ref[hd * HEAD_DIM:(hd + 1) * HEAD_DIM, :] = otg[:, g * ATT_BLOCK:(g + 1) * ATT_BLOCK]
    ut = (att_ref[...] * gt_ref[...]).astype(BF16)
    o_ref[...] = x_ref[...] + _dot_tn(ut, w_ref[...])


def _swa_prompt(qt, kn, vt, sgt, x, mk, mvt, bias_t, mask_t, sink_t, w_out, batch, n_blk):
    cur = lambda b, i: (b * n_blk + i, 0)
    prev = lambda b, i: (b * n_blk + jnp.maximum(i - 1, 0), 0)
    cur_t = lambda b, i: (0, b * n_blk + i)
    prev_t = lambda b, i: (0, b * n_blk + jnp.maximum(i - 1, 0))
    meta = lambda b, i: (b, 0, 0)
    tk = N_META + 2 * ATT_BLOCK
    nq = GROUP * ATT_BLOCK
    rows = kn.shape[0]
    return pl.pallas_call(
        _swa_prompt_kernel,
        grid=(batch, n_blk),
        in_specs=[pl.BlockSpec((ATT_W, ATT_BLOCK), cur_t),
                  pl.BlockSpec((ATT_BLOCK, KV_W), cur), pl.BlockSpec((ATT_BLOCK, KV_W), prev),
                  pl.BlockSpec((KV_W, ATT_BLOCK), cur_t), pl.BlockSpec((KV_W, ATT_BLOCK), prev_t),
                  pl.BlockSpec((None, N_META, KV_W), meta), pl.BlockSpec((None, KV_W, N_META), meta),
                  pl.BlockSpec((ATT_W, ATT_BLOCK), cur_t), pl.BlockSpec((ATT_BLOCK, D_MODEL), cur),
                  pl.BlockSpec((None, N_KV, tk, nq), lambda b, i: (jnp.minimum(i, 1), 0, 0, 0)),
                  pl.BlockSpec((None, tk, nq), lambda b, i: (jnp.minimum(i, 1), 0, 0)),
                  pl.BlockSpec((N_KV, 1, nq), lambda b, i: (0, 0, 0)),
                  pl.BlockSpec((ATT_W, D_MODEL), lambda b, i: (0, 0))],
        out_specs=pl.BlockSpec((ATT_BLOCK, D_MODEL), cur),
        out_shape=jax.ShapeDtypeStruct((rows, D_MODEL), F32),
        scratch_shapes=[pltpu.VMEM((ATT_W, ATT_BLOCK), F32)],
        compiler_params=_cparams("parallel", "parallel"),
        name="swa_prompt",
    )(qt, kn, kn, vt, vt, mk, mvt, sgt, x, bias_t, mask_t, sink_t, w_out)


def _swa_small_kernel(n_seg, *refs):
    q_ref = refs[0]
    k_refs = refs[1:1 + n_seg]
    v_refs = refs[1 + n_seg:1 + 2 * n_seg]
    bias_ref, mask_ref, sink_ref, o_ref = refs[1 + 2 * n_seg:]
    keep = mask_ref[...] > 0.5
    scale = HEAD_DIM ** -0.5
    kall = jnp.concatenate([r[...] for r in k_refs], axis=0).astype(BF16) if n_seg > 1 else k_refs[0][...].astype(BF16)
    vall = jnp.concatenate([r[...] for r in v_refs], axis=0).astype(BF16) if n_seg > 1 else v_refs[0][...].astype(BF16)
    for kv in range(N_KV):
        kk = kall[:, kv * HEAD_DIM:(kv + 1) * HEAD_DIM]
        vv = vall[:, kv * HEAD_DIM:(kv + 1) * HEAD_DIM]
        logits = _dot_nt(q_ref[kv], kk) * scale + bias_ref[kv]
        logits = jnp.where(keep, logits, NEG_INF)
        sk = sink_ref[kv]
        m = jnp.maximum(jnp.max(logits, axis=-1, keepdims=True), sk)
        p = jnp.exp(logits - m)
        den = jnp.sum(p, axis=-1, keepdims=True) + jnp.exp(sk - m)
        o_ref[kv] = _dot(p.astype(BF16), vv) / den


def _swa_small(q, k_segs, v_segs, seg_len, seg_first, bias, mask, sink, group, rq):
    rows = q.shape[1]
    n_seg = len(k_segs)
    tk = group * sum(seg_len)
    seg_specs = [pl.BlockSpec((group * n, KV_W), functools.partial(lambda o, i: (o + i, 0), f // group))
                 for n, f in zip(seg_len, seg_first)]
    return pl.pallas_call(
        functools.partial(_swa_small_kernel, n_seg),
        grid=(rows // (group * rq),),
        in_specs=[pl.BlockSpec((N_KV, group * rq, HEAD_DIM), lambda i: (0, i, 0))] + seg_specs + seg_specs
                 + [pl.BlockSpec((N_KV, group * rq, tk), lambda i: (0, 0, 0)),
                    pl.BlockSpec((group * rq, tk), lambda i: (0, 0)),
                    pl.BlockSpec((N_KV, group * rq, 1), lambda i: (0, 0, 0))],
        out_specs=pl.BlockSpec((N_KV, group * rq, HEAD_DIM), lambda i: (0, i, 0)),
        out_shape=jax.ShapeDtypeStruct((N_KV, rows, HEAD_DIM), F32),
        compiler_params=_cparams("parallel"),
        name="swa_small",
    )(q, *k_segs, *v_segs, bias, mask, sink)


def _to_group_rows(a, nb, t):
    a = a.reshape(nb, t, N_KV, GROUP, HEAD_DIM)
    return jnp.transpose(a, (2, 0, 3, 1, 4)).reshape(N_KV, nb * GROUP * t, HEAD_DIM)


def _from_group_rows(a, nb, t):
    a = a.reshape(N_KV, nb, GROUP, t, HEAD_DIM)
    return jnp.transpose(a, (1, 3, 0, 2, 4)).reshape(nb * t, ATT_W)


def _bias_group_rows(b, t):
    return b.reshape(N_KV, GROUP * t, b.shape[-1])


def _block_diag_keys(a, group, seg_len):
    eye = jnp.eye(group, dtype=a.dtype)
    out, o = [], 0
    for n in seg_len:
        blk = a[..., o:o + n]
        o += n
        big = eye[:, None, :, None] * blk[..., None, :, None, :]
        out.append(big.reshape(a.shape[:-2] + (group * a.shape[-2], group * n)))
    return jnp.concatenate(out, axis=-1)


def _conv_silu(xs, cw_ref, cb_ref):
    cw = cw_ref[...]
    acc = xs[0] * cw[0:1]
    for w in range(1, CONV_W):
        acc = acc + xs[w] * cw[w:w + 1]
    return _silu(acc + cb_ref[...])


def _mlstm_in_kernel(x_ref, nw_ref, w_ref, cs_ref, cw_ref, cb_ref, wq_ref, wk_ref, wv_ref, wg_ref, bg_ref,
                     q_ref, k_ref, v_ref, gt_ref, xc_ref, zo_ref, tail_ref):
    i = pl.program_id(1)
    h = _rms(x_ref[...], nw_ref[...]).astype(BF16)
    zo_ref[...] = _dot(h, w_ref[:, M_INNER:]).astype(zo_ref.dtype)
    xm = _dot(h, w_ref[:, :M_INNER])
    tm = xm.shape[0]
    halo = jnp.where(i == 0, cs_ref[...], tail_ref[...])
    full = jnp.concatenate([halo, xm], axis=0)
    n = tm + 8
    xs = [pltpu.roll(full, n - (8 - (CONV_W - 1) + w), 0)[:tm] for w in range(CONV_W - 1)] + [xm]
    xcb = _conv_silu(xs, cw_ref, cb_ref).astype(BF16)
    xc_ref[...] = xcb
    tail_ref[...] = xm[tm - 8:]
    _qkv_matmuls(xcb, xm.astype(BF16), wq_ref, wk_ref, wv_ref, wg_ref, bg_ref, q_ref, k_ref, v_ref, gt_ref)


def _mlstm_in(x3, nw, w, cs, conv_wts, wts, tm):
    b, t, d = x3.shape
    cur = lambda bi, i: (bi, i, 0)
    seq = lambda bi, i: (bi, 0, 0)
    fixed = lambda bi, i: (0, 0)
    once = pl.Buffered(1)
    tok = pl.BlockSpec((None, tm, M_INNER), cur)
    return pl.pallas_call(
        _mlstm_in_kernel,
        grid=(b, t // tm),
        in_specs=[pl.BlockSpec((None, tm, d), cur), pl.BlockSpec((1, d), fixed),
                  pl.BlockSpec((d, 3 * M_INNER), fixed, pipeline_mode=once),
                  pl.BlockSpec((None, 8, M_INNER), seq),
                  pl.BlockSpec((CONV_W, M_INNER), fixed), pl.BlockSpec((1, M_INNER), fixed)]
                 + _qkv_weight_specs(fixed, lambda bi, i: (0, 0, 0), once),
        out_specs=[tok, tok, tok, pl.BlockSpec((None, tm, 2 * M_HEADS), cur), tok,
                   pl.BlockSpec((None, tm, 2 * M_INNER), cur), pl.BlockSpec((None, 8, M_INNER), seq)],
        out_shape=[jax.ShapeDtypeStruct((b, t, M_INNER), BF16)] * 3
                  + [jax.ShapeDtypeStruct((b, t, 2 * M_HEADS), F32), jax.ShapeDtypeStruct((b, t, M_INNER), BF16),
                     jax.ShapeDtypeStruct((b, t, 2 * M_INNER), BF16), jax.ShapeDtypeStruct((b, 8, M_INNER), F32)],
        compiler_params=_cparams("parallel", "arbitrary"),
        name="mlstm_in",
    )(x3, nw, w, cs, *conv_wts, *wts)


def _qkv_matmuls(xcb, xmb, wq_ref, wk_ref, wv_ref, wg_ref, bg_ref, q_ref, k_ref, v_ref, gt_ref):
    gates = bg_ref[...]
    for h in range(M_HEADS):
        sl = slice(h * M_HD, (h + 1) * M_HD)
        qh = _dot(xcb[:, sl], wq_ref[h]).astype(BF16)
        kh = (_dot(xcb[:, sl], wk_ref[h]) * (M_HD ** -0.5)).astype(BF16)
        vh = _dot(xmb[:, sl], wv_ref[h]).astype(BF16)
        q_ref[:, sl] = qh
        k_ref[:, sl] = kh
        v_ref[:, sl] = vh
        gates = (gates + _dot(qh, wg_ref[h * M_HD:(h + 1) * M_HD])
                 + _dot(kh, wg_ref[M_INNER + h * M_HD:M_INNER + (h + 1) * M_HD])
                 + _dot(vh, wg_ref[2 * M_INNER + h * M_HD:2 * M_INNER + (h + 1) * M_HD]))
    lane = lax.broadcasted_iota(jnp.int32, gates.shape, 1)
    log_f = jnp.minimum(gates, 0.0) - jnp.log1p(jnp.exp(-jnp.abs(gates)))
    gt_ref[...] = jnp.where(lane < M_HEADS, gates, log_f)


def _qkv_shift_kernel(x0_ref, x1_ref, x2_ref, x3_ref, cw_ref, cb_ref, *rest):
    xs = [x0_ref[...], x1_ref[...], x2_ref[...], x3_ref[...]]
    xc = _conv_silu(xs, cw_ref, cb_ref)
    xc_ref = rest[-1]
    xc_ref[...] = xc
    _qkv_matmuls(xc.astype(BF16), xs[CONV_W - 1].astype(BF16), *rest[:-1])


def _qkv_weight_specs(fixed2, fixed3, mode=None):
    return [pl.BlockSpec((M_HEADS, M_HD, M_HD), fixed3, pipeline_mode=mode),
            pl.BlockSpec((M_HEADS, M_HD, M_HD), fixed3, pipeline_mode=mode),
            pl.BlockSpec((M_HEADS, M_HD, M_HD), fixed3, pipeline_mode=mode),
            pl.BlockSpec((3 * M_INNER, 2 * M_HEADS), fixed2, pipeline_mode=mode),
            pl.BlockSpec((1, 2 * M_HEADS), fixed2)]


def _qkv_small(xs, conv_wts, wts, tm):
    rows = xs[0].shape[0]
    row = lambda i: (i, 0)
    fixed = lambda i: (0, 0)
    outs = ([jax.ShapeDtypeStruct((rows, M_INNER), BF16)] * 3 + [jax.ShapeDtypeStruct((rows, 2 * M_HEADS), F32)]
            + [jax.ShapeDtypeStruct((rows, M_INNER), F32)])
    return pl.pallas_call(
        _qkv_shift_kernel,
        grid=(rows // tm,),
        in_specs=[pl.BlockSpec((tm, M_INNER), row)] * CONV_W
                 + [pl.BlockSpec((CONV_W, M_INNER), fixed), pl.BlockSpec((1, M_INNER), fixed)]
                 + _qkv_weight_specs(fixed, lambda i: (0, 0, 0)),
        out_specs=[pl.BlockSpec((tm, M_INNER), row)] * 3 + [pl.BlockSpec((tm, 2 * M_HEADS), row)]
                  + [pl.BlockSpec((tm, M_INNER), row)],
        out_shape=outs,
        compiler_params=_cparams("parallel"),
        name="mlstm_qkv_small",
    )(*xs, *conv_wts, *wts)


def _chunk_kernel(seg, hps, aliased, fused, q_ref, k_ref, v_ref, xc_ref, z_ref, op_ref, gc_ref, gr_ref, c0_ref, n0_ref,
                  m0_ref, hn_ref, sk_ref, *rest):
    if fused:
        x_ref, w_ref = rest[:2]
        rest = rest[2:]
    if aliased:
        rest = rest[1:]
    out_ref, c_ref, n_ref, m_ref = rest[:4]
    u_ref = rest[4] if fused else out_ref
    h_blk = pl.program_id(1)
    c_idx = pl.program_id(2)
    L = q_ref.shape[0]
    nseg = L // seg
    hi = lax.Precision.HIGHEST

    @pl.when(c_idx == 0)
    def _():
        c_ref[...] = c0_ref[...]
        n_ref[...] = n0_ref[...]
        m_ref[...] = m0_ref[...]

    gc = gc_ref[...]
    gr = gr_ref[...]
    col_c = lax.broadcasted_iota(jnp.int32, gc.shape, 1)
    row_r = lax.broadcasted_iota(jnp.int32, gr.shape, 0)
    t_i = lax.broadcasted_iota(jnp.int32, (L, L), 0)
    j_i = lax.broadcasted_iota(jnp.int32, (L, L), 1)
    causal = j_i <= t_i
    upper = t_i <= j_i
    if nseg > 1:
        sid_c = lax.broadcasted_iota(jnp.int32, (L, 1), 0) // seg
        in_seg = [sid_c == s for s in range(nseg)]
        pick = lambda vals: sum(jnp.where(in_seg[s], vals[s], 0.0) for s in range(nseg))
        same = (t_i // seg) == (j_i // seg)
        causal = causal & same
        upper = upper & same
    tril = jnp.where(causal, 1.0, 0.0).astype(F32)
    triu = jnp.where(upper, 1.0, 0.0).astype(F32)
    lf_c = jnp.where(col_c >= M_HEADS, gc, 0.0)
    lf_r = jnp.where(row_r >= M_HEADS, gr, 0.0)
    b_all_c = jnp.dot(tril, lf_c, precision=hi, preferred_element_type=F32)
    b_all_r = jnp.dot(lf_r, triu, precision=hi, preferred_element_type=F32)
    if nseg > 1:
        g_all = jnp.dot(jnp.where(same, 1.0, 0.0).astype(F32), lf_c, precision=hi, preferred_element_type=F32)

    for hh in range(hps):
        h_idx = h_blk * hps + hh
        cols = slice(hh * M_HD, (hh + 1) * M_HD)
        sel_c = lambda a, c: jnp.sum(jnp.where(col_c == c, a, 0.0), axis=1, keepdims=True)
        sel_r = lambda a, r: jnp.sum(jnp.where(row_r == r, a, 0.0), axis=0, keepdims=True)
        i_c, i_r = sel_c(gc, h_idx), sel_r(gr, h_idx)
        b_c, b_r = sel_c(b_all_c, M_HEADS + h_idx), sel_r(b_all_r, M_HEADS + h_idx)
        if nseg > 1:
            g = sel_c(g_all, M_HEADS + h_idx)
            m_old = [m_ref[s, hh][:, 0:1] for s in range(nseg)]
            m_prev = pick(m_old)
        else:
            g = b_c[L - 1:L, :]
            m_prev = m_ref[0, hh][:, 0:1]

        log_d = jnp.where(causal, b_c + (i_r - b_r), -jnp.inf)
        inter = b_c + m_prev
        m_t = jnp.maximum(inter, jnp.max(log_d, axis=-1, keepdims=True))
        w_intra = jnp.exp(log_d - m_t)
        w_inter = jnp.exp(inter - m_t)
        q = q_ref[:, cols]
        k = k_ref[:, cols]
        v = v_ref[:, cols]
        s_mat = _dot_nt(q, k) * w_intra
        qf = q.astype(F32)
        if nseg > 1:
            q_c = pick([_dot(q, c_ref[s, hh].astype(BF16)) for s in range(nseg)])
            q_n = pick([jnp.sum(qf * n_ref[s, hh], axis=-1, keepdims=True) for s in range(nseg)])
        else:
            q_c = _dot(q, c_ref[0, hh].astype(BF16))
            q_n = jnp.sum(qf * n_ref[0, hh], axis=-1, keepdims=True)
        num = _dot(s_mat.astype(BF16), v) + w_inter * q_c
        den = jnp.sum(s_mat, axis=-1, keepdims=True) + w_inter * q_n
        h = num * (1.0 / jnp.maximum(jnp.abs(den), jnp.exp(-m_t)))

        lw = (g - b_c) + i_c
        kf = k.astype(F32)
        if nseg > 1:
            g_s = [jnp.max(jnp.where(in_seg[s], g, -jnp.inf), axis=0, keepdims=True) for s in range(nseg)]
            m_new = [jnp.maximum(g_s[s] + m_old[s],
                                 jnp.max(jnp.where(in_seg[s], lw, -jnp.inf), axis=0, keepdims=True))
                     for s in range(nseg)]
            kw = kf * jnp.exp(lw - pick(m_new))
            for s in range(nseg):
                decay = jnp.exp(g_s[s] + m_old[s] - m_new[s])
                kw_s = jnp.where(in_seg[s], kw, 0.0)
                c_ref[s, hh] = decay * c_ref[s, hh] + _dot_tn(kw_s.astype(BF16), v)
                n_ref[s, hh] = decay * n_ref[s, hh] + jnp.sum(kw_s, axis=0, keepdims=True)
                m_ref[s, hh] = jnp.broadcast_to(m_new[s], m_ref.shape[2:])
        else:
            m_new = jnp.maximum(g + m_prev, jnp.max(lw, axis=0, keepdims=True))
            decay = jnp.exp(g + m_prev - m_new)
            kw = kf * jnp.exp(lw - m_new)
            c_ref[0, hh] = decay * c_ref[0, hh] + _dot_tn(kw.astype(BF16), v)
            n_ref[0, hh] = decay * n_ref[0, hh] + jnp.sum(kw, axis=0, keepdims=True)
            m_ref[0, hh] = jnp.broadcast_to(m_new, m_ref.shape[2:])

        mu = jnp.mean(h, axis=-1, keepdims=True)
        hc = h - mu
        var = jnp.mean(hc * hc, axis=-1, keepdims=True)
        h_out = _sigmoid(op_ref[:, cols].astype(F32)) * (hc * lax.rsqrt(var + EPS) * hn_ref[:, cols])
        u = (h_out + sk_ref[:, cols] * xc_ref[:, cols].astype(F32)) * _silu(z_ref[:, cols].astype(F32))
        u_ref[:, cols] = u.astype(u_ref.dtype)
    if fused:
        out_ref[...] = x_ref[...] + _dot(u_ref[...], w_ref[...])


def _mlstm_chunk(q, k, v, xc, zo, gates, c0, n0, m0, lyr_in, head_norm, skip, chunk, seg, hps,
                 c_acc=None, lyr_out=0, n_lyr_out=1, x=None, w_out=None):
    nb, t, _ = q.shape
    nc = t // chunk
    nseg = chunk // seg
    assert nseg == 1 or nc == 1
    n_seq = nb * nseg
    n_hblk = M_HEADS // hps
    w = hps * M_HD
    gates_t = jnp.swapaxes(gates, 1, 2)
    tok = lambda bi, h, c: (bi, c, h)
    hd_blk = pl.BlockSpec((None, chunk, w), tok)
    st_in = lambda bi, h, c: (lyr_in, bi, h, 0, 0)
    st_out = lambda bi, h, c: (0, bi, h, 0, 0)
    c_out = lambda bi, h, c: (lyr_out, bi, h, 0, 0)
    head_row = pl.BlockSpec((1, w), lambda bi, h, c: (0, h))
    in_specs = [hd_blk, hd_blk, hd_blk, hd_blk,
                hd_blk,
                pl.BlockSpec((None, chunk, w), lambda bi, h, c: (bi, c, n_hblk + h)),
                pl.BlockSpec((None, chunk, 2 * M_HEADS), lambda bi, h, c: (bi, c, 0)),
                pl.BlockSpec((None, 2 * M_HEADS, chunk), lambda bi, h, c: (bi, 0, c)),
                pl.BlockSpec((None, nseg, hps, M_HD, M_HD), st_in),
                pl.BlockSpec((None, nseg, hps, 1, M_HD), st_in),
                pl.BlockSpec((None, nseg, hps, 1, 128), st_in),
                head_row, head_row]
    args = [q, k, v, xc, zo, zo, gates, gates_t, c0, n0, m0, head_norm, skip]
    fused = x is not None
    first_out = (hd_blk, jax.ShapeDtypeStruct((nb, t, M_INNER), BF16))
    scratch = []
    if fused:
        assert hps == M_HEADS
        d = w_out.shape[1]
        x_blk = pl.BlockSpec((None, chunk, d), lambda bi, h, c: (bi, c, 0))
        in_specs += [x_blk, pl.BlockSpec((M_INNER, d), lambda bi, h, c: (0, 0))]
        args += [x, w_out]
        first_out = (x_blk, jax.ShapeDtypeStruct((nb, t, d), F32))
        scratch = [pltpu.VMEM((chunk, M_INNER), BF16)]
    aliases = {}
    if c_acc is not None:
        in_specs.append(pl.BlockSpec(memory_space=pl.ANY))
        aliases = {len(args): 1}
        args.append(c_acc)
    return pl.pallas_call(
        functools.partial(_chunk_kernel, seg, hps, c_acc is not None, fused),
        grid=(nb, n_hblk, nc),
        in_specs=in_specs,
        out_specs=[first_out[0],
                   pl.BlockSpec((None, nseg, hps, M_HD, M_HD), c_out),
                   pl.BlockSpec((None, nseg, hps, 1, M_HD), st_out),
                   pl.BlockSpec((None, nseg, hps, 1, 128), st_out)],
        out_shape=[first_out[1],
                   jax.ShapeDtypeStruct((n_lyr_out, n_seq, M_HEADS, M_HD, M_HD), F32),
                   jax.ShapeDtypeStruct((1, n_seq, M_HEADS, 1, M_HD), F32),
                   jax.ShapeDtypeStruct((1, n_seq, M_HEADS, 1, 128), F32)],
        scratch_shapes=scratch,
        input_output_aliases=aliases,
        compiler_params=_cparams("parallel", "parallel", "arbitrary"),
        name="mlstm_chunk",
    )(*args)


def kernel(x_prompt, x_sample, cache_swa_k, cache_swa_v, cache_swa_meta_k, cache_swa_meta_v, state_mlstm_c, state_mlstm_n, state_mlstm_m, state_mlstm_conv, meta_tokens, rel_bias, norm_w, swa_w_in, swa_q_norm, swa_k_norm, swa_sinks, swa_w_out, mlstm_w_in, mlstm_conv_w, mlstm_conv_b, mlstm_wq, mlstm_wk, mlstm_wv, mlstm_w_gates, mlstm_b_gates, mlstm_head_norm, mlstm_skip, mlstm_w_out):
    B, T, D = x_prompt.shape
    DB, DT, _ = x_sample.shape
    depth = norm_w.shape[0]
    n_buf = cache_swa_k.shape[2]
    n_blk = T // ATT_BLOCK
    n_real, n_samp, n_meta = B * T, DB * DT, B * N_META
    n_small = -(-(n_samp + n_meta) // SMALL_ROW_TILE) * SMALL_ROW_TILE
    pad_small = n_small - n_samp - n_meta
    s_meta = slice(n_samp, n_samp + n_meta)

    def small_rows(samp, meta):
        return jnp.concatenate([samp, meta, jnp.zeros((pad_small,) + samp.shape[1:], samp.dtype)], axis=0)

    xr = x_prompt.reshape(n_real, D)
    xs = small_rows(x_sample.reshape(n_samp, D),
                    jnp.broadcast_to(meta_tokens.astype(x_prompt.dtype)[None], (B, N_META, D)).reshape(n_meta, D))

    ar = lambda n: jnp.arange(n, dtype=jnp.int32)
    meta_pos = ar(N_META)
    geo = []
    for blk in (0, 1):
        q_pos = N_META + blk * ATT_BLOCK + ar(ATT_BLOCK)
        band_pos = N_META + (blk - 1) * ATT_BLOCK + ar(2 * ATT_BLOCK)
        k_pos = jnp.concatenate([meta_pos, band_pos])
        k_valid = jnp.concatenate([jnp.ones((N_META,), bool), band_pos >= N_META])
        geo.append(_mask_and_bucket(q_pos, k_pos, k_valid))
    tk_p = N_META + 2 * ATT_BLOCK
    mask_m, buck_m = _mask_and_bucket(meta_pos, meta_pos, jnp.ones((N_META,), bool))
    buf_pos = PAST_LEN - n_buf + ar(n_buf)
    new_pos = PAST_LEN + ar(DT)
    k_pos_s = jnp.concatenate([meta_pos, buf_pos, new_pos])
    k_valid_s = jnp.concatenate([jnp.ones((N_META,), bool), buf_pos >= N_META, jnp.ones((DT,), bool)])
    mask_s, buck_s = _mask_and_bucket(new_pos, k_pos_s, k_valid_s)
    tk_s = N_META + n_buf + DT

    def place(a, rows):
        return jnp.pad(a, ((0, rows - a.shape[0]), (0, tk_p - a.shape[1])))

    idx_all = jnp.concatenate([geo[0][1], geo[1][1], place(buck_m, N_META), place(buck_s, 8)], axis=0)
    bias_all = _bias_table(idx_all, rel_bias.astype(F32))
    bias_p = jnp.transpose(bias_all[:, :2 * ATT_BLOCK].reshape(N_HEADS, 2, ATT_BLOCK, tk_p), (1, 0, 2, 3))
    bias_p = bias_p.reshape(2, N_KV, GROUP * ATT_BLOCK, tk_p)
    mask_p = jnp.stack([jnp.tile(geo[0][0], (GROUP, 1)), jnp.tile(geo[1][0], (GROUP, 1))]).astype(F32)
    key_major = lambda a: jnp.swapaxes(jnp.concatenate([a[..., N_META:], a[..., :N_META]], axis=-1), -1, -2)
    bias_pt, mask_pt = key_major(bias_p), key_major(mask_p)
    o0 = 2 * ATT_BLOCK
    seg_m, seg_s = (N_META,), (N_META, n_buf, DT)
    grp_s = SAMPLE_ATT_GROUP
    bias_m = _block_diag_keys(_bias_group_rows(bias_all[:, o0:o0 + N_META, :N_META], N_META), B, seg_m)
    mask_mg = _block_diag_keys(jnp.tile(mask_m, (GROUP, 1)).astype(F32), B, seg_m)
    o1 = o0 + N_META
    bias_s = _block_diag_keys(_bias_group_rows(bias_all[:, o1:o1 + DT, :tk_s], DT), grp_s, seg_s)
    mask_sg = _block_diag_keys(jnp.tile(mask_s, (GROUP, 1)).astype(F32), grp_s, seg_s)

    def sink_rows(sinks, t, n_seq=1):
        rows = jnp.repeat(sinks.astype(F32).reshape(N_KV, GROUP, 1), t, axis=2).reshape(N_KV, GROUP * t, 1)
        return jnp.tile(rows, (1, n_seq, 1))

    n_swa = cache_swa_k.shape[0]
    ck_all = cache_swa_k.astype(F32).reshape(n_swa * DB * n_buf, KV_W)
    cv_all = cache_swa_v.astype(F32).reshape(n_swa * DB * n_buf, KV_W)
    cmk_all = cache_swa_meta_k.astype(F32).reshape(n_swa * DB * N_META, KV_W)
    cmv_all = cache_swa_meta_v.astype(F32).reshape(n_swa * DB * N_META, KV_W)
    swa_p, swa_s, ml_p, ml_s = [], [], [], []
    c_samp = None
    assert n_samp % SMALL_CHUNK == 0 and SMALL_CHUNK % DT == 0
    for layer in range(depth):
        j = layer // 2
        nw = norm_w[layer].astype(F32).reshape(1, D)
        if layer % 2 == 0:
            w_in = swa_w_in[j].astype(BF16)
            w_out = swa_w_out[j].astype(BF16)
            qn = swa_q_norm[j].astype(F32).reshape(1, HEAD_DIM)
            kn = swa_k_norm[j].astype(F32).reshape(1, HEAD_DIM)
            qn_scaled = (qn * (HEAD_DIM ** -0.5)).reshape(HEAD_DIM, 1)
            qt_r, k_r, vt_r, gt_r = _swa_proj_t(xr, nw, w_in.T, qn_scaled, kn.reshape(HEAD_DIM, 1), PROJ_T_ROW_TILE)
            q_s, k_s, v_s, g_s = _swa_proj(xs, nw, w_in, qn, kn, SMALL_ROW_TILE)
            mk = k_s[s_meta].reshape(B, N_META, KV_W)
            mv = v_s[s_meta].reshape(B, N_META, KV_W)
            xr = _swa_prompt(qt_r, k_r, vt_r, gt_r, xr, mk, jnp.swapaxes(mv, 1, 2), bias_pt, mask_pt,
                             jnp.swapaxes(sink_rows(swa_sinks[j], ATT_BLOCK), 1, 2), w_out, B, n_blk)
            k_new = k_s[:n_samp]
            v_new = v_s[:n_samp]
            o_samp = _swa_small(_to_group_rows(q_s[:n_samp], DB, DT), [cmk_all, ck_all, k_new], [cmv_all, cv_all, v_new],
                                seg_s, (j * DB, j * DB, 0),
                                bias_s, mask_sg, sink_rows(swa_sinks[j], DT, grp_s), grp_s, GROUP * DT)
            o_meta = _swa_small(_to_group_rows(q_s[s_meta], B, N_META), [k_s[s_meta]], [v_s[s_meta]], seg_m, (0,),
                                bias_m, mask_mg, sink_rows(swa_sinks[j], N_META, B), B, GROUP * N_META)
            o_small = small_rows(_from_group_rows(o_samp, DB, DT), _from_group_rows(o_meta, B, N_META))
            xs = _resid_matmul(o_small, w_out, xs, SMALL_ROW_TILE, gate=g_s)
            n_keep = min(WINDOW, T + N_META)
            last = lambda a: a.reshape(B, T, KV_W)[:, T - n_keep:].reshape(B, n_keep, N_KV, HEAD_DIM)
            v_last = jnp.stack([vt_r[:, (b + 1) * T - n_keep:(b + 1) * T].T for b in range(B)])
            swa_p.append((last(k_r), v_last.reshape(B, n_keep, N_KV, HEAD_DIM),
                          mk.reshape(B, N_META, N_KV, HEAD_DIM), mv.reshape(B, N_META, N_KV, HEAD_DIM)))
            swa_s.append((k_new.reshape(DB, DT, N_KV, HEAD_DIM), v_new.reshape(DB, DT, N_KV, HEAD_DIM)))
        else:
            w_in = mlstm_w_in[j].astype(BF16)
            w_out = mlstm_w_out[j].astype(BF16)
            conv_wts = (mlstm_conv_w[j].astype(F32), mlstm_conv_b[j].astype(F32).reshape(1, M_INNER))
            wts = (mlstm_wq[j].astype(BF16), mlstm_wk[j].astype(BF16), mlstm_wv[j].astype(BF16),
                   mlstm_w_gates[j].astype(BF16), mlstm_b_gates[j].astype(F32).reshape(1, 2 * M_HEADS))
            hn = mlstm_head_norm[j].astype(F32).reshape(1, M_INNER)
            sk = mlstm_skip[j].astype(F32).reshape(1, M_INNER)
            p_s = _norm_proj(xs, nw, w_in, SMALL_ROW_TILE, M_INNER, F32)
            xm_samp = p_s[:n_samp, :M_INNER].reshape(DB, DT, M_INNER)
            xm_meta = p_s[s_meta, :M_INNER].reshape(B, N_META, M_INNER)
            xpad_s = jnp.concatenate([state_mlstm_conv[j].astype(F32), xm_samp], axis=1)
            xpad_m = jnp.concatenate([jnp.zeros((B, CONV_W - 1, M_INNER), F32), xm_meta], axis=1)
            shifted = [small_rows(xpad_s[:, w:w + DT].reshape(n_samp, M_INNER),
                                  xpad_m[:, w:w + N_META].reshape(n_meta, M_INNER)) for w in range(CONV_W)]
            q_s, k_s, v_s, gt_s, xc_s = _qkv_small(shifted, conv_wts, wts, SMALL_ROW_TILE)
            zo_s = p_s[:, M_INNER:]
            cs_real = jnp.pad(xpad_m[:, N_META:], ((0, 0), (8 - (CONV_W - 1), 0), (0, 0)))
            q_r, k_r, v_r, gt_r, xc_r, zo_r, xm_tail = _mlstm_in(xr.reshape(B, T, D), nw, w_in, cs_real,
                                                                 conv_wts, wts, ROW_TILE)

            to3 = lambda a, sl, b, t: a[sl].reshape(b, t, a.shape[-1])
            zc = jnp.zeros((1, B, M_HEADS, M_HD, M_HD), F32)
            zn = jnp.zeros((1, B, M_HEADS, 1, M_HD), F32)
            zm = jnp.zeros((1, B, M_HEADS, 1, 128), F32)
            u_m, c_m, n_m, m_m = _mlstm_chunk(
                to3(q_s, s_meta, B, N_META), to3(k_s, s_meta, B, N_META), to3(v_s, s_meta, B, N_META),
                to3(xc_s, s_meta, B, N_META), to3(zo_s, s_meta, B, N_META), to3(gt_s, s_meta, B, N_META),
                zc, zn, zm, 0, hn, sk, N_META, N_META, M_HEADS)
            x_new, c_r, n_r, m_r = _mlstm_chunk(q_r, k_r, v_r, xc_r, zo_r, gt_r, c_m, n_m, m_m, 0, hn, sk,
                                                PROMPT_CHUNK, PROMPT_CHUNK, M_HEADS,
                                                x=xr.reshape(B, T, D), w_out=w_out)
            s_samp = slice(0, n_samp)
            nb_s = n_samp // SMALL_CHUNK
            u_s, c_samp, n_s, m_s = _mlstm_chunk(
                to3(q_s, s_samp, nb_s, SMALL_CHUNK), to3(k_s, s_samp, nb_s, SMALL_CHUNK),
                to3(v_s, s_samp, nb_s, SMALL_CHUNK), to3(xc_s, s_samp, nb_s, SMALL_CHUNK),
                to3(zo_s, s_samp, nb_s, SMALL_CHUNK), to3(gt_s, s_samp, nb_s, SMALL_CHUNK),
                state_mlstm_c.astype(F32), state_mlstm_n.astype(F32)[:, :, :, None, :],
                jnp.broadcast_to(state_mlstm_m.astype(F32)[..., None, None], state_mlstm_m.shape + (1, 128)),
                j, hn, sk, SMALL_CHUNK, DT, 1, c_acc=c_samp, lyr_out=j, n_lyr_out=state_mlstm_c.shape[0])
            xr = x_new.reshape(n_real, D)
            u_small = small_rows(u_s.reshape(n_samp, M_INNER), u_m.reshape(n_meta, M_INNER))
            xs = _resid_matmul(u_small, w_out, xs, SMALL_ROW_TILE)
            ml_p.append((c_r[0], n_r.reshape(B, M_HEADS, M_HD), m_r[0, :, :, 0, 0],
                         xm_tail[:, 8 - (CONV_W - 1):]))
            ml_s.append((n_s.reshape(DB, M_HEADS, M_HD), m_s[0, :, :, 0, 0], xpad_s[:, DT:]))

    sd = state_mlstm_c.dtype
    stack = lambda rows, idx: jnp.stack([r[idx] for r in rows])
    window = lambda cache, idx: jnp.concatenate([cache[:, :, DT:].astype(F32), stack(swa_s, idx)], axis=2)
    return (xr.reshape(B, T, D), xs[:n_samp].reshape(DB, DT, D),
            stack(swa_p, 0), stack(swa_p, 1), stack(swa_p, 2), stack(swa_p, 3),
            stack(ml_p, 0).astype(sd), stack(ml_p, 1).astype(sd), stack(ml_p, 2).astype(sd), stack(ml_p, 3),
            window(cache_swa_k, 0), window(cache_swa_v, 1),
            c_samp.astype(sd), stack(ml_s, 0).astype(sd), stack(ml_s, 1).astype(sd), stack(ml_s, 2))
```

```python
import functools
import math

import jax
import jax.numpy as jnp
from jax import lax
from jax.experimental import pallas as pl
from jax.experimental.pallas import tpu as pltpu

F32 = jnp.float32
BF16 = jnp.bfloat16

D_MODEL = 1024
N_META = 16
EPS = 1e-6
NEG_INF = -1e30
N_HEADS = 16
HEAD_DIM = 64
N_KV = 4
GROUP = 4
ATT_W = N_HEADS * HEAD_DIM
KV_W = N_KV * HEAD_DIM
WINDOW = 128
ATT_BLOCK = 128
N_BUCKETS = 32
MAX_DISTANCE = 128
M_INNER = 2048
M_HEADS = 4
M_HD = 512
CONV_W = 4
PAST_LEN = 8192

VMEM_LIMIT_BYTES = 56 * 1024 * 1024
PROMPT_CHUNK = 256
SMALL_CHUNK = 16
ATT_BLOCKS_PER_STEP = 4
ROW_TILE = 256
PROJ_T_ROW_TILE = 512
SMALL_ROW_TILE = 128
SAMPLE_ATT_GROUP = 8


def _cparams(*sem):
    return pltpu.CompilerParams(dimension_semantics=sem, vmem_limit_bytes=VMEM_LIMIT_BYTES)


def _rms(x, w):
    ms = jnp.mean(x * x, axis=-1, keepdims=True)
    return x * lax.rsqrt(ms + EPS) * w


def _sigmoid(x):
    return 1.0 / (1.0 + jnp.exp(-x))


def _silu(x):
    return x * _sigmoid(x)


def _dot(a, b):
    return jnp.dot(a, b, preferred_element_type=F32)


def _dot_nt(a, b):
    return lax.dot_general(a, b, (((1,), (1,)), ((), ())), preferred_element_type=F32)


def _dot_tn(a, b):
    return lax.dot_general(a, b, (((0,), (0,)), ((), ())), preferred_element_type=F32)


def _norm_proj_kernel(x_ref, nw_ref, w_ref, o_ref):
    h = _rms(x_ref[...], nw_ref[...]).astype(BF16)
    o_ref[...] = _dot(h, w_ref[...]).astype(o_ref.dtype)


def _norm_proj(x, nw, w, tm, tn, out_dtype):
    rows, d = x.shape
    n = w.shape[1]
    return pl.pallas_call(
        _norm_proj_kernel,
        grid=(n // tn, rows // tm),
        in_specs=[pl.BlockSpec((tm, d), lambda j, i: (i, 0)),
                  pl.BlockSpec((1, d), lambda j, i: (0, 0)),
                  pl.BlockSpec((d, tn), lambda j, i: (0, j))],
        out_specs=pl.BlockSpec((tm, tn), lambda j, i: (i, j)),
        out_shape=jax.ShapeDtypeStruct((rows, n), out_dtype),
        compiler_params=_cparams("parallel", "parallel"),
        name="norm_proj",
    )(x, nw, w)


def _swa_proj_kernel(x_ref, nw_ref, w_ref, qn_ref, kn_ref, q_ref, k_ref, v_ref, g_ref):
    h = _rms(x_ref[...], nw_ref[...]).astype(BF16)
    proj = _dot(h, w_ref[...])
    qn = qn_ref[...]
    kn = kn_ref[...]
    for hd in range(N_HEADS):
        sl = proj[:, hd * HEAD_DIM:(hd + 1) * HEAD_DIM]
        q_ref[:, hd * HEAD_DIM:(hd + 1) * HEAD_DIM] = _rms(sl, qn).astype(q_ref.dtype)
    for hd in range(N_KV):
        sl = proj[:, ATT_W + hd * HEAD_DIM:ATT_W + (hd + 1) * HEAD_DIM]
        k_ref[:, hd * HEAD_DIM:(hd + 1) * HEAD_DIM] = _rms(sl, kn)
    v_ref[...] = proj[:, ATT_W + KV_W:ATT_W + 2 * KV_W]
    g_ref[...] = _silu(proj[:, ATT_W + 2 * KV_W:])


def _swa_proj(x, nw, w, qn, kn, tm):
    rows, d = x.shape
    n = w.shape[1]
    row = lambda i: (i, 0)
    fixed = lambda i: (0, 0)
    return pl.pallas_call(
        _swa_proj_kernel,
        grid=(rows // tm,),
        in_specs=[pl.BlockSpec((tm, d), row), pl.BlockSpec((1, d), fixed), pl.BlockSpec((d, n), fixed),
                  pl.BlockSpec((1, HEAD_DIM), fixed), pl.BlockSpec((1, HEAD_DIM), fixed)],
        out_specs=[pl.BlockSpec((tm, ATT_W), row), pl.BlockSpec((tm, KV_W), row),
                   pl.BlockSpec((tm, KV_W), row), pl.BlockSpec((tm, ATT_W), row)],
        out_shape=[jax.ShapeDtypeStruct((rows, ATT_W), BF16), jax.ShapeDtypeStruct((rows, KV_W), F32),
                   jax.ShapeDtypeStruct((rows, KV_W), F32), jax.ShapeDtypeStruct((rows, ATT_W), F32)],
        compiler_params=_cparams("parallel"),
        name="swa_proj",
    )(x, nw, w, qn, kn)


def _swa_proj_t_kernel(x_ref, nw_ref, wt_ref, qn_ref, kn_ref, qt_ref, k_ref, vt_ref, gt_ref):
    h = _rms(x_ref[...], nw_ref[...]).astype(BF16)
    pt = _dot_nt(wt_ref[...], h)

    def head_norm(r0, gain):
        sl = pt[r0:r0 + HEAD_DIM]
        ms = jnp.mean(sl * sl, axis=0, keepdims=True)
        return sl * lax.rsqrt(ms + EPS) * gain

    qn = qn_ref[...]
    kn = kn_ref[...]
    for hd in range(N_HEADS):
        qt_ref[hd * HEAD_DIM:(hd + 1) * HEAD_DIM, :] = head_norm(hd * HEAD_DIM, qn).astype(qt_ref.dtype)
    kt = jnp.concatenate([head_norm(ATT_W + hd * HEAD_DIM, kn) for hd in range(N_KV)], axis=0)
    k_ref[...] = kt.T
    vt_ref[...] = pt[ATT_W + KV_W:ATT_W + 2 * KV_W]
    gt_ref[...] = _silu(pt[ATT_W + 2 * KV_W:])


def _swa_proj_t(x, nw, wt, qn_col, kn_col, tm):
    rows, d = x.shape
    nt = wt.shape[0]
    row = lambda i: (i, 0)
    col = lambda i: (0, i)
    fixed = lambda i: (0, 0)
    return pl.pallas_call(
        _swa_proj_t_kernel,
        grid=(rows // tm,),
        in_specs=[pl.BlockSpec((tm, d), row), pl.BlockSpec((1, d), fixed), pl.BlockSpec((nt, d), fixed),
                  pl.BlockSpec((HEAD_DIM, 1), fixed), pl.BlockSpec((HEAD_DIM, 1), fixed)],
        out_specs=[pl.BlockSpec((ATT_W, tm), col), pl.BlockSpec((tm, KV_W), row),
                   pl.BlockSpec((KV_W, tm), col), pl.BlockSpec((ATT_W, tm), col)],
        out_shape=[jax.ShapeDtypeStruct((ATT_W, rows), BF16), jax.ShapeDtypeStruct((rows, KV_W), F32),
                   jax.ShapeDtypeStruct((KV_W, rows), F32), jax.ShapeDtypeStruct((ATT_W, rows), F32)],
        compiler_params=_cparams("parallel"),
        name="swa_proj_t",
    )(x, nw, wt, qn_col, kn_col)


def _resid_matmul_kernel(u_ref, w_ref, x_ref, o_ref):
    o_ref[...] = x_ref[...] + _dot(u_ref[...], w_ref[...])


def _gated_resid_matmul_kernel(a_ref, g_ref, w_ref, x_ref, o_ref):
    u = (a_ref[...] * g_ref[...]).astype(BF16)
    o_ref[...] = x_ref[...] + _dot(u, w_ref[...])


def _resid_matmul(u, w, x, tm, gate=None):
    rows, k = u.shape
    d = w.shape[1]
    row = lambda i: (i, 0)
    ins = [u] if gate is None else [u, gate]
    body = _resid_matmul_kernel if gate is None else _gated_resid_matmul_kernel
    return pl.pallas_call(
        body,
        grid=(rows // tm,),
        in_specs=[pl.BlockSpec((tm, k), row)] * len(ins)
                 + [pl.BlockSpec((k, d), lambda i: (0, 0)), pl.BlockSpec((tm, d), row)],
        out_specs=pl.BlockSpec((tm, d), row),
        out_shape=jax.ShapeDtypeStruct((rows, d), F32),
        compiler_params=_cparams("parallel"),
        name="resid_matmul",
    )(*ins, w, x)


def _bias_table_kernel(idx_ref, tab_ref, o_ref):
    idx = idx_ref[...]
    for hd in range(N_HEADS):
        acc = jnp.zeros(idx.shape, F32)
        for b in range(N_BUCKETS):
            acc = jnp.where(idx == b, tab_ref[b, hd], acc)
        o_ref[hd] = acc


def _bias_table(idx, rel_bias):
    r, c = idx.shape
    return pl.pallas_call(
        _bias_table_kernel,
        in_specs=[pl.BlockSpec((r, c), lambda: (0, 0)), pl.BlockSpec(memory_space=pltpu.SMEM)],
        out_specs=pl.BlockSpec((N_HEADS, r, c), lambda: (0, 0, 0)),
        out_shape=jax.ShapeDtypeStruct((N_HEADS, r, c), F32),
        name="bias_table",
    )(idx, rel_bias)


def _rel_bucket(dist):
    n = jnp.maximum(dist, 0)
    max_exact = N_BUCKETS // 2
    nf = jnp.maximum(n, 1).astype(F32)
    large = max_exact + (jnp.log(nf / max_exact) / math.log(MAX_DISTANCE / max_exact)
                         * (N_BUCKETS - max_exact)).astype(jnp.int32)
    large = jnp.minimum(large, N_BUCKETS - 1)
    return jnp.where(n < max_exact, n, large)


def _mask_and_bucket(q_pos, k_pos, k_valid):
    dist = q_pos[:, None] - k_pos[None, :]
    mask = k_valid[None, :] & (dist >= 0) & ((dist <= WINDOW) | (k_pos[None, :] < N_META))
    return mask, _rel_bucket(dist)


def _swa_prompt_kernel(qt_ref, kc_ref, kp_ref, vtc_ref, vtp_ref, mk_ref, mvt_ref, gt_ref, x_ref,
                       bias_ref, mask_ref, sink_ref, w_ref, o_ref, att_ref):
    first = jnp.minimum(pl.program_id(1), 1)
    kc = kc_ref[...]
    vtc = vtc_ref[...]
    mk = mk_ref[...]
    mvt = mvt_ref[...]
    qt = qt_ref[...]
    for s in range(ATT_BLOCKS_PER_STEP):
        geo = first if s == 0 else 1
        own = slice(s * ATT_BLOCK, (s + 1) * ATT_BLOCK)
        k_prev = kp_ref[...] if s == 0 else kc[(s - 1) * ATT_BLOCK:s * ATT_BLOCK]
        vt_prev = vtp_ref[...] if s == 0 else vtc[:, (s - 1) * ATT_BLOCK:s * ATT_BLOCK]
        kband = jnp.concatenate([k_prev, kc[own], mk], axis=0).astype(BF16)
        vtband = jnp.concatenate([vt_prev, vtc[:, own], mvt], axis=1).astype(BF16)
        keep = mask_ref[geo] > 0.5
        for kv in range(N_KV):
            kk = kband[:, kv * HEAD_DIM:(kv + 1) * HEAD_DIM]
            vv = vtband[kv * HEAD_DIM:(kv + 1) * HEAD_DIM]
            qtg = jnp.concatenate(
                [qt[(kv * GROUP + g) * HEAD_DIM:(kv * GROUP + g + 1) * HEAD_DIM, own] for g in range(GROUP)], axis=1)
            logits = _dot(kk, qtg) + bias_ref[geo, kv]
            logits = jnp.where(keep, logits, NEG_INF)
            sk = sink_ref[kv]
            m = jnp.maximum(jnp.max(logits, axis=0, keepdims=True), sk)
            p = jnp.exp(logits - m)
            den = jnp.sum(p, axis=0, keepdims=True) + jnp.exp(sk - m)
            otg = _dot(vv, p.astype(BF16)) * (1.0 / den)
            for g in range(GROUP):
                hd = kv * GROUP + g
                att_ref[hd * HEAD_DIM:(hd + 1) * HEAD_DIM, own] = otg[:, g * ATT_BLOCK:(g + 1) * ATT_BLOCK]
    ut = (att_ref[...] * gt_ref[...]).astype(BF16)
    o_ref[...] = x_ref[...] + _dot_tn(ut, w_ref[...])


def _swa_prompt(qt, kn, vt, sgt, x, mk, mvt, bias_t, mask_t, sink_t, w_out, batch, n_blk):
    nb = ATT_BLOCKS_PER_STEP
    n_step = n_blk // nb
    tq = nb * ATT_BLOCK
    cur = lambda b, i: (b * n_step + i, 0)
    prev = lambda b, i: (b * n_blk + jnp.maximum(nb * i - 1, 0), 0)
    cur_t = lambda b, i: (0, b * n_step + i)
    prev_t = lambda b, i: (0, b * n_blk + jnp.maximum(nb * i - 1, 0))
    meta = lambda b, i: (b, 0, 0)
    tk = N_META + 2 * ATT_BLOCK
    nq = GROUP * ATT_BLOCK
    rows = kn.shape[0]
    return pl.pallas_call(
        _swa_prompt_kernel,
        grid=(batch, n_step),
        in_specs=[pl.BlockSpec((ATT_W, tq), cur_t),
                  pl.BlockSpec((tq, KV_W), cur), pl.BlockSpec((ATT_BLOCK, KV_W), prev),
                  pl.BlockSpec((KV_W, tq), cur_t), pl.BlockSpec((KV_W, ATT_BLOCK), prev_t),
                  pl.BlockSpec((None, N_META, KV_W), meta), pl.BlockSpec((None, KV_W, N_META), meta),
                  pl.BlockSpec((ATT_W, tq), cur_t), pl.BlockSpec((tq, D_MODEL), cur),
                  pl.BlockSpec((2, N_KV, tk, nq), lambda b, i: (0, 0, 0, 0)),
                  pl.BlockSpec((2, tk, nq), lambda b, i: (0, 0, 0)),
                  pl.BlockSpec((N_KV, 1, nq), lambda b, i: (0, 0, 0)),
                  pl.BlockSpec((ATT_W, D_MODEL), lambda b, i: (0, 0))],
        out_specs=pl.BlockSpec((tq, D_MODEL), cur),
        out_shape=jax.ShapeDtypeStruct((rows, D_MODEL), F32),
        scratch_shapes=[pltpu.VMEM((ATT_W, tq), F32)],
        compiler_params=_cparams("parallel", "parallel"),
        name="swa_prompt",
    )(qt, kn, kn, vt, vt, mk, mvt, sgt, x, bias_t, mask_t, sink_t, w_out)


def _swa_small_kernel(n_seg, *refs):
    q_ref = refs[0]
    k_refs = refs[1:1 + n_seg]
    v_refs = refs[1 + n_seg:1 + 2 * n_seg]
    bias_ref, mask_ref, sink_ref, o_ref = refs[1 + 2 * n_seg:]
    keep = mask_ref[...] > 0.5
    scale = HEAD_DIM ** -0.5
    kall = jnp.concatenate([r[...] for r in k_refs], axis=0).astype(BF16) if n_seg > 1 else k_refs[0][...].astype(BF16)
    vall = jnp.concatenate([r[...] for r in v_refs], axis=0).astype(BF16) if n_seg > 1 else v_refs[0][...].astype(BF16)
    for kv in range(N_KV):
        kk = kall[:, kv * HEAD_DIM:(kv + 1) * HEAD_DIM]
        vv = vall[:, kv * HEAD_DIM:(kv + 1) * HEAD_DIM]
        logits = _dot_nt(q_ref[kv], kk) * scale + bias_ref[kv]
        logits = jnp.where(keep, logits, NEG_INF)
        sk = sink_ref[kv]
        m = jnp.maximum(jnp.max(logits, axis=-1, keepdims=True), sk)
        p = jnp.exp(logits - m)
        den = jnp.sum(p, axis=-1, keepdims=True) + jnp.exp(sk - m)
        o_ref[kv] = _dot(p.astype(BF16), vv) / den


def _swa_small(q, k_segs, v_segs, seg_len, seg_first, bias, mask, sink, group, rq):
    rows = q.shape[1]
    n_seg = len(k_segs)
    tk = group * sum(seg_len)
    seg_specs = [pl.BlockSpec((group * n, KV_W), functools.partial(lambda o, i: (o + i, 0), f // group))
                 for n, f in zip(seg_len, seg_first)]
    return pl.pallas_call(
        functools.partial(_swa_small_kernel, n_seg),
        grid=(rows // (group * rq),),
        in_specs=[pl.BlockSpec((N_KV, group * rq, HEAD_DIM), lambda i: (0, i, 0))] + seg_specs + seg_specs
                 + [pl.BlockSpec((N_KV, group * rq, tk), lambda i: (0, 0, 0)),
                    pl.BlockSpec((group * rq, tk), lambda i: (0, 0)),
                    pl.BlockSpec((N_KV, group * rq, 1), lambda i: (0, 0, 0))],
        out_specs=pl.BlockSpec((N_KV, group * rq, HEAD_DIM), lambda i: (0, i, 0)),
        out_shape=jax.ShapeDtypeStruct((N_KV, rows, HEAD_DIM), F32),
        compiler_params=_cparams("parallel"),
        name="swa_small",
    )(q, *k_segs, *v_segs, bias, mask, sink)


def _to_group_rows(a, nb, t):
    a = a.reshape(nb, t, N_KV, GROUP, HEAD_DIM)
    return jnp.transpose(a, (2, 0, 3, 1, 4)).reshape(N_KV, nb * GROUP * t, HEAD_DIM)


def _from_group_rows(a, nb, t):
    a = a.reshape(N_KV, nb, GROUP, t, HEAD_DIM)
    return jnp.transpose(a, (1, 3, 0, 2, 4)).reshape(nb * t, ATT_W)


def _bias_group_rows(b, t):
    return b.reshape(N_KV, GROUP * t, b.shape[-1])


def _block_diag_keys(a, group, seg_len):
    eye = jnp.eye(group, dtype=a.dtype)
    out, o = [], 0
    for n in seg_len:
        blk = a[..., o:o + n]
        o += n
        big = eye[:, None, :, None] * blk[..., None, :, None, :]
        out.append(big.reshape(a.shape[:-2] + (group * a.shape[-2], group * n)))
    return jnp.concatenate(out, axis=-1)


def _conv_silu(xs, cw_ref, cb_ref):
    cw = cw_ref[...]
    acc = xs[0] * cw[0:1]
    for w in range(1, CONV_W):
        acc = acc + xs[w] * cw[w:w + 1]
    return _silu(acc + cb_ref[...])


def _mlstm_in_kernel(x_ref, nw_ref, w_ref, cs_ref, cw_ref, cb_ref, wq_ref, wk_ref, wv_ref, wg_ref, bg_ref,
                     q_ref, k_ref, v_ref, gt_ref, xc_ref, zo_ref, tail_ref):
    i = pl.program_id(1)
    h = _rms(x_ref[...], nw_ref[...]).astype(BF16)
    zo_ref[...] = _dot(h, w_ref[:, M_INNER:]).astype(zo_ref.dtype)
    xm = _dot(h, w_ref[:, :M_INNER])
    tm = xm.shape[0]
    halo = jnp.where(i == 0, cs_ref[...], tail_ref[...])
    full = jnp.concatenate([halo, xm], axis=0)
    n = tm + 8
    xs = [pltpu.roll(full, n - (8 - (CONV_W - 1) + w), 0)[:tm] for w in range(CONV_W - 1)] + [xm]
    xcb = _conv_silu(xs, cw_ref, cb_ref).astype(BF16)
    xc_ref[...] = xcb
    tail_ref[...] = xm[tm - 8:]
    _qkv_matmuls(xcb, xm.astype(BF16), wq_ref, wk_ref, wv_ref, wg_ref, bg_ref, q_ref, k_ref, v_ref, gt_ref)


def _mlstm_in(x3, nw, w, cs, conv_wts, wts, tm):
    b, t, d = x3.shape
    cur = lambda bi, i: (bi, i, 0)
    seq = lambda bi, i: (bi, 0, 0)
    fixed = lambda bi, i: (0, 0)
    once = pl.Buffered(1)
    tok = pl.BlockSpec((None, tm, M_INNER), cur)
    return pl.pallas_call(
        _mlstm_in_kernel,
        grid=(b, t // tm),
        in_specs=[pl.BlockSpec((None, tm, d), cur), pl.BlockSpec((1, d), fixed),
                  pl.BlockSpec((d, 3 * M_INNER), fixed, pipeline_mode=once),
                  pl.BlockSpec((None, 8, M_INNER), seq),
                  pl.BlockSpec((CONV_W, M_INNER), fixed), pl.BlockSpec((1, M_INNER), fixed)]
                 + _qkv_weight_specs(fixed, lambda bi, i: (0, 0, 0), once),
        out_specs=[tok, tok, tok, pl.BlockSpec((None, tm, 2 * M_HEADS), cur), tok,
                   pl.BlockSpec((None, tm, 2 * M_INNER), cur), pl.BlockSpec((None, 8, M_INNER), seq)],
        out_shape=[jax.ShapeDtypeStruct((b, t, M_INNER), BF16)] * 3
                  + [jax.ShapeDtypeStruct((b, t, 2 * M_HEADS), F32), jax.ShapeDtypeStruct((b, t, M_INNER), BF16),
                     jax.ShapeDtypeStruct((b, t, 2 * M_INNER), BF16), jax.ShapeDtypeStruct((b, 8, M_INNER), F32)],
        compiler_params=_cparams("parallel", "arbitrary"),
        name="mlstm_in",
    )(x3, nw, w, cs, *conv_wts, *wts)


def _qkv_matmuls(xcb, xmb, wq_ref, wk_ref, wv_ref, wg_ref, bg_ref, q_ref, k_ref, v_ref, gt_ref):
    gates = bg_ref[...]
    for h in range(M_HEADS):
        sl = slice(h * M_HD, (h + 1) * M_HD)
        qh = _dot(xcb[:, sl], wq_ref[h]).astype(BF16)
        kh = (_dot(xcb[:, sl], wk_ref[h]) * (M_HD ** -0.5)).astype(BF16)
        vh = _dot(xmb[:, sl], wv_ref[h]).astype(BF16)
        q_ref[:, sl] = qh
        k_ref[:, sl] = kh
        v_ref[:, sl] = vh
        gates = (gates + _dot(qh, wg_ref[h * M_HD:(h + 1) * M_HD])
                 + _dot(kh, wg_ref[M_INNER + h * M_HD:M_INNER + (h + 1) * M_HD])
                 + _dot(vh, wg_ref[2 * M_INNER + h * M_HD:2 * M_INNER + (h + 1) * M_HD]))
    lane = lax.broadcasted_iota(jnp.int32, gates.shape, 1)
    log_f = jnp.minimum(gates, 0.0) - jnp.log1p(jnp.exp(-jnp.abs(gates)))
    gt_ref[...] = jnp.where(lane < M_HEADS, gates, log_f)


def _qkv_shift_kernel(x0_ref, x1_ref, x2_ref, x3_ref, cw_ref, cb_ref, *rest):
    xs = [x0_ref[...], x1_ref[...], x2_ref[...], x3_ref[...]]
    xc = _conv_silu(xs, cw_ref, cb_ref)
    xc_ref = rest[-1]
    xc_ref[...] = xc
    _qkv_matmuls(xc.astype(BF16), xs[CONV_W - 1].astype(BF16), *rest[:-1])


def _qkv_weight_specs(fixed2, fixed3, mode=None):
    return [pl.BlockSpec((M_HEADS, M_HD, M_HD), fixed3, pipeline_mode=mode),
            pl.BlockSpec((M_HEADS, M_HD, M_HD), fixed3, pipeline_mode=mode),
            pl.BlockSpec((M_HEADS, M_HD, M_HD), fixed3, pipeline_mode=mode),
            pl.BlockSpec((3 * M_INNER, 2 * M_HEADS), fixed2, pipeline_mode=mode),
            pl.BlockSpec((1, 2 * M_HEADS), fixed2)]


def _qkv_small(xs, conv_wts, wts, tm):
    rows = xs[0].shape[0]
    row = lambda i: (i, 0)
    fixed = lambda i: (0, 0)
    outs = ([jax.ShapeDtypeStruct((rows, M_INNER), BF16)] * 3 + [jax.ShapeDtypeStruct((rows, 2 * M_HEADS), F32)]
            + [jax.ShapeDtypeStruct((rows, M_INNER), F32)])
    return pl.pallas_call(
        _qkv_shift_kernel,
        grid=(rows // tm,),
        in_specs=[pl.BlockSpec((tm, M_INNER), row)] * CONV_W
                 + [pl.BlockSpec((CONV_W, M_INNER), fixed), pl.BlockSpec((1, M_INNER), fixed)]
                 + _qkv_weight_specs(fixed, lambda i: (0, 0, 0)),
        out_specs=[pl.BlockSpec((tm, M_INNER), row)] * 3 + [pl.BlockSpec((tm, 2 * M_HEADS), row)]
                  + [pl.BlockSpec((tm, M_INNER), row)],
        out_shape=outs,
        compiler_params=_cparams("parallel"),
        name="mlstm_qkv_small",
    )(*xs, *conv_wts, *wts)


def _chunk_kernel(seg, hps, aliased, fused, q_ref, k_ref, v_ref, xc_ref, z_ref, op_ref, gc_ref, gr_ref, c0_ref, n0_ref,
                  m0_ref, hn_ref, sk_ref, *rest):
    if fused:
        x_ref, w_ref = rest[:2]
        rest = rest[2:]
    if aliased:
        rest = rest[1:]
    out_ref, c_ref, n_ref, m_ref = rest[:4]
    u_ref = rest[4] if fused else out_ref
    h_blk = pl.program_id(1)
    c_idx = pl.program_id(2)
    L = q_ref.shape[0]
    nseg = L // seg
    hi = lax.Precision.HIGHEST

    @pl.when(c_idx == 0)
    def _():
        c_ref[...] = c0_ref[...]
        n_ref[...] = n0_ref[...]
        m_ref[...] = m0_ref[...]

    gc = gc_ref[...]
    gr = gr_ref[...]
    col_c = lax.broadcasted_iota(jnp.int32, gc.shape, 1)
    row_r = lax.broadcasted_iota(jnp.int32, gr.shape, 0)
    t_i = lax.broadcasted_iota(jnp.int32, (L, L), 0)
    j_i = lax.broadcasted_iota(jnp.int32, (L, L), 1)
    causal = j_i <= t_i
    upper = t_i <= j_i
    if nseg > 1:
        sid_c = lax.broadcasted_iota(jnp.int32, (L, 1), 0) // seg
        in_seg = [sid_c == s for s in range(nseg)]
        pick = lambda vals: sum(jnp.where(in_seg[s], vals[s], 0.0) for s in range(nseg))
        same = (t_i // seg) == (j_i // seg)
        causal = causal & same
        upper = upper & same
    tril = jnp.where(causal, 1.0, 0.0).astype(F32)
    triu = jnp.where(upper, 1.0, 0.0).astype(F32)
    lf_c = jnp.where(col_c >= M_HEADS, gc, 0.0)
    lf_r = jnp.where(row_r >= M_HEADS, gr, 0.0)
    b_all_c = jnp.dot(tril, lf_c, precision=hi, preferred_element_type=F32)
    b_all_r = jnp.dot(lf_r, triu, precision=hi, preferred_element_type=F32)
    if nseg > 1:
        g_all = jnp.dot(jnp.where(same, 1.0, 0.0).astype(F32), lf_c, precision=hi, preferred_element_type=F32)

    for hh in range(hps):
        h_idx = h_blk * hps + hh
        cols = slice(hh * M_HD, (hh + 1) * M_HD)
        sel_c = lambda a, c: jnp.sum(jnp.where(col_c == c, a, 0.0), axis=1, keepdims=True)
        sel_r = lambda a, r: jnp.sum(jnp.where(row_r == r, a, 0.0), axis=0, keepdims=True)
        i_c, i_r = sel_c(gc, h_idx), sel_r(gr, h_idx)
        b_c, b_r = sel_c(b_all_c, M_HEADS + h_idx), sel_r(b_all_r, M_HEADS + h_idx)
        if nseg > 1:
            g = sel_c(g_all, M_HEADS + h_idx)
            m_old = [m_ref[s, hh][:, 0:1] for s in range(nseg)]
            m_prev = pick(m_old)
        else:
            g = b_c[L - 1:L, :]
            m_prev = m_ref[0, hh][:, 0:1]

        log_d = jnp.where(causal, b_c + (i_r - b_r), -jnp.inf)
        inter = b_c + m_prev
        m_t = jnp.maximum(inter, jnp.max(log_d, axis=-1, keepdims=True))
        w_intra = jnp.exp(log_d - m_t)
        w_inter = jnp.exp(inter - m_t)
        q = q_ref[:, cols]
        k = k_ref[:, cols]
        v = v_ref[:, cols]
        s_mat = _dot_nt(q, k) * w_intra
        qf = q.astype(F32)
        if nseg > 1:
            q_c = pick([_dot(q, c_ref[s, hh].astype(BF16)) for s in range(nseg)])
            q_n = pick([jnp.sum(qf * n_ref[s, hh], axis=-1, keepdims=True) for s in range(nseg)])
        else:
            q_c = _dot(q, c_ref[0, hh].astype(BF16))
            q_n = jnp.sum(qf * n_ref[0, hh], axis=-1, keepdims=True)
        num = _dot(s_mat.astype(BF16), v) + w_inter * q_c
        den = jnp.sum(s_mat, axis=-1, keepdims=True) + w_inter * q_n
        h = num * (1.0 / jnp.maximum(jnp.abs(den), jnp.exp(-m_t)))

        lw = (g - b_c) + i_c
        kf = k.astype(F32)
        if nseg > 1:
            g_s = [jnp.max(jnp.where(in_seg[s], g, -jnp.inf), axis=0, keepdims=True) for s in range(nseg)]
            m_new = [jnp.maximum(g_s[s] + m_old[s],
                                 jnp.max(jnp.where(in_seg[s], lw, -jnp.inf), axis=0, keepdims=True))
                     for s in range(nseg)]
            kw = kf * jnp.exp(lw - pick(m_new))
            for s in range(nseg):
                decay = jnp.exp(g_s[s] + m_old[s] - m_new[s])
                kw_s = jnp.where(in_seg[s], kw, 0.0)
                c_ref[s, hh] = decay * c_ref[s, hh] + _dot_tn(kw_s.astype(BF16), v)
                n_ref[s, hh] = decay * n_ref[s, hh] + jnp.sum(kw_s, axis=0, keepdims=True)
                m_ref[s, hh] = jnp.broadcast_to(m_new[s], m_ref.shape[2:])
        else:
            m_new = jnp.maximum(g + m_prev, jnp.max(lw, axis=0, keepdims=True))
            decay = jnp.exp(g + m_prev - m_new)
            kw = kf * jnp.exp(lw - m_new)
            c_ref[0, hh] = decay * c_ref[0, hh] + _dot_tn(kw.astype(BF16), v)
            n_ref[0, hh] = decay * n_ref[0, hh] + jnp.sum(kw, axis=0, keepdims=True)
            m_ref[0, hh] = jnp.broadcast_to(m_new, m_ref.shape[2:])

        mu = jnp.mean(h, axis=-1, keepdims=True)
        hc = h - mu
        var = jnp.mean(hc * hc, axis=-1, keepdims=True)
        h_out = _sigmoid(op_ref[:, cols].astype(F32)) * (hc * lax.rsqrt(var + EPS) * hn_ref[:, cols])
        u = (h_out + sk_ref[:, cols] * xc_ref[:, cols].astype(F32)) * _silu(z_ref[:, cols].astype(F32))
        u_ref[:, cols] = u.astype(u_ref.dtype)
    if fused:
        out_ref[...] = x_ref[...] + _dot(u_ref[...], w_ref[...])


def _mlstm_chunk(q, k, v, xc, zo, gates, c0, n0, m0, lyr_in, head_norm, skip, chunk, seg, hps,
                 c_acc=None, lyr_out=0, n_lyr_out=1, x=None, w_out=None):
    nb, t, _ = q.shape
    nc = t // chunk
    nseg = chunk // seg
    assert nseg == 1 or nc == 1
    n_seq = nb * nseg
    n_hblk = M_HEADS // hps
    w = hps * M_HD
    gates_t = jnp.swapaxes(gates, 1, 2)
    tok = lambda bi, h, c: (bi, c, h)
    hd_blk = pl.BlockSpec((None, chunk, w), tok)
    st_in = lambda bi, h, c: (lyr_in, bi, h, 0, 0)
    st_out = lambda bi, h, c: (0, bi, h, 0, 0)
    c_out = lambda bi, h, c: (lyr_out, bi, h, 0, 0)
    head_row = pl.BlockSpec((1, w), lambda bi, h, c: (0, h))
    in_specs = [hd_blk, hd_blk, hd_blk, hd_blk,
                hd_blk,
                pl.BlockSpec((None, chunk, w), lambda bi, h, c: (bi, c, n_hblk + h)),
                pl.BlockSpec((None, chunk, 2 * M_HEADS), lambda bi, h, c: (bi, c, 0)),
                pl.BlockSpec((None, 2 * M_HEADS, chunk), lambda bi, h, c: (bi, 0, c)),
                pl.BlockSpec((None, nseg, hps, M_HD, M_HD), st_in),
                pl.BlockSpec((None, nseg, hps, 1, M_HD), st_in),
                pl.BlockSpec((None, nseg, hps, 1, 128), st_in),
                head_row, head_row]
    args = [q, k, v, xc, zo, zo, gates, gates_t, c0, n0, m0, head_norm, skip]
    fused = x is not None
    first_out = (hd_blk, jax.ShapeDtypeStruct((nb, t, M_INNER), BF16))
    scratch = []
    if fused:
        assert hps == M_HEADS
        d = w_out.shape[1]
        x_blk = pl.BlockSpec((None, chunk, d), lambda bi, h, c: (bi, c, 0))
        in_specs += [x_blk, pl.BlockSpec((M_INNER, d), lambda bi, h, c: (0, 0))]
        args += [x, w_out]
        first_out = (x_blk, jax.ShapeDtypeStruct((nb, t, d), F32))
        scratch = [pltpu.VMEM((chunk, M_INNER), BF16)]
    aliases = {}
    if c_acc is not None:
        in_specs.append(pl.BlockSpec(memory_space=pl.ANY))
        aliases = {len(args): 1}
        args.append(c_acc)
    return pl.pallas_call(
        functools.partial(_chunk_kernel, seg, hps, c_acc is not None, fused),
        grid=(nb, n_hblk, nc),
        in_specs=in_specs,
        out_specs=[first_out[0],
                   pl.BlockSpec((None, nseg, hps, M_HD, M_HD), c_out),
                   pl.BlockSpec((None, nseg, hps, 1, M_HD), st_out),
                   pl.BlockSpec((None, nseg, hps, 1, 128), st_out)],
        out_shape=[first_out[1],
                   jax.ShapeDtypeStruct((n_lyr_out, n_seq, M_HEADS, M_HD, M_HD), F32),
                   jax.ShapeDtypeStruct((1, n_seq, M_HEADS, 1, M_HD), F32),
                   jax.ShapeDtypeStruct((1, n_seq, M_HEADS, 1, 128), F32)],
        scratch_shapes=scratch,
        input_output_aliases=aliases,
        compiler_params=_cparams("parallel", "parallel", "arbitrary"),
        name="mlstm_chunk",
    )(*args)


def kernel(x_prompt, x_sample, cache_swa_k, cache_swa_v, cache_swa_meta_k, cache_swa_meta_v, state_mlstm_c, state_mlstm_n, state_mlstm_m, state_mlstm_conv, meta_tokens, rel_bias, norm_w, swa_w_in, swa_q_norm, swa_k_norm, swa_sinks, swa_w_out, mlstm_w_in, mlstm_conv_w, mlstm_conv_b, mlstm_wq, mlstm_wk, mlstm_wv, mlstm_w_gates, mlstm_b_gates, mlstm_head_norm, mlstm_skip, mlstm_w_out):
    B, T, D = x_prompt.shape
    DB, DT, _ = x_sample.shape
    depth = norm_w.shape[0]
    n_buf = cache_swa_k.shape[2]
    n_blk = T // ATT_BLOCK
    n_real, n_samp, n_meta = B * T, DB * DT, B * N_META
    n_small = -(-(n_samp + n_meta) // SMALL_ROW_TILE) * SMALL_ROW_TILE
    pad_small = n_small - n_samp - n_meta
    s_meta = slice(n_samp, n_samp + n_meta)

    def small_rows(samp, meta):
        return jnp.concatenate([samp, meta, jnp.zeros((pad_small,) + samp.shape[1:], samp.dtype)], axis=0)

    xr = x_prompt.reshape(n_real, D)
    xs = small_rows(x_sample.reshape(n_samp, D),
                    jnp.broadcast_to(meta_tokens.astype(x_prompt.dtype)[None], (B, N_META, D)).reshape(n_meta, D))

    ar = lambda n: jnp.arange(n, dtype=jnp.int32)
    meta_pos = ar(N_META)
    geo = []
    for blk in (0, 1):
        q_pos = N_META + blk * ATT_BLOCK + ar(ATT_BLOCK)
        band_pos = N_META + (blk - 1) * ATT_BLOCK + ar(2 * ATT_BLOCK)
        k_pos = jnp.concatenate([meta_pos, band_pos])
        k_valid = jnp.concatenate([jnp.ones((N_META,), bool), band_pos >= N_META])
        geo.append(_mask_and_bucket(q_pos, k_pos, k_valid))
    tk_p = N_META + 2 * ATT_BLOCK
    mask_m, buck_m = _mask_and_bucket(meta_pos, meta_pos, jnp.ones((N_META,), bool))
    buf_pos = PAST_LEN - n_buf + ar(n_buf)
    new_pos = PAST_LEN + ar(DT)
    k_pos_s = jnp.concatenate([meta_pos, buf_pos, new_pos])
    k_valid_s = jnp.concatenate([jnp.ones((N_META,), bool), buf_pos >= N_META, jnp.ones((DT,), bool)])
    mask_s, buck_s = _mask_and_bucket(new_pos, k_pos_s, k_valid_s)
    tk_s = N_META + n_buf + DT

    def place(a, rows):
        return jnp.pad(a, ((0, rows - a.shape[0]), (0, tk_p - a.shape[1])))

    idx_all = jnp.concatenate([geo[0][1], geo[1][1], place(buck_m, N_META), place(buck_s, 8)], axis=0)
    bias_all = _bias_table(idx_all, rel_bias.astype(F32))
    bias_p = jnp.transpose(bias_all[:, :2 * ATT_BLOCK].reshape(N_HEADS, 2, ATT_BLOCK, tk_p), (1, 0, 2, 3))
    bias_p = bias_p.reshape(2, N_KV, GROUP * ATT_BLOCK, tk_p)
    mask_p = jnp.stack([jnp.tile(geo[0][0], (GROUP, 1)), jnp.tile(geo[1][0], (GROUP, 1))]).astype(F32)
    key_major = lambda a: jnp.swapaxes(jnp.concatenate([a[..., N_META:], a[..., :N_META]], axis=-1), -1, -2)
    bias_pt, mask_pt = key_major(bias_p), key_major(mask_p)
    o0 = 2 * ATT_BLOCK
    seg_m, seg_s = (N_META,), (N_META, n_buf, DT)
    grp_s = SAMPLE_ATT_GROUP
    bias_m = _block_diag_keys(_bias_group_rows(bias_all[:, o0:o0 + N_META, :N_META], N_META), B, seg_m)
    mask_mg = _block_diag_keys(jnp.tile(mask_m, (GROUP, 1)).astype(F32), B, seg_m)
    o1 = o0 + N_META
    bias_s = _block_diag_keys(_bias_group_rows(bias_all[:, o1:o1 + DT, :tk_s], DT), grp_s, seg_s)
    mask_sg = _block_diag_keys(jnp.tile(mask_s, (GROUP, 1)).astype(F32), grp_s, seg_s)

    def sink_rows(sinks, t, n_seq=1):
        rows = jnp.repeat(sinks.astype(F32).reshape(N_KV, GROUP, 1), t, axis=2).reshape(N_KV, GROUP * t, 1)
        return jnp.tile(rows, (1, n_seq, 1))

    n_swa = cache_swa_k.shape[0]
    ck_all = cache_swa_k.astype(F32).reshape(n_swa * DB * n_buf, KV_W)
    cv_all = cache_swa_v.astype(F32).reshape(n_swa * DB * n_buf, KV_W)
    cmk_all = cache_swa_meta_k.astype(F32).reshape(n_swa * DB * N_META, KV_W)
    cmv_all = cache_swa_meta_v.astype(F32).reshape(n_swa * DB * N_META, KV_W)
    swa_p, swa_s, ml_p, ml_s = [], [], [], []
    c_samp = None
    assert n_samp % SMALL_CHUNK == 0 and SMALL_CHUNK % DT == 0
    for layer in range(depth):
        j = layer // 2
        nw = norm_w[layer].astype(F32).reshape(1, D)
        if layer % 2 == 0:
            w_in = swa_w_in[j].astype(BF16)
            w_out = swa_w_out[j].astype(BF16)
            qn = swa_q_norm[j].astype(F32).reshape(1, HEAD_DIM)
            kn = swa_k_norm[j].astype(F32).reshape(1, HEAD_DIM)
            qn_scaled = (qn * (HEAD_DIM ** -0.5)).reshape(HEAD_DIM, 1)
            qt_r, k_r, vt_r, gt_r = _swa_proj_t(xr, nw, w_in.T, qn_scaled, kn.reshape(HEAD_DIM, 1), PROJ_T_ROW_TILE)
            q_s, k_s, v_s, g_s = _swa_proj(xs, nw, w_in, qn, kn, SMALL_ROW_TILE)
            mk = k_s[s_meta].reshape(B, N_META, KV_W)
            mv = v_s[s_meta].reshape(B, N_META, KV_W)
            xr = _swa_prompt(qt_r, k_r, vt_r, gt_r, xr, mk, jnp.swapaxes(mv, 1, 2), bias_pt, mask_pt,
                             jnp.swapaxes(sink_rows(swa_sinks[j], ATT_BLOCK), 1, 2), w_out, B, n_blk)
            k_new = k_s[:n_samp]
            v_new = v_s[:n_samp]
            o_samp = _swa_small(_to_group_rows(q_s[:n_samp], DB, DT), [cmk_all, ck_all, k_new], [cmv_all, cv_all, v_new],
                                seg_s, (j * DB, j * DB, 0),
                                bias_s, mask_sg, sink_rows(swa_sinks[j], DT, grp_s), grp_s, GROUP * DT)
            o_meta = _swa_small(_to_group_rows(q_s[s_meta], B, N_META), [k_s[s_meta]], [v_s[s_meta]], seg_m, (0,),
                                bias_m, mask_mg, sink_rows(swa_sinks[j], N_META, B), B, GROUP * N_META)
            o_small = small_rows(_from_group_rows(o_samp, DB, DT), _from_group_rows(o_meta, B, N_META))
            xs = _resid_matmul(o_small, w_out, xs, SMALL_ROW_TILE, gate=g_s)
            n_keep = min(WINDOW, T + N_META)
            last = lambda a: a.reshape(B, T, KV_W)[:, T - n_keep:].reshape(B, n_keep, N_KV, HEAD_DIM)
            v_last = jnp.stack([vt_r[:, (b + 1) * T - n_keep:(b + 1) * T].T for b in range(B)])
            swa_p.append((last(k_r), v_last.reshape(B, n_keep, N_KV, HEAD_DIM),
                          mk.reshape(B, N_META, N_KV, HEAD_DIM), mv.reshape(B, N_META, N_KV, HEAD_DIM)))
            swa_s.append((k_new.reshape(DB, DT, N_KV, HEAD_DIM), v_new.reshape(DB, DT, N_KV, HEAD_DIM)))
        else:
            w_in = mlstm_w_in[j].astype(BF16)
            w_out = mlstm_w_out[j].astype(BF16)
            conv_wts = (mlstm_conv_w[j].astype(F32), mlstm_conv_b[j].astype(F32).reshape(1, M_INNER))
            wts = (mlstm_wq[j].astype(BF16), mlstm_wk[j].astype(BF16), mlstm_wv[j].astype(BF16),
                   mlstm_w_gates[j].astype(BF16), mlstm_b_gates[j].astype(F32).reshape(1, 2 * M_HEADS))
            hn = mlstm_head_norm[j].astype(F32).reshape(1, M_INNER)
            sk = mlstm_skip[j].astype(F32).reshape(1, M_INNER)
            p_s = _norm_proj(xs, nw, w_in, SMALL_ROW_TILE, M_INNER, F32)
            xm_samp = p_s[:n_samp, :M_INNER].reshape(DB, DT, M_INNER)
            xm_meta = p_s[s_meta, :M_INNER].reshape(B, N_META, M_INNER)
            xpad_s = jnp.concatenate([state_mlstm_conv[j].astype(F32), xm_samp], axis=1)
            xpad_m = jnp.concatenate([jnp.zeros((B, CONV_W - 1, M_INNER), F32), xm_meta], axis=1)
            shifted = [small_rows(xpad_s[:, w:w + DT].reshape(n_samp, M_INNER),
                                  xpad_m[:, w:w + N_META].reshape(n_meta, M_INNER)) for w in range(CONV_W)]
            q_s, k_s, v_s, gt_s, xc_s = _qkv_small(shifted, conv_wts, wts, SMALL_ROW_TILE)
            zo_s = p_s[:, M_INNER:]
            cs_real = jnp.pad(xpad_m[:, N_META:], ((0, 0), (8 - (CONV_W - 1), 0), (0, 0)))
            q_r, k_r, v_r, gt_r, xc_r, zo_r, xm_tail = _mlstm_in(xr.reshape(B, T, D), nw, w_in, cs_real,
                                                                 conv_wts, wts, ROW_TILE)

            to3 = lambda a, sl, b, t: a[sl].reshape(b, t, a.shape[-1])
            zc = jnp.zeros((1, B, M_HEADS, M_HD, M_HD), F32)
            zn = jnp.zeros((1, B, M_HEADS, 1, M_HD), F32)
            zm = jnp.zeros((1, B, M_HEADS, 1, 128), F32)
            u_m, c_m, n_m, m_m = _mlstm_chunk(
                to3(q_s, s_meta, B, N_META), to3(k_s, s_meta, B, N_META), to3(v_s, s_meta, B, N_META),
                to3(xc_s, s_meta, B, N_META), to3(zo_s, s_meta, B, N_META), to3(gt_s, s_meta, B, N_META),
                zc, zn, zm, 0, hn, sk, N_META, N_META, M_HEADS)
            x_new, c_r, n_r, m_r = _mlstm_chunk(q_r, k_r, v_r, xc_r, zo_r, gt_r, c_m, n_m, m_m, 0, hn, sk,
                                                PROMPT_CHUNK, PROMPT_CHUNK, M_HEADS,
                                                x=xr.reshape(B, T, D), w_out=w_out)
            s_samp = slice(0, n_samp)
            nb_s = n_samp // SMALL_CHUNK
            u_s, c_samp, n_s, m_s = _mlstm_chunk(
                to3(q_s, s_samp, nb_s, SMALL_CHUNK), to3(k_s, s_samp, nb_s, SMALL_CHUNK),
                to3(v_s, s_samp, nb_s, SMALL_CHUNK), to3(xc_s, s_samp, nb_s, SMALL_CHUNK),
                to3(zo_s, s_samp, nb_s, SMALL_CHUNK), to3(gt_s, s_samp, nb_s, SMALL_CHUNK),
                state_mlstm_c.astype(F32), state_mlstm_n.astype(F32)[:, :, :, None, :],
                jnp.broadcast_to(state_mlstm_m.astype(F32)[..., None, None], state_mlstm_m.shape + (1, 128)),
                j, hn, sk, SMALL_CHUNK, DT, 1, c_acc=c_samp, lyr_out=j, n_lyr_out=state_mlstm_c.shape[0])
            xr = x_new.reshape(n_real, D)
            u_small = small_rows(u_s.reshape(n_samp, M_INNER), u_m.reshape(n_meta, M_INNER))
            xs = _resid_matmul(u_small, w_out, xs, SMALL_ROW_TILE)
            ml_p.append((c_r[0], n_r.reshape(B, M_HEADS, M_HD), m_r[0, :, :, 0, 0],
                         xm_tail[:, 8 - (CONV_W - 1):]))
            ml_s.append((n_s.reshape(DB, M_HEADS, M_HD), m_s[0, :, :, 0, 0], xpad_s[:, DT:]))

    sd = state_mlstm_c.dtype
    stack = lambda rows, idx: jnp.stack([r[idx] for r in rows])
    window = lambda cache, idx: jnp.concatenate([cache[:, :, DT:].astype(F32), stack(swa_s, idx)], axis=2)
    return (xr.reshape(B, T, D), xs[:n_samp].reshape(DB, DT, D),
            stack(swa_p, 0), stack(swa_p, 1), stack(swa_p, 2), stack(swa_p, 3),
            stack(ml_p, 0).astype(sd), stack(ml_p, 1).astype(sd), stack(ml_p, 2).astype(sd), stack(ml_p, 3),
            window(cache_swa_k, 0), window(cache_swa_v, 1),
            c_samp.astype(sd), stack(ml_s, 0).astype(sd), stack(ml_s, 1).astype(sd), stack(ml_s, 2))
```

```python
import functools
import math

import jax
import jax.numpy as jnp
from jax import lax
from jax.experimental import pallas as pl
from jax.experimental.pallas import tpu as pltpu

F32 = jnp.float32
BF16 = jnp.bfloat16

D_MODEL = 1024
N_META = 16
EPS = 1e-6
NEG_INF = -1e30
N_HEADS = 16
HEAD_DIM = 64
N_KV = 4
GROUP = 4
ATT_W = N_HEADS * HEAD_DIM
KV_W = N_KV * HEAD_DIM
WINDOW = 128
ATT_BLOCK = 128
N_BUCKETS = 32
MAX_DISTANCE = 128
M_INNER = 2048
M_HEADS = 4
M_HD = 512
CONV_W = 4
PAST_LEN = 8192

VMEM_LIMIT_BYTES = 56 * 1024 * 1024
PROMPT_CHUNK = 256
SMALL_CHUNK = 16
ATT_BLOCKS_PER_STEP = 4
ROW_TILE = 256
PROJ_T_ROW_TILE = 1024
SMALL_ROW_TILE = 128
SAMPLE_ATT_GROUP = 8


def _cparams(*sem):
    return pltpu.CompilerParams(dimension_semantics=sem, vmem_limit_bytes=VMEM_LIMIT_BYTES)


def _rms(x, w):
    ms = jnp.mean(x * x, axis=-1, keepdims=True)
    return x * lax.rsqrt(ms + EPS) * w


def _sigmoid(x):
    return 1.0 / (1.0 + jnp.exp(-x))


def _silu(x):
    return x * _sigmoid(x)


def _dot(a, b):
    return jnp.dot(a, b, preferred_element_type=F32)


def _dot_nt(a, b):
    return lax.dot_general(a, b, (((1,), (1,)), ((), ())), preferred_element_type=F32)


def _dot_tn(a, b):
    return lax.dot_general(a, b, (((0,), (0,)), ((), ())), preferred_element_type=F32)


def _norm_proj_kernel(x_ref, nw_ref, w_ref, o_ref):
    h = _rms(x_ref[...], nw_ref[...]).astype(BF16)
    o_ref[...] = _dot(h, w_ref[...]).astype(o_ref.dtype)


def _norm_proj(x, nw, w, tm, tn, out_dtype):
    rows, d = x.shape
    n = w.shape[1]
    return pl.pallas_call(
        _norm_proj_kernel,
        grid=(n // tn, rows // tm),
        in_specs=[pl.BlockSpec((tm, d), lambda j, i: (i, 0)),
                  pl.BlockSpec((1, d), lambda j, i: (0, 0)),
                  pl.BlockSpec((d, tn), lambda j, i: (0, j))],
        out_specs=pl.BlockSpec((tm, tn), lambda j, i: (i, j)),
        out_shape=jax.ShapeDtypeStruct((rows, n), out_dtype),
        compiler_params=_cparams("parallel", "parallel"),
        name="norm_proj",
    )(x, nw, w)


def _swa_proj_kernel(x_ref, nw_ref, w_ref, qn_ref, kn_ref, q_ref, k_ref, v_ref, g_ref):
    h = _rms(x_ref[...], nw_ref[...]).astype(BF16)
    proj = _dot(h, w_ref[...])
    qn = qn_ref[...]
    kn = kn_ref[...]
    for hd in range(N_HEADS):
        sl = proj[:, hd * HEAD_DIM:(hd + 1) * HEAD_DIM]
        q_ref[:, hd * HEAD_DIM:(hd + 1) * HEAD_DIM] = _rms(sl, qn).astype(q_ref.dtype)
    for hd in range(N_KV):
        sl = proj[:, ATT_W + hd * HEAD_DIM:ATT_W + (hd + 1) * HEAD_DIM]
        k_ref[:, hd * HEAD_DIM:(hd + 1) * HEAD_DIM] = _rms(sl, kn)
    v_ref[...] = proj[:, ATT_W + KV_W:ATT_W + 2 * KV_W]
    g_ref[...] = _silu(proj[:, ATT_W + 2 * KV_W:])


def _swa_proj(x, nw, w, qn, kn, tm):
    rows, d = x.shape
    n = w.shape[1]
    row = lambda i: (i, 0)
    fixed = lambda i: (0, 0)
    return pl.pallas_call(
        _swa_proj_kernel,
        grid=(rows // tm,),
        in_specs=[pl.BlockSpec((tm, d), row), pl.BlockSpec((1, d), fixed), pl.BlockSpec((d, n), fixed),
                  pl.BlockSpec((1, HEAD_DIM), fixed), pl.BlockSpec((1, HEAD_DIM), fixed)],
        out_specs=[pl.BlockSpec((tm, ATT_W), row), pl.BlockSpec((tm, KV_W), row),
                   pl.BlockSpec((tm, KV_W), row), pl.BlockSpec((tm, ATT_W), row)],
        out_shape=[jax.ShapeDtypeStruct((rows, ATT_W), BF16), jax.ShapeDtypeStruct((rows, KV_W), F32),
                   jax.ShapeDtypeStruct((rows, KV_W), F32), jax.ShapeDtypeStruct((rows, ATT_W), F32)],
        compiler_params=_cparams("parallel"),
        name="swa_proj",
    )(x, nw, w, qn, kn)


def _swa_proj_t_kernel(x_ref, nw_ref, wt_ref, qn_ref, kn_ref, qt_ref, k_ref, vt_ref, gt_ref):
    h = _rms(x_ref[...], nw_ref[...]).astype(BF16)
    pt = _dot_nt(wt_ref[...], h)

    def head_norm(r0, gain):
        sl = pt[r0:r0 + HEAD_DIM]
        ms = jnp.mean(sl * sl, axis=0, keepdims=True)
        return sl * lax.rsqrt(ms + EPS) * gain

    qn = qn_ref[...]
    kn = kn_ref[...]
    for hd in range(N_HEADS):
        qt_ref[hd * HEAD_DIM:(hd + 1) * HEAD_DIM, :] = head_norm(hd * HEAD_DIM, qn).astype(qt_ref.dtype)
    kt = jnp.concatenate([head_norm(ATT_W + hd * HEAD_DIM, kn) for hd in range(N_KV)], axis=0)
    k_ref[...] = kt.T
    vt_ref[...] = pt[ATT_W + KV_W:ATT_W + 2 * KV_W]
    gt_ref[...] = _silu(pt[ATT_W + 2 * KV_W:])


def _swa_proj_t(x, nw, wt, qn_col, kn_col, tm):
    rows, d = x.shape
    nt = wt.shape[0]
    row = lambda i: (i, 0)
    col = lambda i: (0, i)
    fixed = lambda i: (0, 0)
    return pl.pallas_call(
        _swa_proj_t_kernel,
        grid=(rows // tm,),
        in_specs=[pl.BlockSpec((tm, d), row), pl.BlockSpec((1, d), fixed), pl.BlockSpec((nt, d), fixed),
                  pl.BlockSpec((HEAD_DIM, 1), fixed), pl.BlockSpec((HEAD_DIM, 1), fixed)],
        out_specs=[pl.BlockSpec((ATT_W, tm), col), pl.BlockSpec((tm, KV_W), row),
                   pl.BlockSpec((KV_W, tm), col), pl.BlockSpec((ATT_W, tm), col)],
        out_shape=[jax.ShapeDtypeStruct((ATT_W, rows), BF16), jax.ShapeDtypeStruct((rows, KV_W), F32),
                   jax.ShapeDtypeStruct((KV_W, rows), F32), jax.ShapeDtypeStruct((ATT_W, rows), F32)],
        compiler_params=_cparams("parallel"),
        name="swa_proj_t",
    )(x, nw, wt, qn_col, kn_col)


def _resid_matmul_kernel(u_ref, w_ref, x_ref, o_ref):
    o_ref[...] = x_ref[...] + _dot(u_ref[...], w_ref[...])


def _gated_resid_matmul_kernel(a_ref, g_ref, w_ref, x_ref, o_ref):
    u = (a_ref[...] * g_ref[...]).astype(BF16)
    o_ref[...] = x_ref[...] + _dot(u, w_ref[...])


def _resid_matmul(u, w, x, tm, gate=None):
    rows, k = u.shape
    d = w.shape[1]
    row = lambda i: (i, 0)
    ins = [u] if gate is None else [u, gate]
    body = _resid_matmul_kernel if gate is None else _gated_resid_matmul_kernel
    return pl.pallas_call(
        body,
        grid=(rows // tm,),
        in_specs=[pl.BlockSpec((tm, k), row)] * len(ins)
                 + [pl.BlockSpec((k, d), lambda i: (0, 0)), pl.BlockSpec((tm, d), row)],
        out_specs=pl.BlockSpec((tm, d), row),
        out_shape=jax.ShapeDtypeStruct((rows, d), F32),
        compiler_params=_cparams("parallel"),
        name="resid_matmul",
    )(*ins, w, x)


def _bias_table_kernel(idx_ref, tab_ref, o_ref):
    idx = idx_ref[...]
    hit = [idx == b for b in range(N_BUCKETS)]
    for hd in range(N_HEADS):
        acc = jnp.zeros(idx.shape, F32)
        for b in range(N_BUCKETS):
            acc = jnp.where(hit[b], tab_ref[b, hd], acc)
        o_ref[hd] = acc


def _bias_table(idx, rel_bias):
    r, c = idx.shape
    return pl.pallas_call(
        _bias_table_kernel,
        in_specs=[pl.BlockSpec((r, c), lambda: (0, 0)), pl.BlockSpec(memory_space=pltpu.SMEM)],
        out_specs=pl.BlockSpec((N_HEADS, r, c), lambda: (0, 0, 0)),
        out_shape=jax.ShapeDtypeStruct((N_HEADS, r, c), F32),
        name="bias_table",
    )(idx, rel_bias)


def _rel_bucket(dist):
    n = jnp.maximum(dist, 0)
    max_exact = N_BUCKETS // 2
    nf = jnp.maximum(n, 1).astype(F32)
    large = max_exact + (jnp.log(nf / max_exact) / math.log(MAX_DISTANCE / max_exact)
                         * (N_BUCKETS - max_exact)).astype(jnp.int32)
    large = jnp.minimum(large, N_BUCKETS - 1)
    return jnp.where(n < max_exact, n, large)


def _mask_and_bucket(q_pos, k_pos, k_valid):
    dist = q_pos[:, None] - k_pos[None, :]
    mask = k_valid[None, :] & (dist >= 0) & ((dist <= WINDOW) | (k_pos[None, :] < N_META))
    return mask, _rel_bucket(dist)


def _swa_prompt_kernel(qt_ref, kc_ref, kp_ref, vtc_ref, vtp_ref, mk_ref, mvt_ref, gt_ref, x_ref,
                       bias_ref, mask_ref, sink_ref, w_ref, o_ref, att_ref):
    first = jnp.minimum(pl.program_id(1), 1)
    kc = kc_ref[...]
    vtc = vtc_ref[...]
    mk = mk_ref[...]
    mvt = mvt_ref[...]
    qt = qt_ref[...]
    for s in range(ATT_BLOCKS_PER_STEP):
        geo = first if s == 0 else 1
        own = slice(s * ATT_BLOCK, (s + 1) * ATT_BLOCK)
        k_prev = kp_ref[...] if s == 0 else kc[(s - 1) * ATT_BLOCK:s * ATT_BLOCK]
        vt_prev = vtp_ref[...] if s == 0 else vtc[:, (s - 1) * ATT_BLOCK:s * ATT_BLOCK]
        kband = jnp.concatenate([k_prev, kc[own], mk], axis=0).astype(BF16)
        vtband = jnp.concatenate([vt_prev, vtc[:, own], mvt], axis=1).astype(BF16)
        keep = mask_ref[geo] > 0.5
        for kv in range(N_KV):
            kk = kband[:, kv * HEAD_DIM:(kv + 1) * HEAD_DIM]
            vv = vtband[kv * HEAD_DIM:(kv + 1) * HEAD_DIM]
            qtg = jnp.concatenate(
                [qt[(kv * GROUP + g) * HEAD_DIM:(kv * GROUP + g + 1) * HEAD_DIM, own] for g in range(GROUP)], axis=1)
            logits = _dot(kk, qtg) + bias_ref[geo, kv]
            logits = jnp.where(keep, logits, NEG_INF)
            sk = sink_ref[kv]
            m = jnp.maximum(jnp.max(logits, axis=0, keepdims=True), sk)
            p = jnp.exp(logits - m)
            den = jnp.sum(p, axis=0, keepdims=True) + jnp.exp(sk - m)
            otg = _dot(vv, p.astype(BF16)) * (1.0 / den)
            for g in range(GROUP):
                hd = kv * GROUP + g
                att_ref[hd * HEAD_DIM:(hd + 1) * HEAD_DIM, own] = otg[:, g * ATT_BLOCK:(g + 1) * ATT_BLOCK]
    ut = (att_ref[...] * gt_ref[...]).astype(BF16)
    o_ref[...] = x_ref[...] + _dot_tn(ut, w_ref[...])


def _swa_prompt(qt, kn, vt, sgt, x, mk, mvt, bias_t, mask_t, sink_t, w_out, batch, n_blk):
    nb = ATT_BLOCKS_PER_STEP
    n_step = n_blk // nb
    tq = nb * ATT_BLOCK
    cur = lambda b, i: (b * n_step + i, 0)
    prev = lambda b, i: (b * n_blk + jnp.maximum(nb * i - 1, 0), 0)
    cur_t = lambda b, i: (0, b * n_step + i)
    prev_t = lambda b, i: (0, b * n_blk + jnp.maximum(nb * i - 1, 0))
    meta = lambda b, i: (b, 0, 0)
    tk = N_META + 2 * ATT_BLOCK
    nq = GROUP * ATT_BLOCK
    rows = kn.shape[0]
    return pl.pallas_call(
        _swa_prompt_kernel,
        grid=(batch, n_step),
        in_specs=[pl.BlockSpec((ATT_W, tq), cur_t),
                  pl.BlockSpec((tq, KV_W), cur), pl.BlockSpec((ATT_BLOCK, KV_W), prev),
                  pl.BlockSpec((KV_W, tq), cur_t), pl.BlockSpec((KV_W, ATT_BLOCK), prev_t),
                  pl.BlockSpec((None, N_META, KV_W), meta), pl.BlockSpec((None, KV_W, N_META), meta),
                  pl.BlockSpec((ATT_W, tq), cur_t), pl.BlockSpec((tq, D_MODEL), cur),
                  pl.BlockSpec((2, N_KV, tk, nq), lambda b, i: (0, 0, 0, 0)),
                  pl.BlockSpec((2, tk, nq), lambda b, i: (0, 0, 0)),
                  pl.BlockSpec((N_KV, 1, nq), lambda b, i: (0, 0, 0)),
                  pl.BlockSpec((ATT_W, D_MODEL), lambda b, i: (0, 0))],
        out_specs=pl.BlockSpec((tq, D_MODEL), cur),
        out_shape=jax.ShapeDtypeStruct((rows, D_MODEL), F32),
        scratch_shapes=[pltpu.VMEM((ATT_W, tq), F32)],
        compiler_params=_cparams("parallel", "parallel"),
        name="swa_prompt",
    )(qt, kn, kn, vt, vt, mk, mvt, sgt, x, bias_t, mask_t, sink_t, w_out)


def _swa_small_kernel(n_seg, *refs):
    q_ref = refs[0]
    k_refs = refs[1:1 + n_seg]
    v_refs = refs[1 + n_seg:1 + 2 * n_seg]
    bias_ref, mask_ref, sink_ref, o_ref = refs[1 + 2 * n_seg:]
    keep = mask_ref[...] > 0.5
    scale = HEAD_DIM ** -0.5
    kall = jnp.concatenate([r[...] for r in k_refs], axis=0).astype(BF16) if n_seg > 1 else k_refs[0][...].astype(BF16)
    vall = jnp.concatenate([r[...] for r in v_refs], axis=0).astype(BF16) if n_seg > 1 else v_refs[0][...].astype(BF16)
    for kv in range(N_KV):
        kk = kall[:, kv * HEAD_DIM:(kv + 1) * HEAD_DIM]
        vv = vall[:, kv * HEAD_DIM:(kv + 1) * HEAD_DIM]
        logits = _dot_nt(q_ref[kv], kk) * scale + bias_ref[kv]
        logits = jnp.where(keep, logits, NEG_INF)
        sk = sink_ref[kv]
        m = jnp.maximum(jnp.max(logits, axis=-1, keepdims=True), sk)
        p = jnp.exp(logits - m)
        den = jnp.sum(p, axis=-1, keepdims=True) + jnp.exp(sk - m)
        o_ref[kv] = _dot(p.astype(BF16), vv) / den


def _swa_small(q, k_segs, v_segs, seg_len, seg_first, bias, mask, sink, group, rq):
    rows = q.shape[1]
    n_seg = len(k_segs)
    tk = group * sum(seg_len)
    seg_specs = [pl.BlockSpec((group * n, KV_W), functools.partial(lambda o, i: (o + i, 0), f // group))
                 for n, f in zip(seg_len, seg_first)]
    return pl.pallas_call(
        functools.partial(_swa_small_kernel, n_seg),
        grid=(rows // (group * rq),),
        in_specs=[pl.BlockSpec((N_KV, group * rq, HEAD_DIM), lambda i: (0, i, 0))] + seg_specs + seg_specs
                 + [pl.BlockSpec((N_KV, group * rq, tk), lambda i: (0, 0, 0)),
                    pl.BlockSpec((group * rq, tk), lambda i: (0, 0)),
                    pl.BlockSpec((N_KV, group * rq, 1), lambda i: (0, 0, 0))],
        out_specs=pl.BlockSpec((N_KV, group * rq, HEAD_DIM), lambda i: (0, i, 0)),
        out_shape=jax.ShapeDtypeStruct((N_KV, rows, HEAD_DIM), F32),
        compiler_params=_cparams("parallel"),
        name="swa_small",
    )(q, *k_segs, *v_segs, bias, mask, sink)


def _to_group_rows(a, nb, t):
    a = a.reshape(nb, t, N_KV, GROUP, HEAD_DIM)
    return jnp.transpose(a, (2, 0, 3, 1, 4)).reshape(N_KV, nb * GROUP * t, HEAD_DIM)


def _from_group_rows(a, nb, t):
    a = a.reshape(N_KV, nb, GROUP, t, HEAD_DIM)
    return jnp.transpose(a, (1, 3, 0, 2, 4)).reshape(nb * t, ATT_W)


def _bias_group_rows(b, t):
    return b.reshape(N_KV, GROUP * t, b.shape[-1])


def _block_diag_keys(a, group, seg_len):
    eye = jnp.eye(group, dtype=a.dtype)
    out, o = [], 0
    for n in seg_len:
        blk = a[..., o:o + n]
        o += n
        big = eye[:, None, :, None] * blk[..., None, :, None, :]
        out.append(big.reshape(a.shape[:-2] + (group * a.shape[-2], group * n)))
    return jnp.concatenate(out, axis=-1)


def _conv_silu(xs, cw_ref, cb_ref):
    cw = cw_ref[...]
    acc = xs[0] * cw[0:1]
    for w in range(1, CONV_W):
        acc = acc + xs[w] * cw[w:w + 1]
    return _silu(acc + cb_ref[...])


def _mlstm_in_kernel(x_ref, nw_ref, w_ref, cs_ref, cw_ref, cb_ref, wq_ref, wk_ref, wv_ref, wg_ref, bg_ref,
                     q_ref, k_ref, v_ref, gt_ref, xc_ref, zo_ref, tail_ref):
    i = pl.program_id(1)
    h = _rms(x_ref[...], nw_ref[...]).astype(BF16)
    zo_ref[...] = _dot(h, w_ref[:, M_INNER:]).astype(zo_ref.dtype)
    xm = _dot(h, w_ref[:, :M_INNER])
    tm = xm.shape[0]
    halo = jnp.where(i == 0, cs_ref[...], tail_ref[...])
    full = jnp.concatenate([halo, xm], axis=0)
    n = tm + 8
    xs = [pltpu.roll(full, n - (8 - (CONV_W - 1) + w), 0)[:tm] for w in range(CONV_W - 1)] + [xm]
    xcb = _conv_silu(xs, cw_ref, cb_ref).astype(BF16)
    xc_ref[...] = xcb
    tail_ref[...] = xm[tm - 8:]
    _qkv_matmuls(xcb, xm.astype(BF16), wq_ref, wk_ref, wv_ref, wg_ref, bg_ref, q_ref, k_ref, v_ref, gt_ref)


def _mlstm_in(x3, nw, w, cs, conv_wts, wts, tm):
    b, t, d = x3.shape
    cur = lambda bi, i: (bi, i, 0)
    seq = lambda bi, i: (bi, 0, 0)
    fixed = lambda bi, i: (0, 0)
    once = pl.Buffered(1)
    tok = pl.BlockSpec((None, tm, M_INNER), cur)
    return pl.pallas_call(
        _mlstm_in_kernel,
        grid=(b, t // tm),
        in_specs=[pl.BlockSpec((None, tm, d), cur), pl.BlockSpec((1, d), fixed),
                  pl.BlockSpec((d, 3 * M_INNER), fixed, pipeline_mode=once),
                  pl.BlockSpec((None, 8, M_INNER), seq),
                  pl.BlockSpec((CONV_W, M_INNER), fixed), pl.BlockSpec((1, M_INNER), fixed)]
                 + _qkv_weight_specs(fixed, lambda bi, i: (0, 0, 0), once),
        out_specs=[tok, tok, tok, pl.BlockSpec((None, tm, 2 * M_HEADS), cur), tok,
                   pl.BlockSpec((None, tm, 2 * M_INNER), cur), pl.BlockSpec((None, 8, M_INNER), seq)],
        out_shape=[jax.ShapeDtypeStruct((b, t, M_INNER), BF16)] * 3
                  + [jax.ShapeDtypeStruct((b, t, 2 * M_HEADS), F32), jax.ShapeDtypeStruct((b, t, M_INNER), BF16),
                     jax.ShapeDtypeStruct((b, t, 2 * M_INNER), BF16), jax.ShapeDtypeStruct((b, 8, M_INNER), F32)],
        compiler_params=_cparams("parallel", "arbitrary"),
        name="mlstm_in",
    )(x3, nw, w, cs, *conv_wts, *wts)


def _qkv_matmuls(xcb, xmb, wq_ref, wk_ref, wv_ref, wg_ref, bg_ref, q_ref, k_ref, v_ref, gt_ref):
    gates = bg_ref[...]
    for h in range(M_HEADS):
        sl = slice(h * M_HD, (h + 1) * M_HD)
        qh = _dot(xcb[:, sl], wq_ref[h]).astype(BF16)
        kh = (_dot(xcb[:, sl], wk_ref[h]) * (M_HD ** -0.5)).astype(BF16)
        vh = _dot(xmb[:, sl], wv_ref[h]).astype(BF16)
        q_ref[:, sl] = qh
        k_ref[:, sl] = kh
        v_ref[:, sl] = vh
        gates = (gates + _dot(qh, wg_ref[h * M_HD:(h + 1) * M_HD])
                 + _dot(kh, wg_ref[M_INNER + h * M_HD:M_INNER + (h + 1) * M_HD])
                 + _dot(vh, wg_ref[2 * M_INNER + h * M_HD:2 * M_INNER + (h + 1) * M_HD]))
    lane = lax.broadcasted_iota(jnp.int32, gates.shape, 1)
    log_f = jnp.minimum(gates, 0.0) - jnp.log1p(jnp.exp(-jnp.abs(gates)))
    gt_ref[...] = jnp.where(lane < M_HEADS, gates, log_f)


def _qkv_shift_kernel(x0_ref, x1_ref, x2_ref, x3_ref, cw_ref, cb_ref, *rest):
    xs = [x0_ref[...], x1_ref[...], x2_ref[...], x3_ref[...]]
    xc = _conv_silu(xs, cw_ref, cb_ref)
    xc_ref = rest[-1]
    xc_ref[...] = xc
    _qkv_matmuls(xc.astype(BF16), xs[CONV_W - 1].astype(BF16), *rest[:-1])


def _qkv_weight_specs(fixed2, fixed3, mode=None):
    return [pl.BlockSpec((M_HEADS, M_HD, M_HD), fixed3, pipeline_mode=mode),
            pl.BlockSpec((M_HEADS, M_HD, M_HD), fixed3, pipeline_mode=mode),
            pl.BlockSpec((M_HEADS, M_HD, M_HD), fixed3, pipeline_mode=mode),
            pl.BlockSpec((3 * M_INNER, 2 * M_HEADS), fixed2, pipeline_mode=mode),
            pl.BlockSpec((1, 2 * M_HEADS), fixed2)]


def _qkv_small(xs, conv_wts, wts, tm):
    rows = xs[0].shape[0]
    row = lambda i: (i, 0)
    fixed = lambda i: (0, 0)
    outs = ([jax.ShapeDtypeStruct((rows, M_INNER), BF16)] * 3 + [jax.ShapeDtypeStruct((rows, 2 * M_HEADS), F32)]
            + [jax.ShapeDtypeStruct((rows, M_INNER), F32)])
    return pl.pallas_call(
        _qkv_shift_kernel,
        grid=(rows // tm,),
        in_specs=[pl.BlockSpec((tm, M_INNER), row)] * CONV_W
                 + [pl.BlockSpec((CONV_W, M_INNER), fixed), pl.BlockSpec((1, M_INNER), fixed)]
                 + _qkv_weight_specs(fixed, lambda i: (0, 0, 0)),
        out_specs=[pl.BlockSpec((tm, M_INNER), row)] * 3 + [pl.BlockSpec((tm, 2 * M_HEADS), row)]
                  + [pl.BlockSpec((tm, M_INNER), row)],
        out_shape=outs,
        compiler_params=_cparams("parallel"),
        name="mlstm_qkv_small",
    )(*xs, *conv_wts, *wts)


def _chunk_kernel(seg, hps, aliased, fused, q_ref, k_ref, v_ref, xc_ref, z_ref, op_ref, gc_ref, gr_ref, c0_ref, n0_ref,
                  m0_ref, hn_ref, sk_ref, *rest):
    if fused:
        x_ref, w_ref = rest[:2]
        rest = rest[2:]
    if aliased:
        rest = rest[1:]
    out_ref, c_ref, n_ref, m_ref = rest[:4]
    u_ref = rest[4] if fused else out_ref
    h_blk = pl.program_id(1)
    c_idx = pl.program_id(2)
    L = q_ref.shape[0]
    nseg = L // seg
    hi = lax.Precision.HIGHEST

    @pl.when(c_idx == 0)
    def _():
        c_ref[...] = c0_ref[...]
        n_ref[...] = n0_ref[...]
        m_ref[...] = m0_ref[...]

    gc = gc_ref[...]
    gr = gr_ref[...]
    col_c = lax.broadcasted_iota(jnp.int32, gc.shape, 1)
    row_r = lax.broadcasted_iota(jnp.int32, gr.shape, 0)
    t_i = lax.broadcasted_iota(jnp.int32, (L, L), 0)
    j_i = lax.broadcasted_iota(jnp.int32, (L, L), 1)
    causal = j_i <= t_i
    upper = t_i <= j_i
    if nseg > 1:
        sid_c = lax.broadcasted_iota(jnp.int32, (L, 1), 0) // seg
        in_seg = [sid_c == s for s in range(nseg)]
        pick = lambda vals: sum(jnp.where(in_seg[s], vals[s], 0.0) for s in range(nseg))
        same = (t_i // seg) == (j_i // seg)
        causal = causal & same
        upper = upper & same
    tril = jnp.where(causal, 1.0, 0.0).astype(F32)
    triu = jnp.where(upper, 1.0, 0.0).astype(F32)
    lf_c = jnp.where(col_c >= M_HEADS, gc, 0.0)
    lf_r = jnp.where(row_r >= M_HEADS, gr, 0.0)
    b_all_c = jnp.dot(tril, lf_c, precision=hi, preferred_element_type=F32)
    b_all_r = jnp.dot(lf_r, triu, precision=hi, preferred_element_type=F32)
    if nseg > 1:
        g_all = jnp.dot(jnp.where(same, 1.0, 0.0).astype(F32), lf_c, precision=hi, preferred_element_type=F32)

    for hh in range(hps):
        h_idx = h_blk * hps + hh
        cols = slice(hh * M_HD, (hh + 1) * M_HD)
        sel_c = lambda a, c: jnp.sum(jnp.where(col_c == c, a, 0.0), axis=1, keepdims=True)
        sel_r = lambda a, r: jnp.sum(jnp.where(row_r == r, a, 0.0), axis=0, keepdims=True)
        i_c, i_r = sel_c(gc, h_idx), sel_r(gr, h_idx)
        b_c, b_r = sel_c(b_all_c, M_HEADS + h_idx), sel_r(b_all_r, M_HEADS + h_idx)
        if nseg > 1:
            g = sel_c(g_all, M_HEADS + h_idx)
            m_old = [m_ref[s, hh][:, 0:1] for s in range(nseg)]
            m_prev = pick(m_old)
        else:
            g = b_c[L - 1:L, :]
            m_prev = m_ref[0, hh][:, 0:1]

        log_d = jnp.where(causal, b_c + (i_r - b_r), -jnp.inf)
        inter = b_c + m_prev
        m_t = jnp.maximum(inter, jnp.max(log_d, axis=-1, keepdims=True))
        w_intra = jnp.exp(log_d - m_t)
        w_inter = jnp.exp(inter - m_t)
        q = q_ref[:, cols]
        k = k_ref[:, cols]
        v = v_ref[:, cols]
        s_mat = _dot_nt(q, k) * w_intra
        qf = q.astype(F32)
        if nseg > 1:
            q_c = pick([_dot(q, c_ref[s, hh].astype(BF16)) for s in range(nseg)])
            q_n = pick([jnp.sum(qf * n_ref[s, hh], axis=-1, keepdims=True) for s in range(nseg)])
        else:
            q_c = _dot(q, c_ref[0, hh].astype(BF16))
            q_n = jnp.sum(qf * n_ref[0, hh], axis=-1, keepdims=True)
        num = _dot(s_mat.astype(BF16), v) + w_inter * q_c
        den = jnp.sum(s_mat, axis=-1, keepdims=True) + w_inter * q_n
        h = num * (1.0 / jnp.maximum(jnp.abs(den), jnp.exp(-m_t)))

        lw = (g - b_c) + i_c
        kf = k.astype(F32)
        if nseg > 1:
            g_s = [jnp.max(jnp.where(in_seg[s], g, -jnp.inf), axis=0, keepdims=True) for s in range(nseg)]
            m_new = [jnp.maximum(g_s[s] + m_old[s],
                                 jnp.max(jnp.where(in_seg[s], lw, -jnp.inf), axis=0, keepdims=True))
                     for s in range(nseg)]
            kw = kf * jnp.exp(lw - pick(m_new))
            for s in range(nseg):
                decay = jnp.exp(g_s[s] + m_old[s] - m_new[s])
                kw_s = jnp.where(in_seg[s], kw, 0.0)
                c_ref[s, hh] = decay * c_ref[s, hh] + _dot_tn(kw_s.astype(BF16), v)
                n_ref[s, hh] = decay * n_ref[s, hh] + jnp.sum(kw_s, axis=0, keepdims=True)
                m_ref[s, hh] = jnp.broadcast_to(m_new[s], m_ref.shape[2:])
        else:
            m_new = jnp.maximum(g + m_prev, jnp.max(lw, axis=0, keepdims=True))
            decay = jnp.exp(g + m_prev - m_new)
            kw = kf * jnp.exp(lw - m_new)
            c_ref[0, hh] = decay * c_ref[0, hh] + _dot_tn(kw.astype(BF16), v)
            n_ref[0, hh] = decay * n_ref[0, hh] + jnp.sum(kw, axis=0, keepdims=True)
            m_ref[0, hh] = jnp.broadcast_to(m_new, m_ref.shape[2:])

        mu = jnp.mean(h, axis=-1, keepdims=True)
        hc = h - mu
        var = jnp.mean(hc * hc, axis=-1, keepdims=True)
        h_out = _sigmoid(op_ref[:, cols].astype(F32)) * (hc * lax.rsqrt(var + EPS) * hn_ref[:, cols])
        u = (h_out + sk_ref[:, cols] * xc_ref[:, cols].astype(F32)) * _silu(z_ref[:, cols].astype(F32))
        u_ref[:, cols] = u.astype(u_ref.dtype)
    if fused:
        out_ref[...] = x_ref[...] + _dot(u_ref[...], w_ref[...])


def _mlstm_chunk(q, k, v, xc, zo, gates, c0, n0, m0, lyr_in, head_norm, skip, chunk, seg, hps,
                 c_acc=None, lyr_out=0, n_lyr_out=1, x=None, w_out=None):
    nb, t, _ = q.shape
    nc = t // chunk
    nseg = chunk // seg
    assert nseg == 1 or nc == 1
    n_seq = nb * nseg
    n_hblk = M_HEADS // hps
    w = hps * M_HD
    gates_t = jnp.swapaxes(gates, 1, 2)
    tok = lambda bi, h, c: (bi, c, h)
    hd_blk = pl.BlockSpec((None, chunk, w), tok)
    st_in = lambda bi, h, c: (lyr_in, bi, h, 0, 0)
    st_out = lambda bi, h, c: (0, bi, h, 0, 0)
    c_out = lambda bi, h, c: (lyr_out, bi, h, 0, 0)
    head_row = pl.BlockSpec((1, w), lambda bi, h, c: (0, h))
    in_specs = [hd_blk, hd_blk, hd_blk, hd_blk,
                hd_blk,
                pl.BlockSpec((None, chunk, w), lambda bi, h, c: (bi, c, n_hblk + h)),
                pl.BlockSpec((None, chunk, 2 * M_HEADS), lambda bi, h, c: (bi, c, 0)),
                pl.BlockSpec((None, 2 * M_HEADS, chunk), lambda bi, h, c: (bi, 0, c)),
                pl.BlockSpec((None, nseg, hps, M_HD, M_HD), st_in),
                pl.BlockSpec((None, nseg, hps, 1, M_HD), st_in),
                pl.BlockSpec((None, nseg, hps, 1, 128), st_in),
                head_row, head_row]
    args = [q, k, v, xc, zo, zo, gates, gates_t, c0, n0, m0, head_norm, skip]
    fused = x is not None
    first_out = (hd_blk, jax.ShapeDtypeStruct((nb, t, M_INNER), BF16))
    scratch = []
    if fused:
        assert hps == M_HEADS
        d = w_out.shape[1]
        x_blk = pl.BlockSpec((None, chunk, d), lambda bi, h, c: (bi, c, 0))
        in_specs += [x_blk, pl.BlockSpec((M_INNER, d), lambda bi, h, c: (0, 0))]
        args += [x, w_out]
        first_out = (x_blk, jax.ShapeDtypeStruct((nb, t, d), F32))
        scratch = [pltpu.VMEM((chunk, M_INNER), BF16)]
    aliases = {}
    if c_acc is not None:
        in_specs.append(pl.BlockSpec(memory_space=pl.ANY))
        aliases = {len(args): 1}
        args.append(c_acc)
    return pl.pallas_call(
        functools.partial(_chunk_kernel, seg, hps, c_acc is not None, fused),
        grid=(nb, n_hblk, nc),
        in_specs=in_specs,
        out_specs=[first_out[0],
                   pl.BlockSpec((None, nseg, hps, M_HD, M_HD), c_out),
                   pl.BlockSpec((None, nseg, hps, 1, M_HD), st_out),
                   pl.BlockSpec((None, nseg, hps, 1, 128), st_out)],
        out_shape=[first_out[1],
                   jax.ShapeDtypeStruct((n_lyr_out, n_seq, M_HEADS, M_HD, M_HD), F32),
                   jax.ShapeDtypeStruct((1, n_seq, M_HEADS, 1, M_HD), F32),
                   jax.ShapeDtypeStruct((1, n_seq, M_HEADS, 1, 128), F32)],
        scratch_shapes=scratch,
        input_output_aliases=aliases,
        compiler_params=_cparams("parallel", "parallel", "arbitrary"),
        name="mlstm_chunk",
    )(*args)


def kernel(x_prompt, x_sample, cache_swa_k, cache_swa_v, cache_swa_meta_k, cache_swa_meta_v, state_mlstm_c, state_mlstm_n, state_mlstm_m, state_mlstm_conv, meta_tokens, rel_bias, norm_w, swa_w_in, swa_q_norm, swa_k_norm, swa_sinks, swa_w_out, mlstm_w_in, mlstm_conv_w, mlstm_conv_b, mlstm_wq, mlstm_wk, mlstm_wv, mlstm_w_gates, mlstm_b_gates, mlstm_head_norm, mlstm_skip, mlstm_w_out):
    B, T, D = x_prompt.shape
    DB, DT, _ = x_sample.shape
    depth = norm_w.shape[0]
    n_buf = cache_swa_k.shape[2]
    n_blk = T // ATT_BLOCK
    n_real, n_samp, n_meta = B * T, DB * DT, B * N_META
    n_small = -(-(n_samp + n_meta) // SMALL_ROW_TILE) * SMALL_ROW_TILE
    pad_small = n_small - n_samp - n_meta
    s_meta = slice(n_samp, n_samp + n_meta)

    def small_rows(samp, meta):
        return jnp.concatenate([samp, meta, jnp.zeros((pad_small,) + samp.shape[1:], samp.dtype)], axis=0)

    xr = x_prompt.reshape(n_real, D)
    xs = small_rows(x_sample.reshape(n_samp, D),
                    jnp.broadcast_to(meta_tokens.astype(x_prompt.dtype)[None], (B, N_META, D)).reshape(n_meta, D))

    ar = lambda n: jnp.arange(n, dtype=jnp.int32)
    meta_pos = ar(N_META)
    geo = []
    for blk in (0, 1):
        q_pos = N_META + blk * ATT_BLOCK + ar(ATT_BLOCK)
        band_pos = N_META + (blk - 1) * ATT_BLOCK + ar(2 * ATT_BLOCK)
        k_pos = jnp.concatenate([meta_pos, band_pos])
        k_valid = jnp.concatenate([jnp.ones((N_META,), bool), band_pos >= N_META])
        geo.append(_mask_and_bucket(q_pos, k_pos, k_valid))
    tk_p = N_META + 2 * ATT_BLOCK
    mask_m, buck_m = _mask_and_bucket(meta_pos, meta_pos, jnp.ones((N_META,), bool))
    buf_pos = PAST_LEN - n_buf + ar(n_buf)
    new_pos = PAST_LEN + ar(DT)
    k_pos_s = jnp.concatenate([meta_pos, buf_pos, new_pos])
    k_valid_s = jnp.concatenate([jnp.ones((N_META,), bool), buf_pos >= N_META, jnp.ones((DT,), bool)])
    mask_s, buck_s = _mask_and_bucket(new_pos, k_pos_s, k_valid_s)
    tk_s = N_META + n_buf + DT

    def place(a, rows):
        return jnp.pad(a, ((0, rows - a.shape[0]), (0, tk_p - a.shape[1])))

    idx_all = jnp.concatenate([geo[0][1], geo[1][1], place(buck_m, N_META), place(buck_s, 8)], axis=0)
    bias_all = _bias_table(idx_all, rel_bias.astype(F32))
    bias_p = jnp.transpose(bias_all[:, :2 * ATT_BLOCK].reshape(N_HEADS, 2, ATT_BLOCK, tk_p), (1, 0, 2, 3))
    bias_p = bias_p.reshape(2, N_KV, GROUP * ATT_BLOCK, tk_p)
    mask_p = jnp.stack([jnp.tile(geo[0][0], (GROUP, 1)), jnp.tile(geo[1][0], (GROUP, 1))]).astype(F32)
    key_major = lambda a: jnp.swapaxes(jnp.concatenate([a[..., N_META:], a[..., :N_META]], axis=-1), -1, -2)
    bias_pt, mask_pt = key_major(bias_p), key_major(mask_p)
    o0 = 2 * ATT_BLOCK
    seg_m, seg_s = (N_META,), (N_META, n_buf, DT)
    grp_s = SAMPLE_ATT_GROUP
    bias_m = _block_diag_keys(_bias_group_rows(bias_all[:, o0:o0 + N_META, :N_META], N_META), B, seg_m)
    mask_mg = _block_diag_keys(jnp.tile(mask_m, (GROUP, 1)).astype(F32), B, seg_m)
    o1 = o0 + N_META
    bias_s = _block_diag_keys(_bias_group_rows(bias_all[:, o1:o1 + DT, :tk_s], DT), grp_s, seg_s)
    mask_sg = _block_diag_keys(jnp.tile(mask_s, (GROUP, 1)).astype(F32), grp_s, seg_s)

    def sink_rows(sinks, t, n_seq=1):
        rows = jnp.repeat(sinks.astype(F32).reshape(N_KV, GROUP, 1), t, axis=2).reshape(N_KV, GROUP * t, 1)
        return jnp.tile(rows, (1, n_seq, 1))

    n_swa = cache_swa_k.shape[0]
    ck_all = cache_swa_k.astype(F32).reshape(n_swa * DB * n_buf, KV_W)
    cv_all = cache_swa_v.astype(F32).reshape(n_swa * DB * n_buf, KV_W)
    cmk_all = cache_swa_meta_k.astype(F32).reshape(n_swa * DB * N_META, KV_W)
    cmv_all = cache_swa_meta_v.astype(F32).reshape(n_swa * DB * N_META, KV_W)
    swa_p, swa_s, ml_p, ml_s = [], [], [], []
    c_samp = None
    assert n_samp % SMALL_CHUNK == 0 and SMALL_CHUNK % DT == 0
    for layer in range(depth):
        j = layer // 2
        nw = norm_w[layer].astype(F32).reshape(1, D)
        if layer % 2 == 0:
            w_in = swa_w_in[j].astype(BF16)
            w_out = swa_w_out[j].astype(BF16)
            qn = swa_q_norm[j].astype(F32).reshape(1, HEAD_DIM)
            kn = swa_k_norm[j].astype(F32).reshape(1, HEAD_DIM)
            qn_scaled = (qn * (HEAD_DIM ** -0.5)).reshape(HEAD_DIM, 1)
            qt_r, k_r, vt_r, gt_r = _swa_proj_t(xr, nw, w_in.T, qn_scaled, kn.reshape(HEAD_DIM, 1), PROJ_T_ROW_TILE)
            q_s, k_s, v_s, g_s = _swa_proj(xs, nw, w_in, qn, kn, SMALL_ROW_TILE)
            mk = k_s[s_meta].reshape(B, N_META, KV_W)
            mv = v_s[s_meta].reshape(B, N_META, KV_W)
            xr = _swa_prompt(qt_r, k_r, vt_r, gt_r, xr, mk, jnp.swapaxes(mv, 1, 2), bias_pt, mask_pt,
                             jnp.swapaxes(sink_rows(swa_sinks[j], ATT_BLOCK), 1, 2), w_out, B, n_blk)
            k_new = k_s[:n_samp]
            v_new = v_s[:n_samp]
            o_samp = _swa_small(_to_group_rows(q_s[:n_samp], DB, DT), [cmk_all, ck_all, k_new], [cmv_all, cv_all, v_new],
                                seg_s, (j * DB, j * DB, 0),
                                bias_s, mask_sg, sink_rows(swa_sinks[j], DT, grp_s), grp_s, GROUP * DT)
            o_meta = _swa_small(_to_group_rows(q_s[s_meta], B, N_META), [k_s[s_meta]], [v_s[s_meta]], seg_m, (0,),
                                bias_m, mask_mg, sink_rows(swa_sinks[j], N_META, B), B, GROUP * N_META)
            o_small = small_rows(_from_group_rows(o_samp, DB, DT), _from_group_rows(o_meta, B, N_META))
            xs = _resid_matmul(o_small, w_out, xs, SMALL_ROW_TILE, gate=g_s)
            n_keep = min(WINDOW, T + N_META)
            last = lambda a: a.reshape(B, T, KV_W)[:, T - n_keep:].reshape(B, n_keep, N_KV, HEAD_DIM)
            v_last = jnp.stack([vt_r[:, (b + 1) * T - n_keep:(b + 1) * T].T for b in range(B)])
            swa_p.append((last(k_r), v_last.reshape(B, n_keep, N_KV, HEAD_DIM),
                          mk.reshape(B, N_META, N_KV, HEAD_DIM), mv.reshape(B, N_META, N_KV, HEAD_DIM)))
            swa_s.append((k_new.reshape(DB, DT, N_KV, HEAD_DIM), v_new.reshape(DB, DT, N_KV, HEAD_DIM)))
        else:
            w_in = mlstm_w_in[j].astype(BF16)
            w_out = mlstm_w_out[j].astype(BF16)
            conv_wts = (mlstm_conv_w[j].astype(F32), mlstm_conv_b[j].astype(F32).reshape(1, M_INNER))
            wts = (mlstm_wq[j].astype(BF16), mlstm_wk[j].astype(BF16), mlstm_wv[j].astype(BF16),
                   mlstm_w_gates[j].astype(BF16), mlstm_b_gates[j].astype(F32).reshape(1, 2 * M_HEADS))
            hn = mlstm_head_norm[j].astype(F32).reshape(1, M_INNER)
            sk = mlstm_skip[j].astype(F32).reshape(1, M_INNER)
            p_s = _norm_proj(xs, nw, w_in, SMALL_ROW_TILE, M_INNER, F32)
            xm_samp = p_s[:n_samp, :M_INNER].reshape(DB, DT, M_INNER)
            xm_meta = p_s[s_meta, :M_INNER].reshape(B, N_META, M_INNER)
            xpad_s = jnp.concatenate([state_mlstm_conv[j].astype(F32), xm_samp], axis=1)
            xpad_m = jnp.concatenate([jnp.zeros((B, CONV_W - 1, M_INNER), F32), xm_meta], axis=1)
            shifted = [small_rows(xpad_s[:, w:w + DT].reshape(n_samp, M_INNER),
                                  xpad_m[:, w:w + N_META].reshape(n_meta, M_INNER)) for w in range(CONV_W)]
            q_s, k_s, v_s, gt_s, xc_s = _qkv_small(shifted, conv_wts, wts, SMALL_ROW_TILE)
            zo_s = p_s[:, M_INNER:]
            cs_real = jnp.pad(xpad_m[:, N_META:], ((0, 0), (8 - (CONV_W - 1), 0), (0, 0)))
            q_r, k_r, v_r, gt_r, xc_r, zo_r, xm_tail = _mlstm_in(xr.reshape(B, T, D), nw, w_in, cs_real,
                                                                 conv_wts, wts, ROW_TILE)

            to3 = lambda a, sl, b, t: a[sl].reshape(b, t, a.shape[-1])
            zc = jnp.zeros((1, B, M_HEADS, M_HD, M_HD), F32)
            zn = jnp.zeros((1, B, M_HEADS, 1, M_HD), F32)
            zm = jnp.zeros((1, B, M_HEADS, 1, 128), F32)
            u_m, c_m, n_m, m_m = _mlstm_chunk(
                to3(q_s, s_meta, B, N_META), to3(k_s, s_meta, B, N_META), to3(v_s, s_meta, B, N_META),
                to3(xc_s, s_meta, B, N_META), to3(zo_s, s_meta, B, N_META), to3(gt_s, s_meta, B, N_META),
                zc, zn, zm, 0, hn, sk, N_META, N_META, M_HEADS)
            x_new, c_r, n_r, m_r = _mlstm_chunk(q_r, k_r, v_r, xc_r, zo_r, gt_r, c_m, n_m, m_m, 0, hn, sk,
                                                PROMPT_CHUNK, PROMPT_CHUNK, M_HEADS,
                                                x=xr.reshape(B, T, D), w_out=w_out)
            s_samp = slice(0, n_samp)
            nb_s = n_samp // SMALL_CHUNK
            u_s, c_samp, n_s, m_s = _mlstm_chunk(
                to3(q_s, s_samp, nb_s, SMALL_CHUNK), to3(k_s, s_samp, nb_s, SMALL_CHUNK),
                to3(v_s, s_samp, nb_s, SMALL_CHUNK), to3(xc_s, s_samp, nb_s, SMALL_CHUNK),
                to3(zo_s, s_samp, nb_s, SMALL_CHUNK), to3(gt_s, s_samp, nb_s, SMALL_CHUNK),
                state_mlstm_c.astype(F32), state_mlstm_n.astype(F32)[:, :, :, None, :],
                jnp.broadcast_to(state_mlstm_m.astype(F32)[..., None, None], state_mlstm_m.shape + (1, 128)),
                j, hn, sk, SMALL_CHUNK, DT, 1, c_acc=c_samp, lyr_out=j, n_lyr_out=state_mlstm_c.shape[0])
            xr = x_new.reshape(n_real, D)
            u_small = small_rows(u_s.reshape(n_samp, M_INNER), u_m.reshape(n_meta, M_INNER))
            xs = _resid_matmul(u_small, w_out, xs, SMALL_ROW_TILE)
            ml_p.append((c_r[0], n_r.reshape(B, M_HEADS, M_HD), m_r[0, :, :, 0, 0],
                         xm_tail[:, 8 - (CONV_W - 1):]))
            ml_s.append((n_s.reshape(DB, M_HEADS, M_HD), m_s[0, :, :, 0, 0], xpad_s[:, DT:]))

    sd = state_mlstm_c.dtype
    stack = lambda rows, idx: jnp.stack([r[idx] for r in rows])
    window = lambda cache, idx: jnp.concatenate([cache.astype(F32), stack(swa_s, idx)], axis=2)[:, :, DT:]
    return (xr.reshape(B, T, D), xs[:n_samp].reshape(DB, DT, D),
            stack(swa_p, 0), stack(swa_p, 1), stack(swa_p, 2), stack(swa_p, 3),
            stack(ml_p, 0).astype(sd), stack(ml_p, 1).astype(sd), stack(ml_p, 2).astype(sd), stack(ml_p, 3),
            window(cache_swa_k, 0), window(cache_swa_v, 1),
            c_samp.astype(sd), stack(ml_s, 0).astype(sd), stack(ml_s, 1).astype(sd), stack(ml_s, 2))
```
